```python
import jax
import jax.numpy as jnp
from jax import lax
import numpy as np

D_MODEL = 1024
BATCH = 8
SEQ = 4096
DEPTH = 2

GRID_W = 64
CTX_LEN = 256
N_MOD = 9
D_FF = 2816
RMS_EPS = 1e-6
NEG_INF = -1e30

GLA_HEADS = 4
GLA_DK = 128
GLA_DV = 256
GLA_LR = 16
GLA_TAU = 16.0
GLA_CHUNK = 64

NAT_HEADS = 16
NAT_HD = 64
WIN_R = 8
WIN_C = 16

GQA_HEADS = 8
GQA_KV_HEADS = 2
GQA_HD = 128
Q_BLOCK = 128
ROPE_BASE = 10000.0

N_BRANCH = 3
BRANCH_W = GLA_HEADS * GLA_DV

IN_SPLITS = (
    GLA_HEADS * GLA_DK,
    GLA_HEADS * GLA_DK,
    GLA_HEADS * GLA_DV,
    GLA_HEADS * GLA_DV,
    2 * GLA_LR,
    NAT_HEADS * NAT_HD,
    NAT_HEADS * NAT_HD,
    NAT_HEADS * NAT_HD,
    GQA_HEADS * GQA_HD,
    GQA_KV_HEADS * GQA_HD,
    GQA_KV_HEADS * GQA_HD,
    D_MODEL, D_MODEL, D_MODEL,
)
IN_WIDTH = sum(IN_SPLITS)

kernel_name = 'hybrid_gla_natten_gqa_macaron_dit'


def rms_norm(x, g):
    xf = x.astype(jnp.float32)
    y = xf * lax.rsqrt(jnp.mean(xf * xf, axis=-1, keepdims=True) + RMS_EPS)
    return (y * g.astype(jnp.float32)).astype(x.dtype)


def modulate(h, shift, scale):
    return h * (1.0 + scale) + shift


def swiglu(h, w_up, w_down):
    a, b = jnp.split(h @ w_up, 2, axis=-1)
    return (jax.nn.silu(a) * b) @ w_down


def to_heads(t, n):
    b, l, _ = t.shape
    return t.reshape(b, l, n, -1).transpose(0, 2, 1, 3)


def merge_heads(t):
    b, n, l, d = t.shape
    return t.transpose(0, 2, 1, 3).reshape(b, l, n * d)


def flip(t):
    return jnp.flip(t, axis=2)


def rope_1d(x, pos):
    half = x.shape[-1] // 2
    freqs = ROPE_BASE ** (-jnp.arange(half, dtype=jnp.float32) / half)
    ang = pos.astype(jnp.float32)[:, None] * freqs
    cos, sin = jnp.cos(ang), jnp.sin(ang)
    xf = x.astype(jnp.float32)
    x1, x2 = xf[..., :half], xf[..., half:]
    return jnp.concatenate([x1 * cos - x2 * sin, x2 * cos + x1 * sin], axis=-1).astype(x.dtype)


def rope_2d(x, rows, cols):
    half = x.shape[-1] // 2
    return jnp.concatenate([rope_1d(x[..., :half], rows), rope_1d(x[..., half:], cols)], axis=-1)


def grouped_attention(q, k, v):
    b, h, l, d = q.shape
    kvh = k.shape[1]
    nb = l // Q_BLOCK
    qb = jnp.moveaxis(q.reshape(b, kvh, h // kvh, nb, Q_BLOCK, d), 3, 0)

    def block(qi):
        s = jnp.einsum('bkgqd,bksd->bkgqs', qi, k).astype(jnp.float32)
        p = jax.nn.softmax(s, axis=-1).astype(v.dtype)
        return jnp.einsum('bkgqs,bksd->bkgqd', p, v)

    o = lax.map(block, qb)
    return jnp.moveaxis(o, 0, 3).reshape(b, h, l, d)


def neighbourhood_attention(q, k, v, k_ctx, v_ctx, rpb):
    b, n, l, d = q.shape
    rows = l // GRID_W
    kr = min(WIN_R, rows)
    qg = q.reshape(b, n, rows, GRID_W, d)
    kg = k.reshape(b, n, rows, GRID_W, d)
    vg = v.reshape(b, n, rows, GRID_W, d)
    col = jnp.arange(GRID_W)
    c_start = jnp.clip(col - WIN_C // 2, 0, GRID_W - WIN_C)
    col_mask = (col[None, :] >= c_start[:, None]) & (col[None, :] < c_start[:, None] + WIN_C)
    dc = jnp.clip(col[None, :] - col[:, None] + WIN_C - 1, 0, 2 * WIN_C - 2)
    rpb_c = rpb.astype(jnp.float32)[:, :, dc]

    def row_block(r):
        r_start = jnp.clip(r - kr // 2, 0, rows - kr)
        k_s = lax.dynamic_slice_in_dim(kg, r_start, kr, axis=2)
        v_s = lax.dynamic_slice_in_dim(vg, r_start, kr, axis=2)
        q_r = lax.dynamic_index_in_dim(qg, r, axis=2, keepdims=False)
        dr = r_start + jnp.arange(kr) - r + WIN_R - 1
        bias = rpb_c[:, dr].transpose(0, 2, 1, 3)
        s = jnp.einsum('bhqd,bhikd->bhqik', q_r, k_s).astype(jnp.float32) + bias
        s = jnp.where(col_mask[:, None, :], s, NEG_INF).reshape(b, n, GRID_W, kr * GRID_W)
        s_ctx = jnp.einsum('bhqd,bhcd->bhqc', q_r, k_ctx).astype(jnp.float32)
        p = jax.nn.softmax(jnp.concatenate([s, s_ctx], axis=-1), axis=-1).astype(v.dtype)
        p_lat = p[..., :kr * GRID_W].reshape(b, n, GRID_W, kr, GRID_W)
        return (jnp.einsum('bhqik,bhikd->bhqd', p_lat, v_s)
                + jnp.einsum('bhqc,bhcd->bhqd', p[..., kr * GRID_W:], v_ctx))

    o = lax.map(row_block, jnp.arange(rows))
    return jnp.moveaxis(o, 0, 2).reshape(b, n, l, d)


def gla_chunk_scan(q, k, v, log_a, s0):
    b, n, l, _ = q.shape
    dv = v.shape[-1]
    nc = l // GLA_CHUNK

    def chunks(t):
        return t.astype(jnp.float32).reshape(b, n, nc, GLA_CHUNK, t.shape[-1])

    qc, kc, vc = chunks(q), chunks(k), chunks(v)
    g = jnp.cumsum(chunks(log_a), axis=3)
    g_last = g[:, :, :, -1, :]
    q_e = qc * jnp.exp(g)
    k_in = kc * jnp.exp(-g)
    k_end = kc * jnp.exp(g_last[:, :, :, None, :] - g)
    lower = jnp.tril(jnp.ones((GLA_CHUNK, GLA_CHUNK), dtype=bool))
    att = jnp.where(lower, jnp.einsum('bhnid,bhnjd->bhnij', q_e, k_in), 0.0)
    o_intra = jnp.einsum('bhnij,bhnjv->bhniv', att, vc)

    def step(state, inp):
        q_t, k_t, v_t, gl_t = inp
        o_t = jnp.einsum('bhcd,bhdv->bhcv', q_t, state)
        state = jnp.exp(gl_t)[..., None] * state + jnp.einsum('bhcd,bhcv->bhdv', k_t, v_t)
        return state, o_t

    xs = tuple(jnp.moveaxis(t, 2, 0) for t in (q_e, k_end, vc, g_last))
    _, o_inter = lax.scan(step, s0.astype(jnp.float32), xs)
    return (o_intra + jnp.moveaxis(o_inter, 0, 2)).reshape(b, n, l, dv)


def gla_final_state(k, v, log_a):
    g = jnp.cumsum(log_a.astype(jnp.float32), axis=2)
    k_dec = k.astype(jnp.float32) * jnp.exp(g[:, :, -1:, :] - g)
    return jnp.einsum('bhld,bhlv->bhdv', k_dec, v.astype(jnp.float32))


def gla_bidir(q, k, v, la_f, la_b, s_f, s_b):
    o_f = gla_chunk_scan(q, k, v, la_f, s_f)
    o_b = flip(gla_chunk_scan(flip(q), flip(k), flip(v), flip(la_b), s_b))
    return o_f + o_b


def gla_log_decay(fg, w2, b2, direction):
    z = fg[..., direction * GLA_LR:(direction + 1) * GLA_LR] @ w2[direction] + b2[direction]
    return to_heads(jax.nn.log_sigmoid(z.astype(jnp.float32)) / GLA_TAU, GLA_HEADS)


def gla_branch(p, pc, fg_w2, fg_b, gain, need_ctx):
    k = to_heads(p[1], GLA_HEADS)
    v = to_heads(p[2], GLA_HEADS)
    kc = to_heads(pc[1], GLA_HEADS)
    vc = to_heads(pc[2], GLA_HEADS)
    lac_f = gla_log_decay(pc[4], fg_w2, fg_b, 0)
    lac_b = gla_log_decay(pc[4], fg_w2, fg_b, 1)
    s_f = gla_final_state(kc, vc, lac_f)
    s_b = gla_final_state(flip(kc), flip(vc), flip(lac_b))
    q = to_heads(p[0], GLA_HEADS) * GLA_DK ** -0.5
    o = gla_bidir(q, k, v, gla_log_decay(p[4], fg_w2, fg_b, 0), gla_log_decay(p[4], fg_w2, fg_b, 1), s_f, s_b)
    y = merge_heads(rms_norm(o, gain).astype(p[3].dtype)) * jax.nn.silu(p[3])
    if not need_ctx:
        return y, None
    qc = to_heads(pc[0], GLA_HEADS) * GLA_DK ** -0.5
    zero = jnp.zeros_like(s_f)
    oc = gla_bidir(qc, kc, vc, lac_f, lac_b, zero, zero)
    return y, merge_heads(rms_norm(oc, gain).astype(pc[3].dtype)) * jax.nn.silu(pc[3])


def nat_branch(p, pc, q_gain, k_gain, rpb, need_ctx):
    k = rms_norm(to_heads(p[1], NAT_HEADS), k_gain)
    v = to_heads(p[2], NAT_HEADS)
    kc = rms_norm(to_heads(pc[1], NAT_HEADS), k_gain)
    vc = to_heads(pc[2], NAT_HEADS)
    q = rms_norm(to_heads(p[0], NAT_HEADS), q_gain) * NAT_HD ** -0.5
    y = merge_heads(neighbourhood_attention(q, k, v, kc, vc, rpb))
    if not need_ctx:
        return y, None
    qc = rms_norm(to_heads(pc[0], NAT_HEADS), q_gain) * NAT_HD ** -0.5
    return y, merge_heads(grouped_attention(qc, kc, vc))


def gqa_branch(p, pc, q_gain, k_gain, rows, cols, need_ctx):
    k = rope_2d(rms_norm(to_heads(p[1], GQA_KV_HEADS), k_gain), rows, cols)
    v = to_heads(p[2], GQA_KV_HEADS)
    kc = rms_norm(to_heads(pc[1], GQA_KV_HEADS), k_gain)
    vc = to_heads(pc[2], GQA_KV_HEADS)
    q = rope_2d(rms_norm(to_heads(p[0], GQA_HEADS), q_gain), rows, cols) * GQA_HD ** -0.5
    y = merge_heads(grouped_attention(q, jnp.concatenate([kc, k], axis=2), jnp.concatenate([vc, v], axis=2)))
    if not need_ctx:
        return y, None
    qc = rms_norm(to_heads(pc[0], GQA_HEADS), q_gain) * GQA_HD ** -0.5
    return y, merge_heads(grouped_attention(qc, kc, vc))


def token_mixer(u, uc, rows, cols, w_in, gla_fg_w2, gla_fg_b, gla_norm_g, nat_q_norm, nat_k_norm,
                nat_rpb, gqa_q_norm, gqa_k_norm, w_branch, w_out, need_ctx):
    split_at = [int(s) for s in np.cumsum(IN_SPLITS)[:-1]]
    p = jnp.split(u @ w_in, split_at, axis=-1)
    pc = jnp.split(uc @ w_in, split_at, axis=-1)
    outs = (
        gla_branch(p[0:5], pc[0:5], gla_fg_w2, gla_fg_b, gla_norm_g, need_ctx),
        nat_branch(p[5:8], pc[5:8], nat_q_norm, nat_k_norm, nat_rpb, need_ctx),
        gqa_branch(p[8:11], pc[8:11], gqa_q_norm, gqa_k_norm, rows, cols, need_ctx),
    )

    def merge(ys, gates):
        z = jax.nn.sigmoid(gates[0]) * (ys[0] @ w_branch[0])
        for i in range(1, N_BRANCH):
            z = z + jax.nn.sigmoid(gates[i]) * (ys[i] @ w_branch[i])
        return z @ w_out

    y = merge([o[0] for o in outs], p[11:14])
    yc = merge([o[1] for o in outs], pc[11:14]) if need_ctx else None
    return y, yc


def setup_inputs(seed: int = 0) -> dict:
    key = jax.random.key(seed)
    ks = jax.random.split(key, 20)
    f32 = jnp.float32

    def dense(k, shape, fan_in):
        return jax.random.normal(k, shape, f32) * fan_in ** -0.5

    def gain(k, shape):
        return 1.0 + 0.1 * jax.random.normal(k, shape, f32)

    return {
        'x': jax.random.normal(ks[0], (BATCH, SEQ, D_MODEL), f32),
        'c': jax.random.normal(ks[1], (BATCH, D_MODEL), f32),
        'ctx': jax.random.normal(ks[2], (BATCH, CTX_LEN, D_MODEL), f32),
        'c_ctx': jax.random.normal(ks[3], (D_MODEL,), f32),
        'w_mod': dense(ks[4], (DEPTH, D_MODEL, N_MOD * D_MODEL), D_MODEL),
        'b_mod': 0.01 * jax.random.normal(ks[5], (DEPTH, N_MOD * D_MODEL), f32),
        'norm_g': gain(ks[6], (DEPTH, 3, D_MODEL)),
        'ffn_w_in': dense(ks[7], (DEPTH, 2, D_MODEL, 2 * D_FF), D_MODEL),
        'ffn_w_out': dense(ks[8], (DEPTH, 2, D_FF, D_MODEL), D_FF),
        'w_in': dense(ks[9], (DEPTH, D_MODEL, IN_WIDTH), D_MODEL),
        'gla_fg_w2': dense(ks[10], (DEPTH, 2, GLA_LR, GLA_HEADS * GLA_DK), GLA_LR),
        'gla_fg_b': 0.1 * jax.random.normal(ks[11], (DEPTH, 2, GLA_HEADS * GLA_DK), f32),
        'gla_norm_g': gain(ks[12], (DEPTH, GLA_DV)),
        'nat_q_norm': gain(ks[13], (DEPTH, NAT_HD)),
        'nat_k_norm': gain(ks[14], (DEPTH, NAT_HD)),
        'nat_rpb': 0.2 * jax.random.normal(ks[15], (DEPTH, NAT_HEADS, 2 * WIN_R - 1, 2 * WIN_C - 1), f32),
        'gqa_q_norm': gain(ks[16], (DEPTH, GQA_HD)),
        'gqa_k_norm': gain(ks[17], (DEPTH, GQA_HD)),
        'w_branch': dense(ks[18], (DEPTH, N_BRANCH, BRANCH_W, D_MODEL), BRANCH_W),
        'w_out': dense(ks[19], (DEPTH, D_MODEL, D_MODEL), D_MODEL),
    }


def reference(x, c, ctx, c_ctx, w_mod, b_mod, norm_g, ffn_w_in, ffn_w_out, w_in, gla_fg_w2, gla_fg_b,
              gla_norm_g, nat_q_norm, nat_k_norm, nat_rpb, gqa_q_norm, gqa_k_norm, w_branch, w_out):
    n_tok = x.shape[1]
    t = jnp.arange(n_tok, dtype=jnp.int32)
    rows = t // GRID_W
    cols = t % GRID_W
    h = ctx
    s_c = jax.nn.silu(c)
    s_cc = jax.nn.silu(c_ctx)
    for l in range(DEPTH):
        need_ctx = l < DEPTH - 1
        sh1, sc1, g1, sh2, sc2, g2, sh3, sc3, g3 = jnp.split((s_c @ w_mod[l] + b_mod[l])[:, None, :], N_MOD, axis=-1)
        ch1, cs1, cg1, ch2, cs2, cg2, ch3, cs3, cg3 = jnp.split(s_cc @ w_mod[l] + b_mod[l], N_MOD, axis=-1)
        x = x + 0.5 * g1 * swiglu(modulate(rms_norm(x, norm_g[l, 0]), sh1, sc1), ffn_w_in[l, 0], ffn_w_out[l, 0])
        h = h + 0.5 * cg1 * swiglu(modulate(rms_norm(h, norm_g[l, 0]), ch1, cs1), ffn_w_in[l, 0], ffn_w_out[l, 0])
        y, yc = token_mixer(
            modulate(rms_norm(x, norm_g[l, 1]), sh2, sc2), modulate(rms_norm(h, norm_g[l, 1]), ch2, cs2),
            rows, cols, w_in[l], gla_fg_w2[l], gla_fg_b[l], gla_norm_g[l], nat_q_norm[l], nat_k_norm[l],
            nat_rpb[l], gqa_q_norm[l], gqa_k_norm[l], w_branch[l], w_out[l], need_ctx)
        x = x + g2 * y
        x = x + 0.5 * g3 * swiglu(modulate(rms_norm(x, norm_g[l, 2]), sh3, sc3), ffn_w_in[l, 1], ffn_w_out[l, 1])
        if need_ctx:
            h = h + cg2 * yc
            h = h + 0.5 * cg3 * swiglu(modulate(rms_norm(h, norm_g[l, 2]), ch3, cs3), ffn_w_in[l, 1], ffn_w_out[l, 1])
    return x
```

```python
import functools

import numpy as np
import jax
import jax.numpy as jnp
from jax import lax
from jax.experimental import pallas as pl
from jax.experimental.pallas import tpu as pltpu

F32 = jnp.float32
BF16 = jnp.bfloat16

D_MODEL = 1024
GRID_W = 64
N_MOD = 9
D_FF = 2816
RMS_EPS = 1e-6
NEG_INF = -1e30

GLA_HEADS = 4
GLA_DK = 128
GLA_DV = 256
GLA_LR = 16
GLA_TAU = 16.0
GLA_CHUNK = 128

NAT_HEADS = 16
NAT_HD = 64
WIN_R = 8
WIN_C = 16
NAT_QROWS = 4
NAT_KBLOCKS = 3

GQA_HEADS = 8
GQA_KV_HEADS = 2
GQA_HD = 128
ROPE_BASE = 10000.0

TM = 256
LANES = 128
MXU = 256
VMEM_LIMIT = 56 * 1024 * 1024


def _dot(a, b):
    return jnp.dot(a, b, preferred_element_type=F32)


def _dot_nt(a, b):
    return lax.dot_general(a, b, (((1,), (1,)), ((), ())), preferred_element_type=F32)


def _dot_tn(a, b):
    return lax.dot_general(a, b, (((0,), (0,)), ((), ())), preferred_element_type=F32)


def _sigmoid(x):
    return 1.0 / (1.0 + jnp.exp(-x))


def _log_sigmoid(x):
    return -(jnp.maximum(-x, 0.0) + jnp.log1p(jnp.exp(-jnp.abs(x))))


def _norm_mod(x, g, shift, scale):
    y = x * lax.rsqrt(jnp.mean(x * x, axis=-1, keepdims=True) + RMS_EPS) * g
    return y * (1.0 + scale) + shift


def _split_dot(x, m):
    hi = x.astype(BF16)
    lo = (x - hi.astype(F32)).astype(BF16)
    return _dot(hi, m) + _dot(lo, m)


def _head_norm(t, gain, ones_bd, head_dim):
    sq = t * t
    parts = [_split_dot(sq[:, i:i + MXU], ones_bd) for i in range(0, t.shape[1], MXU)]
    ss = parts[0] if len(parts) == 1 else jnp.concatenate(parts, axis=1)
    return t * lax.rsqrt(ss * (1.0 / head_dim) + RMS_EPS) * gain


def _params(semantics):
    return pltpu.CompilerParams(dimension_semantics=semantics, vmem_limit_bytes=VMEM_LIMIT)


def _whole(arr, ngrid):
    zeros = (0,) * arr.ndim
    return pl.BlockSpec(arr.shape, lambda *_: zeros, pipeline_mode=pl.Buffered(1))


def _row_spec(off, width):
    return pl.BlockSpec((1, TM, width), lambda b, j: (b, j + off, 0))


def _mod_spec(col, off, ctx_row):
    return pl.BlockSpec((1, 1, D_MODEL), lambda b, j: (jnp.where(j + off == 0, ctx_row, b), 0, col))


def _mod_kernel(c_ref, w_ref, b_ref, o_ref):
    c = c_ref[...]
    s = (c * _sigmoid(c)).astype(BF16)
    o_ref[0] = _dot(s, w_ref[0].astype(BF16)) + b_ref[0]


def _mod_table(cvec, w_mod, b_mod):
    depth = w_mod.shape[0]
    rows = cvec.shape[0]
    return pl.pallas_call(
        _mod_kernel,
        grid=(depth, N_MOD),
        in_specs=[pl.BlockSpec((rows, D_MODEL), lambda l, n: (0, 0)),
                  pl.BlockSpec((1, D_MODEL, D_MODEL), lambda l, n: (l, 0, n)),
                  pl.BlockSpec((1, 1, D_MODEL), lambda l, n: (l, 0, n))],
        out_specs=pl.BlockSpec((1, rows, D_MODEL), lambda l, n: (l, 0, n)),
        out_shape=jax.ShapeDtypeStruct((depth, rows, N_MOD * D_MODEL), F32),
        compiler_params=_params(("arbitrary", "arbitrary")),
        name="mod_table",
    )(cvec, w_mod, b_mod.reshape(depth, 1, N_MOD * D_MODEL))


def _ffn_kernel(x_ref, sh_ref, sc_ref, gt_ref, g_ref, wa_ref, wb_ref, wd_ref, o_ref):
    x = x_ref[0]
    u = _norm_mod(x, g_ref[...], sh_ref[0], sc_ref[0]).astype(BF16)
    a = _dot(u, wa_ref[...])
    b = _dot(u, wb_ref[...])
    h = (a * _sigmoid(a) * b).astype(BF16)
    o_ref[0] = x + (0.5 * gt_ref[0]) * _dot(h, wd_ref[...])


def _ffn(xall, mod, g, wa, wb, wd, sub, off, out_off, out_tokens):
    bsz, t, _ = xall.shape
    ctx_row = mod.shape[0] - 1
    return pl.pallas_call(
        _ffn_kernel,
        grid=(bsz, t // TM - off),
        in_specs=[_row_spec(off, D_MODEL),
                  _mod_spec(3 * sub, off, ctx_row), _mod_spec(3 * sub + 1, off, ctx_row),
                  _mod_spec(3 * sub + 2, off, ctx_row),
                  _whole(g, 2), _whole(wa, 2), _whole(wb, 2), _whole(wd, 2)],
        out_specs=_row_spec(out_off, D_MODEL),
        out_shape=jax.ShapeDtypeStruct((bsz, out_tokens, D_MODEL), F32),
        compiler_params=_params(("parallel", "parallel")),
        name="ffn",
    )(xall, mod, mod, mod, g, wa, wb, wd)


def _proj_gla_kernel(x_ref, sh_ref, sc_ref, g_ref, w_ref, q_ref, k_ref, v_ref, og_ref, fg_ref):
    u = _norm_mod(x_ref[0], g_ref[...], sh_ref[0], sc_ref[0]).astype(BF16)
    p = _dot(u, w_ref[...])
    nqk = GLA_HEADS * GLA_DK
    nv = GLA_HEADS * GLA_DV
    q_ref[0] = (p[:, :nqk] * GLA_DK ** -0.5).astype(BF16)
    k_ref[0] = p[:, nqk:2 * nqk].astype(BF16)
    v_ref[0] = p[:, 2 * nqk:2 * nqk + nv].astype(BF16)
    og = p[:, 2 * nqk + nv:2 * nqk + 2 * nv]
    og_ref[0] = (og * _sigmoid(og)).astype(BF16)
    fg_ref[0] = p[:, 2 * nqk + 2 * nv:]


def _proj_nat_kernel(x_ref, sh_ref, sc_ref, g_ref, wqk_ref, wvt_ref, gq_ref, gk_ref, ones_ref,
                     q_ref, k_ref, vt_ref):
    u = _norm_mod(x_ref[0], g_ref[...], sh_ref[0], sc_ref[0]).astype(BF16)
    p = _dot(u, wqk_ref[...])
    n = NAT_HEADS * NAT_HD
    ones_bd = ones_ref[...]
    q_ref[0] = (_head_norm(p[:, :n], gq_ref[...], ones_bd, NAT_HD) * NAT_HD ** -0.5).astype(BF16)
    k_ref[0] = _head_norm(p[:, n:], gk_ref[...], ones_bd, NAT_HD).astype(BF16)
    vt_ref[0] = _dot_nt(wvt_ref[...], u).astype(BF16)


def _rope(t, cos, sin, lane):
    partner = jnp.where(lane % (GQA_HD // 2) < GQA_HD // 4,
                        pltpu.roll(t, GQA_HD - GQA_HD // 4, axis=1), pltpu.roll(t, GQA_HD // 4, axis=1))
    return t * cos + partner * sin


def _proj_gqa_kernel(x_ref, sh_ref, sc_ref, g_ref, wqk_ref, wvt_ref, wg_ref, gq_ref, gk_ref, ones_ref,
                     cos_ref, sin_ref, q_ref, k_ref, vt_ref, sg_ref):
    u = _norm_mod(x_ref[0], g_ref[...], sh_ref[0], sc_ref[0]).astype(BF16)
    p = _dot(u, wqk_ref[...])
    nq = GQA_HEADS * GQA_HD
    ones_bd = ones_ref[...]
    qn = _head_norm(p[:, :nq], gq_ref[...], ones_bd, GQA_HD)
    kn = _head_norm(p[:, nq:], gk_ref[...], ones_bd, GQA_HD)
    cos = cos_ref[...]
    sin = sin_ref[...]
    lane = lax.broadcasted_iota(jnp.int32, (TM, GQA_HD), 1)
    for h in range(GQA_HEADS):
        sl = slice(h * GQA_HD, (h + 1) * GQA_HD)
        q_ref[0, :, sl] = (_rope(qn[:, sl], cos, sin, lane) * GQA_HD ** -0.5).astype(BF16)
    for h in range(GQA_KV_HEADS):
        sl = slice(h * GQA_HD, (h + 1) * GQA_HD)
        k_ref[0, :, sl] = _rope(kn[:, sl], cos, sin, lane).astype(BF16)
    vt_ref[0] = _dot_nt(wvt_ref[...], u).astype(BF16)
    sg_ref[0] = _sigmoid(_dot(u, wg_ref[...])).astype(BF16)


def _col_spec(rows, off):
    return pl.BlockSpec((1, rows, TM), lambda b, j: (b, 0, j + off))


def _projections(xall, mod, g, lw, rope_cos, rope_sin):
    bsz, t, _ = xall.shape
    ctx_row = mod.shape[0] - 1
    grid = (bsz, t // TM)
    common = [_row_spec(0, D_MODEL), _mod_spec(3, 0, ctx_row), _mod_spec(4, 0, ctx_row), _whole(g, 2)]
    sds = jax.ShapeDtypeStruct
    nqk = GLA_HEADS * GLA_DK
    nv = GLA_HEADS * GLA_DV

    gla = pl.pallas_call(
        _proj_gla_kernel, grid=grid,
        in_specs=common + [_whole(lw["w_gla"], 2)],
        out_specs=[_row_spec(0, nqk), _row_spec(0, nqk), _row_spec(0, nv), _row_spec(0, nv),
                   _row_spec(0, LANES)],
        out_shape=[sds((bsz, t, nqk), BF16), sds((bsz, t, nqk), BF16), sds((bsz, t, nv), BF16),
                   sds((bsz, t, nv), BF16), sds((bsz, t, LANES), F32)],
        compiler_params=_params(("parallel", "parallel")), name="proj_gla",
    )(xall, mod, mod, g, lw["w_gla"])

    n = NAT_HEADS * NAT_HD
    nat = pl.pallas_call(
        _proj_nat_kernel, grid=grid,
        in_specs=common + [_whole(lw["w_nat_qk"], 2), _whole(lw["w_nat_vt"], 2), _whole(lw["nat_gq"], 2),
                           _whole(lw["nat_gk"], 2), _whole(lw["ones64"], 2)],
        out_specs=[_row_spec(0, n), _row_spec(0, n), _col_spec(n, 0)],
        out_shape=[sds((bsz, t, n), BF16), sds((bsz, t, n), BF16), sds((bsz, n, t), BF16)],
        compiler_params=_params(("parallel", "parallel")), name="proj_nat",
    )(xall, mod, mod, g, lw["w_nat_qk"], lw["w_nat_vt"], lw["nat_gq"], lw["nat_gk"], lw["ones64"])

    nq = GQA_HEADS * GQA_HD
    nkv = GQA_KV_HEADS * GQA_HD
    tab = pl.BlockSpec((TM, GQA_HD), lambda b, j: (j, 0))
    gqa = pl.pallas_call(
        _proj_gqa_kernel, grid=grid,
        in_specs=common + [_whole(lw["w_gqa_qk"], 2), _whole(lw["w_gqa_vt"], 2), _whole(lw["w_gates"], 2),
                           _whole(lw["gqa_gq"], 2), _whole(lw["gqa_gk"], 2), _whole(lw["ones128"], 2),
                           tab, tab],
        out_specs=[_row_spec(0, nq), _row_spec(0, nkv), _col_spec(nkv, 0), _row_spec(0, 3 * D_MODEL)],
        out_shape=[sds((bsz, t, nq), BF16), sds((bsz, t, nkv), BF16), sds((bsz, nkv, t), BF16),
                   sds((bsz, t, 3 * D_MODEL), BF16)],
        compiler_params=_params(("parallel", "parallel")), name="proj_gqa",
    )(xall, mod, mod, g, lw["w_gqa_qk"], lw["w_gqa_vt"], lw["w_gates"], lw["gqa_gq"], lw["gqa_gk"],
      lw["ones128"], rope_cos, rope_sin)
    return gla, nat, gqa


def _gla_direction(n, forward, q_ref, k_ref, v_ref, fg_ref, w2_ref, b2_ref, st_ref, o_ref, tri, tri_bf):
    c = GLA_CHUNK
    rows = pl.ds(pl.multiple_of(n * c, c), c)
    z = _dot(fg_ref[0, rows, :].astype(BF16), w2_ref[...]) + b2_ref[...]
    la = _log_sigmoid(z) * (1.0 / GLA_TAU)
    hi = la.astype(BF16)
    r1 = la - hi.astype(F32)
    mid = r1.astype(BF16)
    lo = (r1 - mid.astype(F32)).astype(BF16)
    cs = _dot(tri_bf, jnp.concatenate([hi, mid, lo], axis=1))
    g = cs[:, :GLA_DK] + cs[:, GLA_DK:2 * GLA_DK] + cs[:, 2 * GLA_DK:]
    if forward:
        g_mid = g[c // 2 - 1:c // 2, :]
        g_edge = g[c - 1:c, :]
    else:
        g_mid = g[c // 2:c // 2 + 1, :]
        g_edge = g[0:1, :]
    q = q_ref[0, rows, :].astype(F32)
    k = k_ref[0, rows, :].astype(F32)
    v = v_ref[0, rows, :]
    q_in = (q * jnp.exp(g - g_mid)).astype(BF16)
    k_in = (k * jnp.exp(g_mid - g)).astype(BF16)
    q_x = (q * jnp.exp(g)).astype(BF16)
    k_end = (k * jnp.exp(g_edge - g)).astype(BF16)
    att = jnp.where(tri, _dot_nt(q_in, k_in), 0.0).astype(BF16)
    st = st_ref[...]
    o_ref[rows, :] = _dot(att, v) + _dot_nt(q_x, st.astype(BF16))
    st_ref[...] = st * jnp.exp(g_edge) + _dot_tn(v, k_end)


def _gla_kernel(q_ref, k_ref, v_ref, og_ref, fg_ref, w2f_ref, w2b_ref, b2f_ref, b2b_ref, gain_ref, y_ref,
                of_ref, ob_ref, sf_ref, sb_ref, *, n_chunks, n_ctx_chunks):
    c = GLA_CHUNK
    ri = lax.broadcasted_iota(jnp.int32, (c, c), 0)
    ci = lax.broadcasted_iota(jnp.int32, (c, c), 1)
    lower = ri >= ci
    upper = ci >= ri
    lower_bf = lower.astype(BF16)
    upper_bf = upper.astype(BF16)
    sf_ref[...] = jnp.zeros_like(sf_ref)
    sb_ref[...] = jnp.zeros_like(sb_ref)

    def scan(s, carry):
        nb = jnp.where(s < n_ctx_chunks, n_ctx_chunks - 1 - s, n_chunks - 1 - (s - n_ctx_chunks))
        _gla_direction(s, True, q_ref, k_ref, v_ref, fg_ref, w2f_ref, b2f_ref, sf_ref, of_ref, lower, lower_bf)
        _gla_direction(nb, False, q_ref, k_ref, v_ref, fg_ref, w2b_ref, b2b_ref, sb_ref, ob_ref, upper, upper_bf)
        return carry

    lax.fori_loop(0, n_chunks, scan, 0)

    def finish(n, carry):
        rows = pl.ds(pl.multiple_of(n * c, c), c)
        o = of_ref[rows, :] + ob_ref[rows, :]
        y = o * lax.rsqrt(jnp.mean(o * o, axis=-1, keepdims=True) + RMS_EPS) * gain_ref[...]
        y_ref[0, rows, :] = (y * og_ref[0, rows, :].astype(F32)).astype(BF16)
        return carry

    lax.fori_loop(0, n_chunks, finish, 0)


def _gla(q, k, v, og, fg, lw, ctx_len):
    bsz, t, _ = q.shape
    seq_spec = lambda w: pl.BlockSpec((1, t, w), lambda b, h: (b, 0, h))
    head_w = pl.BlockSpec((LANES, GLA_DK), lambda b, h: (0, h))
    head_b = pl.BlockSpec((1, GLA_DK), lambda b, h: (0, h))
    kern = functools.partial(_gla_kernel, n_chunks=t // GLA_CHUNK, n_ctx_chunks=ctx_len // GLA_CHUNK)
    return pl.pallas_call(
        kern, grid=(bsz, GLA_HEADS),
        in_specs=[seq_spec(GLA_DK), seq_spec(GLA_DK), seq_spec(GLA_DV), seq_spec(GLA_DV),
                  pl.BlockSpec((1, t, LANES), lambda b, h: (b, 0, 0)),
                  head_w, head_w, head_b, head_b, pl.BlockSpec((1, GLA_DV), lambda b, h: (0, 0))],
        out_specs=seq_spec(GLA_DV),
        out_shape=jax.ShapeDtypeStruct((bsz, t, GLA_HEADS * GLA_DV), BF16),
        scratch_shapes=[pltpu.VMEM((t, GLA_DV), F32), pltpu.VMEM((t, GLA_DV), F32),
                        pltpu.VMEM((GLA_DV, GLA_DK), F32), pltpu.VMEM((GLA_DV, GLA_DK), F32)],
        compiler_params=_params(("parallel", "parallel")), name="gla",
    )(q, k, v, og, fg, lw["w2f"], lw["w2b"], lw["b2f"], lw["b2b"], lw["gla_gain"])


NAT_GROUP = 4


def _softmax_t(st, vt):
    m = jnp.max(st, axis=0, keepdims=True)
    p = jnp.exp(st - m)
    l = jnp.sum(p, axis=0, keepdims=True)
    return _dot(vt, p.astype(BF16)) / l


def _nat_kernel(q_ref, k0_ref, k1_ref, k2_ref, kc_ref, v0_ref, v1_ref, v2_ref, vc_ref, bias_ref, o_ref, *, off):
    is_ctx = pl.program_id(0) + off == 0
    lane = lax.broadcasted_iota(jnp.int32, (TM, LANES), 1)
    row = lax.broadcasted_iota(jnp.int32, (LANES, TM), 0)

    def pair(hp, keys, vals, with_bias):
        sl = slice(hp * LANES, (hp + 1) * LANES)
        q = q_ref[0, :, sl]
        kcat = jnp.concatenate([r[0, :, sl] for r in keys], axis=0) if len(keys) > 1 else keys[0][0, :, sl]
        vcat = jnp.concatenate([r[0, sl, :] for r in vals], axis=1) if len(vals) > 1 else vals[0][0, sl, :]
        outs = []
        for e in range(2):
            qm = jnp.where((lane >= NAT_HD) if e else (lane < NAT_HD), q, jnp.zeros_like(q))
            st = _dot_nt(kcat, qm)
            if with_bias:
                nb = NAT_KBLOCKS * TM
                st = jnp.concatenate([st[:nb] + bias_ref[0, 2 * hp + e], st[nb:]], axis=0)
            outs.append(_softmax_t(st, vcat))
        ot = jnp.where(row < NAT_HD, outs[0], outs[1])
        o_ref[0, :, sl] = ot.T.astype(BF16)

    @pl.when(is_ctx)
    def _():
        for hp in range(NAT_GROUP):
            pair(hp, [kc_ref], [vc_ref], False)

    @pl.when(jnp.logical_not(is_ctx))
    def _():
        for hp in range(NAT_GROUP):
            pair(hp, [k0_ref, k1_ref, k2_ref, kc_ref], [v0_ref, v1_ref, v2_ref, vc_ref], True)


def _nat(q, k, vt, bias, need_ctx):
    bsz, t, n = q.shape
    off = 0 if need_ctx else 1
    nblk = t // TM - 1
    gw = NAT_GROUP * LANES
    ngroups = n // gw

    def kblock(qi, i):
        j = qi + off - 1
        return 1 + jnp.clip(j - 1, 0, nblk - NAT_KBLOCKS) + i

    def bias_class(qi):
        j = qi + off - 1
        return jnp.where(j <= 0, 0, jnp.where(j == nblk - 1, 2, 1))

    kspec = lambda i: pl.BlockSpec((1, TM, gw), lambda qi, hg, b: (b, kblock(qi, i), hg))
    vspec = lambda i: pl.BlockSpec((1, gw, TM), lambda qi, hg, b: (b, hg, kblock(qi, i)))
    return pl.pallas_call(
        functools.partial(_nat_kernel, off=off),
        grid=(t // TM - off, ngroups, bsz),
        in_specs=[pl.BlockSpec((1, TM, gw), lambda qi, hg, b: (b, qi + off, hg)),
                  kspec(0), kspec(1), kspec(2),
                  pl.BlockSpec((1, TM, gw), lambda qi, hg, b: (b, 0, hg)),
                  vspec(0), vspec(1), vspec(2),
                  pl.BlockSpec((1, gw, TM), lambda qi, hg, b: (b, hg, 0)),
                  pl.BlockSpec((1, 2 * NAT_GROUP, NAT_KBLOCKS * TM, TM),
                               lambda qi, hg, b: (bias_class(qi), hg, 0, 0))],
        out_specs=pl.BlockSpec((1, TM, gw), lambda qi, hg, b: (b, qi + off, hg)),
        out_shape=jax.ShapeDtypeStruct((bsz, t, n), BF16),
        compiler_params=_params(("parallel", "parallel", "parallel")), name="nat",
    )(q, k, k, k, k, vt, vt, vt, vt, bias)


def _nat_bias_tables(rpb, rows):
    nblk = rows // NAT_QROWS
    kr_win = min(WIN_R, rows)
    col = np.arange(GRID_W)
    c_start = np.clip(col - WIN_C // 2, 0, GRID_W - WIN_C)
    cmask = (col[:, None] >= c_start[None, :]) & (col[:, None] < c_start[None, :] + WIN_C)
    dc = np.clip(col[:, None] - col[None, :] + WIN_C - 1, 0, 2 * WIN_C - 2)
    drs, rmasks = [], []
    for jblk in (0, 1, nblk - 1):
        base = int(np.clip(jblk - 1, 0, nblk - NAT_KBLOCKS))
        r = NAT_QROWS * jblk + np.arange(NAT_QROWS)
        r_start = np.clip(r - kr_win // 2, 0, rows - kr_win)
        kr = NAT_QROWS * base + np.arange(NAT_KBLOCKS * NAT_QROWS)
        rmasks.append((kr[:, None] >= r_start[None, :]) & (kr[:, None] < r_start[None, :] + kr_win))
        drs.append(np.clip(kr[:, None] - r[None, :] + WIN_R - 1, 0, 2 * WIN_R - 2))
    dr = np.stack(drs)
    rmask = np.stack(rmasks)
    a = rpb.astype(F32)[:, :, dc]
    t = a[:, dr]
    t = jnp.transpose(t, (1, 0, 2, 4, 3, 5))
    mask = rmask[:, None, :, None, :, None] & cmask[None, None, None, :, None, :]
    t = jnp.where(mask, t, NEG_INF)
    nk = NAT_KBLOCKS * NAT_QROWS * GRID_W
    return t.reshape(3, rpb.shape[0], nk, NAT_QROWS * GRID_W)


def _gqa_kernel(q_ref, k_ref, vt_ref, o_ref, m_ref, l_ref, acc_ref, *, off, n_kv):
    is_ctx = pl.program_id(2) + off == 0
    group = GQA_HEADS // GQA_KV_HEADS
    qs = jnp.concatenate([q_ref[0, :, g * GQA_HD:(g + 1) * GQA_HD] for g in range(group)], axis=0)
    m_ref[...] = jnp.full_like(m_ref, NEG_INF)
    l_ref[...] = jnp.zeros_like(l_ref)
    acc_ref[...] = jnp.zeros_like(acc_ref)

    def kv_step(j, carry):
        start = pl.multiple_of(j * TM, TM)
        st = _dot_nt(k_ref[0, pl.ds(start, TM), :], qs)
        m_old = m_ref[...]
        m_new = jnp.maximum(m_old, jnp.max(st, axis=0, keepdims=True))
        alpha = jnp.exp(m_old - m_new)
        p = jnp.exp(st - m_new)
        l_ref[...] = alpha * l_ref[...] + jnp.sum(p, axis=0, keepdims=True)
        acc_ref[...] = alpha * acc_ref[...] + _dot(vt_ref[0, :, pl.ds(start, TM)], p.astype(BF16))
        m_ref[...] = m_new
        return carry

    lax.fori_loop(0, jnp.where(is_ctx, 1, n_kv), kv_step, 0)
    ot = acc_ref[...] / l_ref[...]
    for g in range(group):
        o_ref[0, :, g * GQA_HD:(g + 1) * GQA_HD] = ot[:, g * TM:(g + 1) * TM].T.astype(BF16)


def _gqa(q, k, vt, need_ctx):
    bsz, t, nq = q.shape
    off = 0 if need_ctx else 1
    group = GQA_HEADS // GQA_KV_HEADS
    gw = group * GQA_HD
    return pl.pallas_call(
        functools.partial(_gqa_kernel, off=off, n_kv=t // TM),
        grid=(bsz, GQA_KV_HEADS, t // TM - off),
        in_specs=[pl.BlockSpec((1, TM, gw), lambda b, h, qi: (b, qi + off, h)),
                  pl.BlockSpec((1, t, GQA_HD), lambda b, h, qi: (b, 0, h)),
                  pl.BlockSpec((1, GQA_HD, t), lambda b, h, qi: (b, h, 0))],
        out_specs=pl.BlockSpec((1, TM, gw), lambda b, h, qi: (b, qi + off, h)),
        out_shape=jax.ShapeDtypeStruct((bsz, t, nq), BF16),
        scratch_shapes=[pltpu.VMEM((1, group * TM), F32), pltpu.VMEM((1, group * TM), F32),
                        pltpu.VMEM((GQA_HD, group * TM), F32)],
        compiler_params=_params(("parallel", "parallel", "arbitrary")), name="gqa",
    )(q, k, vt)


def _merge_kernel(x_ref, gt_ref, y0_ref, y1_ref, y2_ref, sg_ref, wb_ref, wo_ref, o_ref):
    z = None
    for i, y_ref in enumerate((y0_ref, y1_ref, y2_ref)):
        zi = sg_ref[0, :, i * D_MODEL:(i + 1) * D_MODEL].astype(F32) * _dot(y_ref[0], wb_ref[i])
        z = zi if z is None else z + zi
    o_ref[0] = x_ref[0] + gt_ref[0] * _dot(z.astype(BF16), wo_ref[...])


def _merge(xall, mod, ys, sg, wb, wo, off):
    bsz, t, _ = xall.shape
    ctx_row = mod.shape[0] - 1
    return pl.pallas_call(
        _merge_kernel,
        grid=(bsz, t // TM - off),
        in_specs=[_row_spec(off, D_MODEL), _mod_spec(5, off, ctx_row),
                  _row_spec(off, D_MODEL), _row_spec(off, D_MODEL), _row_spec(off, D_MODEL),
                  _row_spec(off, 3 * D_MODEL), _whole(wb, 2), _whole(wo, 2)],
        out_specs=_row_spec(off, D_MODEL),
        out_shape=jax.ShapeDtypeStruct((bsz, t, D_MODEL), F32),
        compiler_params=_params(("parallel", "parallel")), name="merge",
    )(xall, mod, ys[0], ys[1], ys[2], sg, wb, wo)


def _block_diag_ones(head_dim):
    i = np.arange(MXU)
    return jnp.asarray((i[:, None] // head_dim) == (i[None, :] // head_dim), dtype=BF16)


def _rope_tables(ctx_len, seq):
    quarter = GQA_HD // 4
    freqs = ROPE_BASE ** (-np.arange(quarter, dtype=np.float64) / quarter)
    tok = np.arange(seq)
    ang_r = (tok // GRID_W)[:, None] * freqs
    ang_c = (tok % GRID_W)[:, None] * freqs
    ang = np.concatenate([ang_r, ang_r, ang_c, ang_c], axis=1)
    sign = np.tile(np.concatenate([-np.ones(quarter), np.ones(quarter)]), 2)
    cos = np.concatenate([np.ones((ctx_len, GQA_HD)), np.cos(ang)], axis=0)
    sin = np.concatenate([np.zeros((ctx_len, GQA_HD)), np.sin(ang) * sign], axis=0)
    return jnp.asarray(cos, F32), jnp.asarray(sin, F32)


def _layer_weights(l, ffn_w_in, ffn_w_out, w_in, gla_fg_w2, gla_fg_b, gla_norm_g, nat_q_norm, nat_k_norm,
                   gqa_q_norm, gqa_k_norm, w_branch, w_out):
    w = w_in[l]
    nqk = GLA_HEADS * GLA_DK
    nv = GLA_HEADS * GLA_DV
    o_fg = 2 * nqk + 2 * nv
    o_nat = o_fg + 2 * GLA_LR
    n = NAT_HEADS * NAT_HD
    o_gqa = o_nat + 3 * n
    nq = GQA_HEADS * GQA_HD
    nkv = GQA_KV_HEADS * GQA_HD
    o_gate = o_gqa + nq + 2 * nkv
    pad = jnp.zeros((D_MODEL, LANES - 2 * GLA_LR), w.dtype)
    w2 = gla_fg_w2[l]
    zeros_lr = jnp.zeros((GLA_LR, nqk), w2.dtype)
    zeros_rest = jnp.zeros((LANES - 2 * GLA_LR, nqk), w2.dtype)
    return {
        "ffn": [(ffn_w_in[l, i, :, :D_FF].astype(BF16), ffn_w_in[l, i, :, D_FF:].astype(BF16),
                 ffn_w_out[l, i].astype(BF16)) for i in range(2)],
        "w_gla": jnp.concatenate([w[:, :o_fg], w[:, o_fg:o_nat], pad], axis=1).astype(BF16),
        "w_nat_qk": w[:, o_nat:o_nat + 2 * n].astype(BF16),
        "w_nat_vt": w[:, o_nat + 2 * n:o_gqa].T.astype(BF16),
        "w_gqa_qk": w[:, o_gqa:o_gqa + nq + nkv].astype(BF16),
        "w_gqa_vt": w[:, o_gqa + nq + nkv:o_gate].T.astype(BF16),
        "w_gates": w[:, o_gate:].astype(BF16),
        "w2f": jnp.concatenate([w2[0], zeros_lr, zeros_rest], axis=0).astype(BF16),
        "w2b": jnp.concatenate([zeros_lr, w2[1], zeros_rest], axis=0).astype(BF16),
        "b2f": gla_fg_b[l, 0].reshape(1, nqk),
        "b2b": gla_fg_b[l, 1].reshape(1, nqk),
        "gla_gain": gla_norm_g[l].reshape(1, GLA_DV),
        "nat_gq": jnp.tile(nat_q_norm[l], NAT_HEADS).reshape(1, n),
        "nat_gk": jnp.tile(nat_k_norm[l], NAT_HEADS).reshape(1, n),
        "gqa_gq": jnp.tile(gqa_q_norm[l], GQA_HEADS).reshape(1, nq),
        "gqa_gk": jnp.tile(gqa_k_norm[l], GQA_KV_HEADS).reshape(1, nkv),
        "ones64": _block_diag_ones(NAT_HD),
        "ones128": _block_diag_ones(GQA_HD),
        "w_branch": w_branch[l].astype(BF16),
        "w_out": w_out[l].astype(BF16),
    }


def kernel(x, c, ctx, c_ctx, w_mod, b_mod, norm_g, ffn_w_in, ffn_w_out, w_in, gla_fg_w2, gla_fg_b, gla_norm_g,
           nat_q_norm, nat_k_norm, nat_rpb, gqa_q_norm, gqa_k_norm, w_branch, w_out):
    bsz, seq, _ = x.shape
    ctx_len = ctx.shape[1]
    depth = w_mod.shape[0]
    assert seq % TM == 0 and ctx_len == TM and seq // TM >= NAT_KBLOCKS

    mod_rows = -(-(bsz + 1) // 8) * 8
    cvec = jnp.concatenate([c, c_ctx[None, :], jnp.zeros((mod_rows - bsz - 1, D_MODEL), c.dtype)], axis=0)
    mods = _mod_table(cvec, w_mod, b_mod)[:, :bsz + 1].reshape(depth, bsz + 1, 1, N_MOD * D_MODEL)
    rope_cos, rope_sin = _rope_tables(ctx_len, seq)

    xall = jnp.concatenate([ctx, x], axis=1)
    t = ctx_len + seq
    for l in range(depth):
        need_ctx = l < depth - 1
        off = 0 if need_ctx else 1
        lw = _layer_weights(l, ffn_w_in, ffn_w_out, w_in, gla_fg_w2, gla_fg_b, gla_norm_g, nat_q_norm,
                            nat_k_norm, gqa_q_norm, gqa_k_norm, w_branch, w_out)
        mod = mods[l]
        g = [norm_g[l, i].reshape(1, D_MODEL) for i in range(3)]
        xall = _ffn(xall, mod, g[0], *lw["ffn"][0], sub=0, off=0, out_off=0, out_tokens=t)
        (gq, gk, gv, gog, gfg), (nq_, nk_, nvt), (aq, ak, avt, sg) = _projections(
            xall, mod, g[1], lw, rope_cos, rope_sin)
        y_gla = _gla(gq, gk, gv, gog, gfg, lw, ctx_len)
        y_nat = _nat(nq_, nk_, nvt, _nat_bias_tables(nat_rpb[l], seq // GRID_W), need_ctx)
        y_gqa = _gqa(aq, ak, avt, need_ctx)
        xall = _merge(xall, mod, (y_gla, y_nat, y_gqa), sg, lw["w_branch"], lw["w_out"], off)
        if need_ctx:
            xall = _ffn(xall, mod, g[2], *lw["ffn"][1], sub=2, off=0, out_off=0, out_tokens=t)
        else:
            xall = _ffn(xall, mod, g[2], *lw["ffn"][1], sub=2, off=1, out_off=0, out_tokens=seq)
    return xall
```

```python
import functools

import numpy as np
import jax
import jax.numpy as jnp
from jax import lax
from jax.experimental import pallas as pl
from jax.experimental.pallas import tpu as pltpu

F32 = jnp.float32
BF16 = jnp.bfloat16

D_MODEL = 1024
GRID_W = 64
N_MOD = 9
D_FF = 2816
RMS_EPS = 1e-6
NEG_INF = -1e30

GLA_HEADS = 4
GLA_DK = 128
GLA_DV = 256
GLA_LR = 16
GLA_TAU = 16.0
GLA_CHUNK = 128

NAT_HEADS = 16
NAT_HD = 64
WIN_R = 8
WIN_C = 16
NAT_QROWS = 4
NAT_KBLOCKS = 3

GQA_HEADS = 8
GQA_KV_HEADS = 2
GQA_HD = 128
ROPE_BASE = 10000.0
GQA_KB = 512
LOG2E = 1.4426950408889634

TM = 256
LANES = 128
MXU = 256
VMEM_LIMIT = 56 * 1024 * 1024


def _dot(a, b):
    return jnp.dot(a, b, preferred_element_type=F32)


def _dot_nt(a, b):
    return lax.dot_general(a, b, (((1,), (1,)), ((), ())), preferred_element_type=F32)


def _dot_tn(a, b):
    return lax.dot_general(a, b, (((0,), (0,)), ((), ())), preferred_element_type=F32)


def _sigmoid(x):
    return 1.0 / (1.0 + jnp.exp(-x))


def _log_sigmoid(x):
    return -(jnp.maximum(-x, 0.0) + jnp.log1p(jnp.exp(-jnp.abs(x))))


def _norm_mod(x, g, shift, scale):
    y = x * lax.rsqrt(jnp.mean(x * x, axis=-1, keepdims=True) + RMS_EPS) * g
    return y * (1.0 + scale) + shift


def _split_dot(x, m):
    hi = x.astype(BF16)
    lo = (x - hi.astype(F32)).astype(BF16)
    return _dot(hi, m) + _dot(lo, m)


def _head_norm(t, gain, ones_bd, head_dim):
    sq = t * t
    parts = [_split_dot(sq[:, i:i + MXU], ones_bd) for i in range(0, t.shape[1], MXU)]
    ss = parts[0] if len(parts) == 1 else jnp.concatenate(parts, axis=1)
    return t * lax.rsqrt(ss * (1.0 / head_dim) + RMS_EPS) * gain


def _params(semantics):
    return pltpu.CompilerParams(dimension_semantics=semantics, vmem_limit_bytes=VMEM_LIMIT)


def _whole(arr, ngrid):
    zeros = (0,) * arr.ndim
    return pl.BlockSpec(arr.shape, lambda *_: zeros, pipeline_mode=pl.Buffered(1))


def _row_spec(off, width):
    return pl.BlockSpec((1, TM, width), lambda b, j: (b, j + off, 0))


def _mod_spec(col, off, ctx_row):
    return pl.BlockSpec((1, 1, D_MODEL), lambda b, j: (jnp.where(j + off == 0, ctx_row, b), 0, col))


def _mod_kernel(c_ref, w_ref, b_ref, o_ref):
    c = c_ref[...]
    s = (c * _sigmoid(c)).astype(BF16)
    o_ref[0] = _dot(s, w_ref[0].astype(BF16)) + b_ref[0]


def _mod_table(cvec, w_mod, b_mod):
    depth = w_mod.shape[0]
    rows = cvec.shape[0]
    return pl.pallas_call(
        _mod_kernel,
        grid=(depth, N_MOD),
        in_specs=[pl.BlockSpec((rows, D_MODEL), lambda l, n: (0, 0)),
                  pl.BlockSpec((1, D_MODEL, D_MODEL), lambda l, n: (l, 0, n)),
                  pl.BlockSpec((1, 1, D_MODEL), lambda l, n: (l, 0, n))],
        out_specs=pl.BlockSpec((1, rows, D_MODEL), lambda l, n: (l, 0, n)),
        out_shape=jax.ShapeDtypeStruct((depth, rows, N_MOD * D_MODEL), F32),
        compiler_params=_params(("arbitrary", "arbitrary")),
        name="mod_table",
    )(cvec, w_mod, b_mod.reshape(depth, 1, N_MOD * D_MODEL))


def _ffn_kernel(x_ref, sh_ref, sc_ref, gt_ref, g_ref, wa_ref, wb_ref, wd_ref, o_ref):
    x = x_ref[0]
    u = _norm_mod(x, g_ref[...], sh_ref[0], sc_ref[0]).astype(BF16)
    a = _dot(u, wa_ref[...])
    b = _dot(u, wb_ref[...])
    h = (a * _sigmoid(a) * b).astype(BF16)
    o_ref[0] = x + (0.5 * gt_ref[0]) * _dot(h, wd_ref[...])


def _ffn(xall, mod, g, wa, wb, wd, sub, off, out_off, out_tokens):
    bsz, t, _ = xall.shape
    ctx_row = mod.shape[0] - 1
    return pl.pallas_call(
        _ffn_kernel,
        grid=(bsz, t // TM - off),
        in_specs=[_row_spec(off, D_MODEL),
                  _mod_spec(3 * sub, off, ctx_row), _mod_spec(3 * sub + 1, off, ctx_row),
                  _mod_spec(3 * sub + 2, off, ctx_row),
                  _whole(g, 2), _whole(wa, 2), _whole(wb, 2), _whole(wd, 2)],
        out_specs=_row_spec(out_off, D_MODEL),
        out_shape=jax.ShapeDtypeStruct((bsz, out_tokens, D_MODEL), F32),
        compiler_params=_params(("parallel", "parallel")),
        name="ffn",
    )(xall, mod, mod, mod, g, wa, wb, wd)


def _proj_gla_kernel(x_ref, sh_ref, sc_ref, g_ref, w_ref, q_ref, k_ref, v_ref, og_ref, fg_ref):
    u = _norm_mod(x_ref[0], g_ref[...], sh_ref[0], sc_ref[0]).astype(BF16)
    p = _dot(u, w_ref[...])
    nqk = GLA_HEADS * GLA_DK
    nv = GLA_HEADS * GLA_DV
    q_ref[0] = (p[:, :nqk] * GLA_DK ** -0.5).astype(BF16)
    k_ref[0] = p[:, nqk:2 * nqk].astype(BF16)
    v_ref[0] = p[:, 2 * nqk:2 * nqk + nv].astype(BF16)
    og = p[:, 2 * nqk + nv:2 * nqk + 2 * nv]
    og_ref[0] = (og * _sigmoid(og)).astype(BF16)
    fg_ref[0] = p[:, 2 * nqk + 2 * nv:]


def _proj_nat_kernel(x_ref, sh_ref, sc_ref, g_ref, wqk_ref, wvt_ref, gq_ref, gk_ref, ones_ref,
                     q_ref, k_ref, vt_ref):
    u = _norm_mod(x_ref[0], g_ref[...], sh_ref[0], sc_ref[0]).astype(BF16)
    p = _dot(u, wqk_ref[...])
    n = NAT_HEADS * NAT_HD
    ones_bd = ones_ref[...]
    q_ref[0] = (_head_norm(p[:, :n], gq_ref[...], ones_bd, NAT_HD) * (NAT_HD ** -0.5 * LOG2E)).astype(BF16)
    k_ref[0] = _head_norm(p[:, n:], gk_ref[...], ones_bd, NAT_HD).astype(BF16)
    vt_ref[0] = _dot_nt(wvt_ref[...], u).astype(BF16)


def _rope(t, cos, sin, lane):
    partner = jnp.where(lane % (GQA_HD // 2) < GQA_HD // 4,
                        pltpu.roll(t, GQA_HD - GQA_HD // 4, axis=1), pltpu.roll(t, GQA_HD // 4, axis=1))
    return t * cos + partner * sin


def _proj_gqa_kernel(x_ref, sh_ref, sc_ref, g_ref, wqk_ref, wvt_ref, wg_ref, gq_ref, gk_ref, ones_ref,
                     cos_ref, sin_ref, q_ref, k_ref, vt_ref, sg_ref):
    u = _norm_mod(x_ref[0], g_ref[...], sh_ref[0], sc_ref[0]).astype(BF16)
    p = _dot(u, wqk_ref[...])
    nq = GQA_HEADS * GQA_HD
    ones_bd = ones_ref[...]
    qn = _head_norm(p[:, :nq], gq_ref[...], ones_bd, GQA_HD)
    kn = _head_norm(p[:, nq:], gk_ref[...], ones_bd, GQA_HD)
    cos = cos_ref[...]
    sin = sin_ref[...]
    lane = lax.broadcasted_iota(jnp.int32, (TM, GQA_HD), 1)
    for h in range(GQA_HEADS):
        sl = slice(h * GQA_HD, (h + 1) * GQA_HD)
        q_ref[0, :, sl] = (_rope(qn[:, sl], cos, sin, lane) * (GQA_HD ** -0.5 * LOG2E)).astype(BF16)
    for h in range(GQA_KV_HEADS):
        sl = slice(h * GQA_HD, (h + 1) * GQA_HD)
        k_ref[0, :, sl] = _rope(kn[:, sl], cos, sin, lane).astype(BF16)
    vt_ref[0] = _dot_nt(wvt_ref[...], u).astype(BF16)
    sg_ref[0] = _sigmoid(_dot(u, wg_ref[...])).astype(BF16)


def _col_spec(rows, off):
    return pl.BlockSpec((1, rows, TM), lambda b, j: (b, 0, j + off))


def _projections(xall, mod, g, lw, rope_cos, rope_sin):
    bsz, t, _ = xall.shape
    ctx_row = mod.shape[0] - 1
    grid = (bsz, t // TM)
    common = [_row_spec(0, D_MODEL), _mod_spec(3, 0, ctx_row), _mod_spec(4, 0, ctx_row), _whole(g, 2)]
    sds = jax.ShapeDtypeStruct
    nqk = GLA_HEADS * GLA_DK
    nv = GLA_HEADS * GLA_DV

    gla = pl.pallas_call(
        _proj_gla_kernel, grid=grid,
        in_specs=common + [_whole(lw["w_gla"], 2)],
        out_specs=[_row_spec(0, nqk), _row_spec(0, nqk), _row_spec(0, nv), _row_spec(0, nv),
                   _row_spec(0, LANES)],
        out_shape=[sds((bsz, t, nqk), BF16), sds((bsz, t, nqk), BF16), sds((bsz, t, nv), BF16),
                   sds((bsz, t, nv), BF16), sds((bsz, t, LANES), F32)],
        compiler_params=_params(("parallel", "parallel")), name="proj_gla",
    )(xall, mod, mod, g, lw["w_gla"])

    n = NAT_HEADS * NAT_HD
    nat = pl.pallas_call(
        _proj_nat_kernel, grid=grid,
        in_specs=common + [_whole(lw["w_nat_qk"], 2), _whole(lw["w_nat_vt"], 2), _whole(lw["nat_gq"], 2),
                           _whole(lw["nat_gk"], 2), _whole(lw["ones64"], 2)],
        out_specs=[_row_spec(0, n), _row_spec(0, n), _col_spec(n, 0)],
        out_shape=[sds((bsz, t, n), BF16), sds((bsz, t, n), BF16), sds((bsz, n, t), BF16)],
        compiler_params=_params(("parallel", "parallel")), name="proj_nat",
    )(xall, mod, mod, g, lw["w_nat_qk"], lw["w_nat_vt"], lw["nat_gq"], lw["nat_gk"], lw["ones64"])

    nq = GQA_HEADS * GQA_HD
    nkv = GQA_KV_HEADS * GQA_HD
    tab = pl.BlockSpec((TM, GQA_HD), lambda b, j: (j, 0))
    gqa = pl.pallas_call(
        _proj_gqa_kernel, grid=grid,
        in_specs=common + [_whole(lw["w_gqa_qk"], 2), _whole(lw["w_gqa_vt"], 2), _whole(lw["w_gates"], 2),
                           _whole(lw["gqa_gq"], 2), _whole(lw["gqa_gk"], 2), _whole(lw["ones128"], 2),
                           tab, tab],
        out_specs=[_row_spec(0, nq), _row_spec(0, nkv), _col_spec(nkv, 0), _row_spec(0, 3 * D_MODEL)],
        out_shape=[sds((bsz, t, nq), BF16), sds((bsz, t, nkv), BF16), sds((bsz, nkv, t), BF16),
                   sds((bsz, t, 3 * D_MODEL), BF16)],
        compiler_params=_params(("parallel", "parallel")), name="proj_gqa",
    )(xall, mod, mod, g, lw["w_gqa_qk"], lw["w_gqa_vt"], lw["w_gates"], lw["gqa_gq"], lw["gqa_gk"],
      lw["ones128"], rope_cos, rope_sin)
    return gla, nat, gqa


def _gla_direction(n, forward, q_ref, k_ref, v_ref, fg_ref, w2_ref, b2_ref, st_ref, o_ref, tri, tri_bf):
    c = GLA_CHUNK
    rows = pl.ds(pl.multiple_of(n * c, c), c)
    z = _dot(fg_ref[0, rows, :].astype(BF16), w2_ref[...]) + b2_ref[...]
    la = _log_sigmoid(z) * (1.0 / GLA_TAU)
    hi = la.astype(BF16)
    r1 = la - hi.astype(F32)
    mid = r1.astype(BF16)
    lo = (r1 - mid.astype(F32)).astype(BF16)
    cs = _dot(tri_bf, jnp.concatenate([hi, mid, lo], axis=1))
    g = cs[:, :GLA_DK] + cs[:, GLA_DK:2 * GLA_DK] + cs[:, 2 * GLA_DK:]
    if forward:
        g_mid = g[c // 2 - 1:c // 2, :]
        g_edge = g[c - 1:c, :]
    else:
        g_mid = g[c // 2:c // 2 + 1, :]
        g_edge = g[0:1, :]
    q = q_ref[0, rows, :].astype(F32)
    k = k_ref[0, rows, :].astype(F32)
    v = v_ref[0, rows, :]
    q_in = (q * jnp.exp(g - g_mid)).astype(BF16)
    k_in = (k * jnp.exp(g_mid - g)).astype(BF16)
    q_x = (q * jnp.exp(g)).astype(BF16)
    k_end = (k * jnp.exp(g_edge - g)).astype(BF16)
    att = jnp.where(tri, _dot_nt(q_in, k_in), 0.0).astype(BF16)
    st = st_ref[...]
    o_ref[rows, :] = _dot(att, v) + _dot_nt(q_x, st.astype(BF16))
    st_ref[...] = st * jnp.exp(g_edge) + _dot_tn(v, k_end)


def _gla_kernel(q_ref, k_ref, v_ref, og_ref, fg_ref, w2f_ref, w2b_ref, b2f_ref, b2b_ref, gain_ref, y_ref,
                of_ref, ob_ref, sf_ref, sb_ref, *, n_chunks, n_ctx_chunks):
    c = GLA_CHUNK
    ri = lax.broadcasted_iota(jnp.int32, (c, c), 0)
    ci = lax.broadcasted_iota(jnp.int32, (c, c), 1)
    lower = ri >= ci
    upper = ci >= ri
    lower_bf = lower.astype(BF16)
    upper_bf = upper.astype(BF16)
    sf_ref[...] = jnp.zeros_like(sf_ref)
    sb_ref[...] = jnp.zeros_like(sb_ref)

    def scan(s, carry):
        nb = jnp.where(s < n_ctx_chunks, n_ctx_chunks - 1 - s, n_chunks - 1 - (s - n_ctx_chunks))
        _gla_direction(s, True, q_ref, k_ref, v_ref, fg_ref, w2f_ref, b2f_ref, sf_ref, of_ref, lower, lower_bf)
        _gla_direction(nb, False, q_ref, k_ref, v_ref, fg_ref, w2b_ref, b2b_ref, sb_ref, ob_ref, upper, upper_bf)
        return carry

    lax.fori_loop(0, n_chunks, scan, 0)

    def finish(n, carry):
        rows = pl.ds(pl.multiple_of(n * c, c), c)
        o = of_ref[rows, :] + ob_ref[rows, :]
        y = o * lax.rsqrt(jnp.mean(o * o, axis=-1, keepdims=True) + RMS_EPS) * gain_ref[...]
        y_ref[0, rows, :] = (y * og_ref[0, rows, :].astype(F32)).astype(BF16)
        return carry

    lax.fori_loop(0, n_chunks, finish, 0)


def _gla(q, k, v, og, fg, lw, ctx_len):
    bsz, t, _ = q.shape
    seq_spec = lambda w: pl.BlockSpec((1, t, w), lambda b, h: (b, 0, h))
    head_w = pl.BlockSpec((LANES, GLA_DK), lambda b, h: (0, h))
    head_b = pl.BlockSpec((1, GLA_DK), lambda b, h: (0, h))
    kern = functools.partial(_gla_kernel, n_chunks=t // GLA_CHUNK, n_ctx_chunks=ctx_len // GLA_CHUNK)
    return pl.pallas_call(
        kern, grid=(bsz, GLA_HEADS),
        in_specs=[seq_spec(GLA_DK), seq_spec(GLA_DK), seq_spec(GLA_DV), seq_spec(GLA_DV),
                  pl.BlockSpec((1, t, LANES), lambda b, h: (b, 0, 0)),
                  head_w, head_w, head_b, head_b, pl.BlockSpec((1, GLA_DV), lambda b, h: (0, 0))],
        out_specs=seq_spec(GLA_DV),
        out_shape=jax.ShapeDtypeStruct((bsz, t, GLA_HEADS * GLA_DV), BF16),
        scratch_shapes=[pltpu.VMEM((t, GLA_DV), F32), pltpu.VMEM((t, GLA_DV), F32),
                        pltpu.VMEM((GLA_DV, GLA_DK), F32), pltpu.VMEM((GLA_DV, GLA_DK), F32)],
        compiler_params=_params(("parallel", "parallel")), name="gla",
    )(q, k, v, og, fg, lw["w2f"], lw["w2b"], lw["b2f"], lw["b2b"], lw["gla_gain"])


NAT_GROUP = 4


def _nat_kernel(q_ref, k0_ref, k1_ref, k2_ref, kc_ref, v0_ref, v1_ref, v2_ref, vc_ref, bias_ref, o_ref, s_ref,
                *, off):
    is_ctx = pl.program_id(0) + off == 0
    lane = lax.broadcasted_iota(jnp.int32, (TM, LANES), 1)
    row = lax.broadcasted_iota(jnp.int32, (LANES, TM), 0)
    n_heads = 2 * NAT_GROUP
    n_lat = NAT_KBLOCKS * TM

    def run(lat_keys, lat_vals):
        kcats, vlats = {}, {}

        def scores(i):
            hp, e = divmod(i, 2)
            sl = slice(hp * LANES, (hp + 1) * LANES)
            if hp not in kcats:
                kcats[hp] = jnp.concatenate([r[0, :, sl] for r in lat_keys + [kc_ref]], axis=0)
            q = q_ref[0, :, sl]
            qm = jnp.where((lane >= NAT_HD) if e else (lane < NAT_HD), q, jnp.zeros_like(q))
            s_ref[i % 2, :kcats[hp].shape[0], :] = _dot_nt(kcats[hp], qm)

        scores(0)
        outs = []
        for i in range(n_heads):
            if i + 1 < n_heads:
                scores(i + 1)
            hp, e = divmod(i, 2)
            sl = slice(hp * LANES, (hp + 1) * LANES)
            n_l = n_lat if lat_keys else 0
            s_ctx = s_ref[i % 2, n_l:n_l + TM, :]
            m = jnp.max(s_ctx, axis=0, keepdims=True)
            if lat_keys:
                s_lat = s_ref[i % 2, :n_lat, :] + bias_ref[0, i]
                m = jnp.maximum(m, jnp.max(s_lat, axis=0, keepdims=True))
            p_ctx = jnp.exp2(s_ctx - m)
            l = jnp.sum(p_ctx, axis=0, keepdims=True)
            o = _dot(vc_ref[0, sl, :], p_ctx.astype(BF16))
            if lat_keys:
                if hp not in vlats:
                    vlats[hp] = jnp.concatenate([r[0, sl, :] for r in lat_vals], axis=1)
                p_lat = jnp.exp2(s_lat - m)
                l = l + jnp.sum(p_lat, axis=0, keepdims=True)
                o = o + _dot(vlats[hp], p_lat.astype(BF16))
            outs.append(o * (1.0 / l))
            if e == 1:
                ot = jnp.where(row < NAT_HD, outs[-2], outs[-1])
                o_ref[0, :, sl] = ot.T.astype(BF16)

    @pl.when(is_ctx)
    def _():
        run([], [])

    @pl.when(jnp.logical_not(is_ctx))
    def _():
        run([k0_ref, k1_ref, k2_ref], [v0_ref, v1_ref, v2_ref])


def _nat(q, k, vt, bias, need_ctx):
    bsz, t, n = q.shape
    off = 0 if need_ctx else 1
    nblk = t // TM - 1
    gw = NAT_GROUP * LANES
    ngroups = n // gw

    def kblock(qi, i):
        j = qi + off - 1
        return 1 + jnp.clip(j - 1, 0, nblk - NAT_KBLOCKS) + i

    def bias_class(qi):
        j = qi + off - 1
        return jnp.where(j <= 0, 0, jnp.where(j == nblk - 1, 2, 1))

    kspec = lambda i: pl.BlockSpec((1, TM, gw), lambda qi, hg, b: (b, kblock(qi, i), hg))
    vspec = lambda i: pl.BlockSpec((1, gw, TM), lambda qi, hg, b: (b, hg, kblock(qi, i)))
    return pl.pallas_call(
        functools.partial(_nat_kernel, off=off),
        grid=(t // TM - off, ngroups, bsz),
        in_specs=[pl.BlockSpec((1, TM, gw), lambda qi, hg, b: (b, qi + off, hg)),
                  kspec(0), kspec(1), kspec(2),
                  pl.BlockSpec((1, TM, gw), lambda qi, hg, b: (b, 0, hg)),
                  vspec(0), vspec(1), vspec(2),
                  pl.BlockSpec((1, gw, TM), lambda qi, hg, b: (b, hg, 0)),
                  pl.BlockSpec((1, 2 * NAT_GROUP, NAT_KBLOCKS * TM, TM),
                               lambda qi, hg, b: (bias_class(qi), hg, 0, 0))],
        out_specs=pl.BlockSpec((1, TM, gw), lambda qi, hg, b: (b, qi + off, hg)),
        out_shape=jax.ShapeDtypeStruct((bsz, t, n), BF16),
        scratch_shapes=[pltpu.VMEM((2, (NAT_KBLOCKS + 1) * TM, TM), F32)],
        compiler_params=_params(("parallel", "parallel", "parallel")), name="nat",
    )(q, k, k, k, k, vt, vt, vt, vt, bias)


def _nat_bias_tables(rpb, rows):
    nblk = rows // NAT_QROWS
    kr_win = min(WIN_R, rows)
    col = np.arange(GRID_W)
    c_start = np.clip(col - WIN_C // 2, 0, GRID_W - WIN_C)
    cmask = (col[:, None] >= c_start[None, :]) & (col[:, None] < c_start[None, :] + WIN_C)
    dc = np.clip(col[:, None] - col[None, :] + WIN_C - 1, 0, 2 * WIN_C - 2)
    drs, rmasks = [], []
    for jblk in (0, 1, nblk - 1):
        base = int(np.clip(jblk - 1, 0, nblk - NAT_KBLOCKS))
        r = NAT_QROWS * jblk + np.arange(NAT_QROWS)
        r_start = np.clip(r - kr_win // 2, 0, rows - kr_win)
        kr = NAT_QROWS * base + np.arange(NAT_KBLOCKS * NAT_QROWS)
        rmasks.append((kr[:, None] >= r_start[None, :]) & (kr[:, None] < r_start[None, :] + kr_win))
        drs.append(np.clip(kr[:, None] - r[None, :] + WIN_R - 1, 0, 2 * WIN_R - 2))
    dr = np.stack(drs)
    rmask = np.stack(rmasks)
    a = rpb.astype(F32)[:, :, dc]
    t = a[:, dr]
    t = jnp.transpose(t, (1, 0, 2, 4, 3, 5))
    mask = rmask[:, None, :, None, :, None] & cmask[None, None, None, :, None, :]
    t = jnp.where(mask, t * LOG2E, NEG_INF)
    nk = NAT_KBLOCKS * NAT_QROWS * GRID_W
    return t.reshape(3, rpb.shape[0], nk, NAT_QROWS * GRID_W)


def _gqa_kernel(q_ref, k_ref, vt_ref, o_ref, s_ref, m_ref, l_ref, acc_ref, *, off, n_lat):
    is_ctx = pl.program_id(2) + off == 0
    group = GQA_HEADS // GQA_KV_HEADS
    qs = jnp.concatenate([q_ref[0, :, g * GQA_HD:(g + 1) * GQA_HD] for g in range(group)], axis=0)

    def scores(start, size, slot):
        s_ref[slot, :size, :] = _dot_nt(k_ref[0, start:start + size, :], qs)

    scores(0, TM, 0)
    scores(TM, GQA_KB, 1)
    st = s_ref[0, :TM, :]
    m = jnp.max(st, axis=0, keepdims=True)
    p = jnp.exp2(st - m)
    m_ref[...] = m
    l_ref[...] = jnp.sum(p, axis=0, keepdims=True)
    acc_ref[...] = _dot(vt_ref[0, :, :TM], p.astype(BF16))

    @pl.when(jnp.logical_not(is_ctx))
    def _():
        m = m_ref[...]
        l = l_ref[...]
        for j in range(n_lat):
            start = TM + j * GQA_KB
            if j + 1 < n_lat:
                scores(start + GQA_KB, GQA_KB, j % 2)
            st = s_ref[(j + 1) % 2]
            m_new = jnp.maximum(m, jnp.max(st, axis=0, keepdims=True))
            alpha = jnp.exp2(m - m_new)
            p = jnp.exp2(st - m_new)
            l = alpha * l + jnp.sum(p, axis=0, keepdims=True)
            acc_ref[...] = alpha * acc_ref[...] + _dot(vt_ref[0, :, start:start + GQA_KB], p.astype(BF16))
            m = m_new
        l_ref[...] = l

    ot = acc_ref[...] * (1.0 / l_ref[...])
    for g in range(group):
        o_ref[0, :, g * GQA_HD:(g + 1) * GQA_HD] = ot[:, g * TM:(g + 1) * TM].T.astype(BF16)


def _gqa(q, k, vt, need_ctx):
    bsz, t, nq = q.shape
    off = 0 if need_ctx else 1
    group = GQA_HEADS // GQA_KV_HEADS
    gw = group * GQA_HD
    assert (t - TM) % GQA_KB == 0
    return pl.pallas_call(
        functools.partial(_gqa_kernel, off=off, n_lat=(t - TM) // GQA_KB),
        grid=(bsz, GQA_KV_HEADS, t // TM - off),
        in_specs=[pl.BlockSpec((1, TM, gw), lambda b, h, qi: (b, qi + off, h)),
                  pl.BlockSpec((1, t, GQA_HD), lambda b, h, qi: (b, 0, h)),
                  pl.BlockSpec((1, GQA_HD, t), lambda b, h, qi: (b, h, 0))],
        out_specs=pl.BlockSpec((1, TM, gw), lambda b, h, qi: (b, qi + off, h)),
        out_shape=jax.ShapeDtypeStruct((bsz, t, nq), BF16),
        scratch_shapes=[pltpu.VMEM((2, GQA_KB, group * TM), F32),
                        pltpu.VMEM((1, group * TM), F32), pltpu.VMEM((1, group * TM), F32),
                        pltpu.VMEM((GQA_HD, group * TM), F32)],
        compiler_params=_params(("parallel", "parallel", "arbitrary")), name="gqa",
    )(q, k, vt)


def _merge_kernel(x_ref, gt_ref, y0_ref, y1_ref, y2_ref, sg_ref, wb_ref, wo_ref, o_ref):
    z = None
    for i, y_ref in enumerate((y0_ref, y1_ref, y2_ref)):
        zi = sg_ref[0, :, i * D_MODEL:(i + 1) * D_MODEL].astype(F32) * _dot(y_ref[0], wb_ref[i])
        z = zi if z is None else z + zi
    o_ref[0] = x_ref[0] + gt_ref[0] * _dot(z.astype(BF16), wo_ref[...])


def _merge(xall, mod, ys, sg, wb, wo, off):
    bsz, t, _ = xall.shape
    ctx_row = mod.shape[0] - 1
    return pl.pallas_call(
        _merge_kernel,
        grid=(bsz, t // TM - off),
        in_specs=[_row_spec(off, D_MODEL), _mod_spec(5, off, ctx_row),
                  _row_spec(off, D_MODEL), _row_spec(off, D_MODEL), _row_spec(off, D_MODEL),
                  _row_spec(off, 3 * D_MODEL), _whole(wb, 2), _whole(wo, 2)],
        out_specs=_row_spec(off, D_MODEL),
        out_shape=jax.ShapeDtypeStruct((bsz, t, D_MODEL), F32),
        compiler_params=_params(("parallel", "parallel")), name="merge",
    )(xall, mod, ys[0], ys[1], ys[2], sg, wb, wo)


def _block_diag_ones(head_dim):
    i = np.arange(MXU)
    return jnp.asarray((i[:, None] // head_dim) == (i[None, :] // head_dim), dtype=BF16)


def _rope_tables(ctx_len, seq):
    quarter = GQA_HD // 4
    freqs = ROPE_BASE ** (-np.arange(quarter, dtype=np.float64) / quarter)
    tok = np.arange(seq)
    ang_r = (tok // GRID_W)[:, None] * freqs
    ang_c = (tok % GRID_W)[:, None] * freqs
    ang = np.concatenate([ang_r, ang_r, ang_c, ang_c], axis=1)
    sign = np.tile(np.concatenate([-np.ones(quarter), np.ones(quarter)]), 2)
    cos = np.concatenate([np.ones((ctx_len, GQA_HD)), np.cos(ang)], axis=0)
    sin = np.concatenate([np.zeros((ctx_len, GQA_HD)), np.sin(ang) * sign], axis=0)
    return jnp.asarray(cos, F32), jnp.asarray(sin, F32)


def _layer_weights(l, ffn_w_in, ffn_w_out, w_in, gla_fg_w2, gla_fg_b, gla_norm_g, nat_q_norm, nat_k_norm,
                   gqa_q_norm, gqa_k_norm, w_branch, w_out):
    w = w_in[l]
    nqk = GLA_HEADS * GLA_DK
    nv = GLA_HEADS * GLA_DV
    o_fg = 2 * nqk + 2 * nv
    o_nat = o_fg + 2 * GLA_LR
    n = NAT_HEADS * NAT_HD
    o_gqa = o_nat + 3 * n
    nq = GQA_HEADS * GQA_HD
    nkv = GQA_KV_HEADS * GQA_HD
    o_gate = o_gqa + nq + 2 * nkv
    pad = jnp.zeros((D_MODEL, LANES - 2 * GLA_LR), w.dtype)
    w2 = gla_fg_w2[l]
    zeros_lr = jnp.zeros((GLA_LR, nqk), w2.dtype)
    zeros_rest = jnp.zeros((LANES - 2 * GLA_LR, nqk), w2.dtype)
    return {
        "ffn": [(ffn_w_in[l, i, :, :D_FF].astype(BF16), ffn_w_in[l, i, :, D_FF:].astype(BF16),
                 ffn_w_out[l, i].astype(BF16)) for i in range(2)],
        "w_gla": jnp.concatenate([w[:, :o_fg], w[:, o_fg:o_nat], pad], axis=1).astype(BF16),
        "w_nat_qk": w[:, o_nat:o_nat + 2 * n].astype(BF16),
        "w_nat_vt": w[:, o_nat + 2 * n:o_gqa].T.astype(BF16),
        "w_gqa_qk": w[:, o_gqa:o_gqa + nq + nkv].astype(BF16),
        "w_gqa_vt": w[:, o_gqa + nq + nkv:o_gate].T.astype(BF16),
        "w_gates": w[:, o_gate:].astype(BF16),
        "w2f": jnp.concatenate([w2[0], zeros_lr, zeros_rest], axis=0).astype(BF16),
        "w2b": jnp.concatenate([zeros_lr, w2[1], zeros_rest], axis=0).astype(BF16),
        "b2f": gla_fg_b[l, 0].reshape(1, nqk),
        "b2b": gla_fg_b[l, 1].reshape(1, nqk),
        "gla_gain": gla_norm_g[l].reshape(1, GLA_DV),
        "nat_gq": jnp.tile(nat_q_norm[l], NAT_HEADS).reshape(1, n),
        "nat_gk": jnp.tile(nat_k_norm[l], NAT_HEADS).reshape(1, n),
        "gqa_gq": jnp.tile(gqa_q_norm[l], GQA_HEADS).reshape(1, nq),
        "gqa_gk": jnp.tile(gqa_k_norm[l], GQA_KV_HEADS).reshape(1, nkv),
        "ones64": _block_diag_ones(NAT_HD),
        "ones128": _block_diag_ones(GQA_HD),
        "w_branch": w_branch[l].astype(BF16),
        "w_out": w_out[l].astype(BF16),
    }


def kernel(x, c, ctx, c_ctx, w_mod, b_mod, norm_g, ffn_w_in, ffn_w_out, w_in, gla_fg_w2, gla_fg_b, gla_norm_g,
           nat_q_norm, nat_k_norm, nat_rpb, gqa_q_norm, gqa_k_norm, w_branch, w_out):
    bsz, seq, _ = x.shape
    ctx_len = ctx.shape[1]
    depth = w_mod.shape[0]
    assert seq % TM == 0 and ctx_len == TM and seq // TM >= NAT_KBLOCKS

    mod_rows = -(-(bsz + 1) // 8) * 8
    cvec = jnp.concatenate([c, c_ctx[None, :], jnp.zeros((mod_rows - bsz - 1, D_MODEL), c.dtype)], axis=0)
    mods = _mod_table(cvec, w_mod, b_mod)[:, :bsz + 1].reshape(depth, bsz + 1, 1, N_MOD * D_MODEL)
    rope_cos, rope_sin = _rope_tables(ctx_len, seq)

    xall = jnp.concatenate([ctx, x], axis=1)
    t = ctx_len + seq
    for l in range(depth):
        need_ctx = l < depth - 1
        off = 0 if need_ctx else 1
        lw = _layer_weights(l, ffn_w_in, ffn_w_out, w_in, gla_fg_w2, gla_fg_b, gla_norm_g, nat_q_norm,
                            nat_k_norm, gqa_q_norm, gqa_k_norm, w_branch, w_out)
        mod = mods[l]
        g = [norm_g[l, i].reshape(1, D_MODEL) for i in range(3)]
        xall = _ffn(xall, mod, g[0], *lw["ffn"][0], sub=0, off=0, out_off=0, out_tokens=t)
        (gq, gk, gv, gog, gfg), (nq_, nk_, nvt), (aq, ak, avt, sg) = _projections(
            xall, mod, g[1], lw, rope_cos, rope_sin)
        y_gla = _gla(gq, gk, gv, gog, gfg, lw, ctx_len)
        y_nat = _nat(nq_, nk_, nvt, _nat_bias_tables(nat_rpb[l], seq // GRID_W), need_ctx)
        y_gqa = _gqa(aq, ak, avt, need_ctx)
        xall = _merge(xall, mod, (y_gla, y_nat, y_gqa), sg, lw["w_branch"], lw["w_out"], off)
        if need_ctx:
            xall = _ffn(xall, mod, g[2], *lw["ffn"][1], sub=2, off=0, out_off=0, out_tokens=t)
        else:
            xall = _ffn(xall, mod, g[2], *lw["ffn"][1], sub=2, off=1, out_off=0, out_tokens=seq)
    return xall
```

```python
import functools

import numpy as np
import jax
import jax.numpy as jnp
from jax import lax
from jax.experimental import pallas as pl
from jax.experimental.pallas import tpu as pltpu

F32 = jnp.float32
BF16 = jnp.bfloat16

D_MODEL = 1024
GRID_W = 64
N_MOD = 9
D_FF = 2816
RMS_EPS = 1e-6
NEG_INF = -1e30

GLA_HEADS = 4
GLA_DK = 128
GLA_DV = 256
GLA_LR = 16
GLA_TAU = 16.0
GLA_CHUNK = 128

NAT_HEADS = 16
NAT_HD = 64
WIN_R = 8
WIN_C = 16
NAT_QROWS = 4
NAT_KBLOCKS = 3

GQA_HEADS = 8
GQA_KV_HEADS = 2
GQA_HD = 128
ROPE_BASE = 10000.0
GQA_KB = 512
LOG2E = 1.4426950408889634

TM = 256
LANES = 128
MXU = 256
VMEM_LIMIT = 56 * 1024 * 1024


def _dot(a, b):
    return jnp.dot(a, b, preferred_element_type=F32)


def _dot_nt(a, b):
    return lax.dot_general(a, b, (((1,), (1,)), ((), ())), preferred_element_type=F32)


def _dot_tn(a, b):
    return lax.dot_general(a, b, (((0,), (0,)), ((), ())), preferred_element_type=F32)


def _sigmoid(x):
    return 1.0 / (1.0 + jnp.exp(-x))


def _log_sigmoid(x):
    return -(jnp.maximum(-x, 0.0) + jnp.log1p(jnp.exp(-jnp.abs(x))))


def _norm_mod(x, g, shift, scale):
    y = x * lax.rsqrt(jnp.mean(x * x, axis=-1, keepdims=True) + RMS_EPS) * g
    return y * (1.0 + scale) + shift


def _split_dot(x, m):
    hi = x.astype(BF16)
    lo = (x - hi.astype(F32)).astype(BF16)
    return _dot(hi, m) + _dot(lo, m)


def _head_norm(t, gain, ones_bd, head_dim):
    sq = t * t
    parts = [_split_dot(sq[:, i:i + MXU], ones_bd) for i in range(0, t.shape[1], MXU)]
    ss = parts[0] if len(parts) == 1 else jnp.concatenate(parts, axis=1)
    return t * lax.rsqrt(ss * (1.0 / head_dim) + RMS_EPS) * gain


def _params(semantics):
    return pltpu.CompilerParams(dimension_semantics=semantics, vmem_limit_bytes=VMEM_LIMIT)


def _whole(arr, ngrid):
    zeros = (0,) * arr.ndim
    return pl.BlockSpec(arr.shape, lambda *_: zeros, pipeline_mode=pl.Buffered(1))


def _row_spec(off, width):
    return pl.BlockSpec((1, TM, width), lambda b, j: (b, j + off, 0))


def _mod_spec(col, off, ctx_row):
    return pl.BlockSpec((1, 1, D_MODEL), lambda b, j: (jnp.where(j + off == 0, ctx_row, b), 0, col))


def _mod_kernel(c_ref, w_ref, b_ref, o_ref):
    c = c_ref[...]
    s = (c * _sigmoid(c)).astype(BF16)
    o_ref[0] = _dot(s, w_ref[0].astype(BF16)) + b_ref[0]


def _mod_table(cvec, w_mod, b_mod):
    depth = w_mod.shape[0]
    rows = cvec.shape[0]
    return pl.pallas_call(
        _mod_kernel,
        grid=(depth, N_MOD),
        in_specs=[pl.BlockSpec((rows, D_MODEL), lambda l, n: (0, 0)),
                  pl.BlockSpec((1, D_MODEL, D_MODEL), lambda l, n: (l, 0, n)),
                  pl.BlockSpec((1, 1, D_MODEL), lambda l, n: (l, 0, n))],
        out_specs=pl.BlockSpec((1, rows, D_MODEL), lambda l, n: (l, 0, n)),
        out_shape=jax.ShapeDtypeStruct((depth, rows, N_MOD * D_MODEL), F32),
        compiler_params=_params(("arbitrary", "arbitrary")),
        name="mod_table",
    )(cvec, w_mod, b_mod.reshape(depth, 1, N_MOD * D_MODEL))


def _ffn_kernel(x_ref, sh_ref, sc_ref, gt_ref, g_ref, wa_ref, wb_ref, wd_ref, o_ref):
    x = x_ref[0]
    u = _norm_mod(x, g_ref[...], sh_ref[0], sc_ref[0]).astype(BF16)
    a = _dot(u, wa_ref[...])
    b = _dot(u, wb_ref[...])
    h = (a * _sigmoid(a) * b).astype(BF16)
    o_ref[0] = x + (0.5 * gt_ref[0]) * _dot(h, wd_ref[...])


def _ffn(xall, mod, g, wa, wb, wd, sub, off, out_off, out_tokens):
    bsz, t, _ = xall.shape
    ctx_row = mod.shape[0] - 1
    return pl.pallas_call(
        _ffn_kernel,
        grid=(bsz, t // TM - off),
        in_specs=[_row_spec(off, D_MODEL),
                  _mod_spec(3 * sub, off, ctx_row), _mod_spec(3 * sub + 1, off, ctx_row),
                  _mod_spec(3 * sub + 2, off, ctx_row),
                  _whole(g, 2), _whole(wa, 2), _whole(wb, 2), _whole(wd, 2)],
        out_specs=_row_spec(out_off, D_MODEL),
        out_shape=jax.ShapeDtypeStruct((bsz, out_tokens, D_MODEL), F32),
        compiler_params=_params(("parallel", "parallel")),
        name="ffn",
    )(xall, mod, mod, mod, g, wa, wb, wd)


def _chunk_cumsum(tri, la):
    hi = la.astype(BF16)
    r1 = la - hi.astype(F32)
    mid = r1.astype(BF16)
    lo = (r1 - mid.astype(F32)).astype(BF16)
    return _dot(tri, hi) + _dot(tri, mid) + _dot(tri, lo)


def _proj_gla_kernel(x_ref, sh_ref, sc_ref, g_ref, w_ref, w2_ref, b2_ref, tril_ref, triu_ref,
                     q_ref, k_ref, v_ref, og_ref, gf_ref, gb_ref):
    u = _norm_mod(x_ref[0], g_ref[...], sh_ref[0], sc_ref[0]).astype(BF16)
    p = _dot(u, w_ref[...])
    nqk = GLA_HEADS * GLA_DK
    nv = GLA_HEADS * GLA_DV
    q_ref[0] = (p[:, :nqk] * GLA_DK ** -0.5).astype(BF16)
    k_ref[0] = p[:, nqk:2 * nqk].astype(BF16)
    v_ref[0] = p[:, 2 * nqk:2 * nqk + nv].astype(BF16)
    og = p[:, 2 * nqk + nv:2 * nqk + 2 * nv]
    og_ref[0] = (og * _sigmoid(og)).astype(BF16)
    z = _dot(p[:, 2 * nqk + 2 * nv:].astype(BF16), w2_ref[...]) + b2_ref[...]
    la = _log_sigmoid(z) * (1.0 / GLA_TAU)
    gf_ref[0] = _chunk_cumsum(tril_ref[...], la[:, :nqk])
    gb_ref[0] = _chunk_cumsum(triu_ref[...], la[:, nqk:])


def _proj_nat_kernel(x_ref, sh_ref, sc_ref, g_ref, wqk_ref, wvt_ref, gq_ref, gk_ref, ones_ref,
                     q_ref, k_ref, vt_ref):
    u = _norm_mod(x_ref[0], g_ref[...], sh_ref[0], sc_ref[0]).astype(BF16)
    p = _dot(u, wqk_ref[...])
    n = NAT_HEADS * NAT_HD
    ones_bd = ones_ref[...]
    q_ref[0] = (_head_norm(p[:, :n], gq_ref[...], ones_bd, NAT_HD) * (NAT_HD ** -0.5 * LOG2E)).astype(BF16)
    k_ref[0] = _head_norm(p[:, n:], gk_ref[...], ones_bd, NAT_HD).astype(BF16)
    vt_ref[0] = _dot_nt(wvt_ref[...], u).astype(BF16)


def _rope(t, cos, sin, lane):
    partner = jnp.where(lane % (GQA_HD // 2) < GQA_HD // 4,
                        pltpu.roll(t, GQA_HD - GQA_HD // 4, axis=1), pltpu.roll(t, GQA_HD // 4, axis=1))
    return t * cos + partner * sin


def _proj_gqa_kernel(x_ref, sh_ref, sc_ref, g_ref, wqk_ref, wvt_ref, wg_ref, gq_ref, gk_ref, ones_ref,
                     cos_ref, sin_ref, q_ref, k_ref, vt_ref, sg_ref):
    u = _norm_mod(x_ref[0], g_ref[...], sh_ref[0], sc_ref[0]).astype(BF16)
    p = _dot(u, wqk_ref[...])
    nq = GQA_HEADS * GQA_HD
    ones_bd = ones_ref[...]
    qn = _head_norm(p[:, :nq], gq_ref[...], ones_bd, GQA_HD)
    kn = _head_norm(p[:, nq:], gk_ref[...], ones_bd, GQA_HD)
    cos = cos_ref[...]
    sin = sin_ref[...]
    lane = lax.broadcasted_iota(jnp.int32, (TM, GQA_HD), 1)
    for h in range(GQA_HEADS):
        sl = slice(h * GQA_HD, (h + 1) * GQA_HD)
        q_ref[0, :, sl] = (_rope(qn[:, sl], cos, sin, lane) * (GQA_HD ** -0.5 * LOG2E)).astype(BF16)
    for h in range(GQA_KV_HEADS):
        sl = slice(h * GQA_HD, (h + 1) * GQA_HD)
        k_ref[0, :, sl] = _rope(kn[:, sl], cos, sin, lane).astype(BF16)
    vt_ref[0] = _dot_nt(wvt_ref[...], u).astype(BF16)
    sg_ref[0] = _sigmoid(_dot(u, wg_ref[...])).astype(BF16)


def _col_spec(rows, off):
    return pl.BlockSpec((1, rows, TM), lambda b, j: (b, 0, j + off))


def _projections(xall, mod, g, lw, rope_cos, rope_sin):
    bsz, t, _ = xall.shape
    ctx_row = mod.shape[0] - 1
    grid = (bsz, t // TM)
    common = [_row_spec(0, D_MODEL), _mod_spec(3, 0, ctx_row), _mod_spec(4, 0, ctx_row), _whole(g, 2)]
    sds = jax.ShapeDtypeStruct
    nqk = GLA_HEADS * GLA_DK
    nv = GLA_HEADS * GLA_DV

    gla = pl.pallas_call(
        _proj_gla_kernel, grid=grid,
        in_specs=common + [_whole(lw["w_gla"], 2), _whole(lw["w2"], 2), _whole(lw["b2"], 2),
                           _whole(lw["tril"], 2), _whole(lw["triu"], 2)],
        out_specs=[_row_spec(0, nqk), _row_spec(0, nqk), _row_spec(0, nv), _row_spec(0, nv),
                   _row_spec(0, nqk), _row_spec(0, nqk)],
        out_shape=[sds((bsz, t, nqk), BF16), sds((bsz, t, nqk), BF16), sds((bsz, t, nv), BF16),
                   sds((bsz, t, nv), BF16), sds((bsz, t, nqk), F32), sds((bsz, t, nqk), F32)],
        compiler_params=_params(("parallel", "parallel")), name="proj_gla",
    )(xall, mod, mod, g, lw["w_gla"], lw["w2"], lw["b2"], lw["tril"], lw["triu"])

    n = NAT_HEADS * NAT_HD
    nat = pl.pallas_call(
        _proj_nat_kernel, grid=grid,
        in_specs=common + [_whole(lw["w_nat_qk"], 2), _whole(lw["w_nat_vt"], 2), _whole(lw["nat_gq"], 2),
                           _whole(lw["nat_gk"], 2), _whole(lw["ones64"], 2)],
        out_specs=[_row_spec(0, n), _row_spec(0, n), _col_spec(n, 0)],
        out_shape=[sds((bsz, t, n), BF16), sds((bsz, t, n), BF16), sds((bsz, n, t), BF16)],
        compiler_params=_params(("parallel", "parallel")), name="proj_nat",
    )(xall, mod, mod, g, lw["w_nat_qk"], lw["w_nat_vt"], lw["nat_gq"], lw["nat_gk"], lw["ones64"])

    nq = GQA_HEADS * GQA_HD
    nkv = GQA_KV_HEADS * GQA_HD
    tab = pl.BlockSpec((TM, GQA_HD), lambda b, j: (j, 0))
    gqa = pl.pallas_call(
        _proj_gqa_kernel, grid=grid,
        in_specs=common + [_whole(lw["w_gqa_qk"], 2), _whole(lw["w_gqa_vt"], 2), _whole(lw["w_gates"], 2),
                           _whole(lw["gqa_gq"], 2), _whole(lw["gqa_gk"], 2), _whole(lw["ones128"], 2),
                           tab, tab],
        out_specs=[_row_spec(0, nq), _row_spec(0, nkv), _col_spec(nkv, 0), _row_spec(0, 3 * D_MODEL)],
        out_shape=[sds((bsz, t, nq), BF16), sds((bsz, t, nkv), BF16), sds((bsz, nkv, t), BF16),
                   sds((bsz, t, 3 * D_MODEL), BF16)],
        compiler_params=_params(("parallel", "parallel")), name="proj_gqa",
    )(xall, mod, mod, g, lw["w_gqa_qk"], lw["w_gqa_vt"], lw["w_gates"], lw["gqa_gq"], lw["gqa_gk"],
      lw["ones128"], rope_cos, rope_sin)
    return gla, nat, gqa


GLA_UNROLL = 2


def _gla_chain(n, forward, q_ref, k_ref, v_ref, g_ref, o_ref, state, tri):
    c = GLA_CHUNK
    rows = pl.ds(pl.multiple_of(n * c, c), c)
    g = g_ref[0, rows, :]
    if forward:
        g_mid = g[c // 2 - 1:c // 2, :]
        g_edge = g[c - 1:c, :]
    else:
        g_mid = g[c // 2:c // 2 + 1, :]
        g_edge = g[0:1, :]
    q = q_ref[0, rows, :].astype(F32)
    k = k_ref[0, rows, :].astype(F32)
    v = v_ref[0, rows, :]
    q_in = (q * jnp.exp(g - g_mid)).astype(BF16)
    k_in = (k * jnp.exp(g_mid - g)).astype(BF16)
    q_x = (q * jnp.exp(g)).astype(BF16)
    k_end = (k * jnp.exp(g_edge - g)).astype(BF16)
    decay = jnp.exp(g_edge)
    yield
    att = _dot_nt(q_in, k_in)
    s_in = state["s"]
    state["s"] = s_in * decay + _dot_tn(v, k_end)
    yield
    att = jnp.where(tri, att, 0.0).astype(BF16)
    o = _dot(att, v) + _dot_nt(q_x, s_in.astype(BF16))
    yield
    o_ref[rows, :] = o


def _gla_kernel(q_ref, k_ref, v_ref, og_ref, gf_ref, gb_ref, gain_ref, y_ref,
                of_ref, ob_ref, sf_ref, sb_ref, *, n_chunks, n_ctx_chunks):
    c = GLA_CHUNK
    ri = lax.broadcasted_iota(jnp.int32, (c, c), 0)
    ci = lax.broadcasted_iota(jnp.int32, (c, c), 1)
    lower = ri >= ci
    upper = ci >= ri
    sf_ref[...] = jnp.zeros_like(sf_ref)
    sb_ref[...] = jnp.zeros_like(sb_ref)

    def scan(it, carry):
        fwd = {"s": sf_ref[...]}
        bwd = {"s": sb_ref[...]}
        chains = []
        for u in range(GLA_UNROLL):
            s = it * GLA_UNROLL + u
            nb = jnp.where(s < n_ctx_chunks, n_ctx_chunks - 1 - s, n_chunks - 1 - (s - n_ctx_chunks))
            chains.append(_gla_chain(s, True, q_ref, k_ref, v_ref, gf_ref, of_ref, fwd, lower))
            chains.append(_gla_chain(nb, False, q_ref, k_ref, v_ref, gb_ref, ob_ref, bwd, upper))
        while chains:
            alive = []
            for ch in chains:
                try:
                    next(ch)
                    alive.append(ch)
                except StopIteration:
                    pass
            chains = alive
        sf_ref[...] = fwd["s"]
        sb_ref[...] = bwd["s"]
        return carry

    lax.fori_loop(0, n_chunks // GLA_UNROLL, scan, 0)

    def finish(n, carry):
        rows = pl.ds(pl.multiple_of(n * c, c), c)
        o = of_ref[rows, :] + ob_ref[rows, :]
        y = o * lax.rsqrt(jnp.mean(o * o, axis=-1, keepdims=True) + RMS_EPS) * gain_ref[...]
        y_ref[0, rows, :] = (y * og_ref[0, rows, :].astype(F32)).astype(BF16)
        return carry

    lax.fori_loop(0, n_chunks, finish, 0, unroll=2)


def _gla(q, k, v, og, gf, gb, lw, ctx_len):
    bsz, t, _ = q.shape
    assert (t // GLA_CHUNK) % GLA_UNROLL == 0 and ctx_len % GLA_CHUNK == 0
    seq_spec = lambda w: pl.BlockSpec((1, t, w), lambda b, h: (b, 0, h))
    kern = functools.partial(_gla_kernel, n_chunks=t // GLA_CHUNK, n_ctx_chunks=ctx_len // GLA_CHUNK)
    return pl.pallas_call(
        kern, grid=(bsz, GLA_HEADS),
        in_specs=[seq_spec(GLA_DK), seq_spec(GLA_DK), seq_spec(GLA_DV), seq_spec(GLA_DV),
                  seq_spec(GLA_DK), seq_spec(GLA_DK), pl.BlockSpec((1, GLA_DV), lambda b, h: (0, 0))],
        out_specs=seq_spec(GLA_DV),
        out_shape=jax.ShapeDtypeStruct((bsz, t, GLA_HEADS * GLA_DV), BF16),
        scratch_shapes=[pltpu.VMEM((t, GLA_DV), F32), pltpu.VMEM((t, GLA_DV), F32),
                        pltpu.VMEM((GLA_DV, GLA_DK), F32), pltpu.VMEM((GLA_DV, GLA_DK), F32)],
        compiler_params=_params(("parallel", "parallel")), name="gla",
    )(q, k, v, og, gf, gb, lw["gla_gain"])


NAT_GROUP = 4


def _nat_kernel(q_ref, k0_ref, k1_ref, k2_ref, kc_ref, v0_ref, v1_ref, v2_ref, vc_ref, bias_ref, o_ref, s_ref,
                *, off):
    is_ctx = pl.program_id(0) + off == 0
    lane = lax.broadcasted_iota(jnp.int32, (TM, LANES), 1)
    row = lax.broadcasted_iota(jnp.int32, (LANES, TM), 0)
    n_heads = 2 * NAT_GROUP
    n_lat = NAT_KBLOCKS * TM

    def run(lat_keys, lat_vals):
        kcats, vlats = {}, {}

        def scores(i):
            hp, e = divmod(i, 2)
            sl = slice(hp * LANES, (hp + 1) * LANES)
            if hp not in kcats:
                kcats[hp] = jnp.concatenate([r[0, :, sl] for r in lat_keys + [kc_ref]], axis=0)
            q = q_ref[0, :, sl]
            qm = jnp.where((lane >= NAT_HD) if e else (lane < NAT_HD), q, jnp.zeros_like(q))
            s_ref[i % 2, :kcats[hp].shape[0], :] = _dot_nt(kcats[hp], qm)

        scores(0)
        outs = []
        for i in range(n_heads):
            if i + 1 < n_heads:
                scores(i + 1)
            hp, e = divmod(i, 2)
            sl = slice(hp * LANES, (hp + 1) * LANES)
            n_l = n_lat if lat_keys else 0
            s_ctx = s_ref[i % 2, n_l:n_l + TM, :]
            m = jnp.max(s_ctx, axis=0, keepdims=True)
            if lat_keys:
                s_lat = s_ref[i % 2, :n_lat, :] + bias_ref[0, i]
                m = jnp.maximum(m, jnp.max(s_lat, axis=0, keepdims=True))
            p_ctx = jnp.exp2(s_ctx - m)
            l = jnp.sum(p_ctx, axis=0, keepdims=True)
            o = _dot(vc_ref[0, sl, :], p_ctx.astype(BF16))
            if lat_keys:
                if hp not in vlats:
                    vlats[hp] = jnp.concatenate([r[0, sl, :] for r in lat_vals], axis=1)
                p_lat = jnp.exp2(s_lat - m)
                l = l + jnp.sum(p_lat, axis=0, keepdims=True)
                o = o + _dot(vlats[hp], p_lat.astype(BF16))
            outs.append(o * (1.0 / l))
            if e == 1:
                ot = jnp.where(row < NAT_HD, outs[-2], outs[-1])
                o_ref[0, :, sl] = ot.T.astype(BF16)

    @pl.when(is_ctx)
    def _():
        run([], [])

    @pl.when(jnp.logical_not(is_ctx))
    def _():
        run([k0_ref, k1_ref, k2_ref], [v0_ref, v1_ref, v2_ref])


def _nat(q, k, vt, bias, need_ctx):
    bsz, t, n = q.shape
    off = 0 if need_ctx else 1
    nblk = t // TM - 1
    gw = NAT_GROUP * LANES
    ngroups = n // gw

    def kblock(qi, i):
        j = qi + off - 1
        return 1 + jnp.clip(j - 1, 0, nblk - NAT_KBLOCKS) + i

    def bias_class(qi):
        j = qi + off - 1
        return jnp.where(j <= 0, 0, jnp.where(j == nblk - 1, 2, 1))

    kspec = lambda i: pl.BlockSpec((1, TM, gw), lambda qi, hg, b: (b, kblock(qi, i), hg))
    vspec = lambda i: pl.BlockSpec((1, gw, TM), lambda qi, hg, b: (b, hg, kblock(qi, i)))
    return pl.pallas_call(
        functools.partial(_nat_kernel, off=off),
        grid=(t // TM - off, ngroups, bsz),
        in_specs=[pl.BlockSpec((1, TM, gw), lambda qi, hg, b: (b, qi + off, hg)),
                  kspec(0), kspec(1), kspec(2),
                  pl.BlockSpec((1, TM, gw), lambda qi, hg, b: (b, 0, hg)),
                  vspec(0), vspec(1), vspec(2),
                  pl.BlockSpec((1, gw, TM), lambda qi, hg, b: (b, hg, 0)),
                  pl.BlockSpec((1, 2 * NAT_GROUP, NAT_KBLOCKS * TM, TM),
                               lambda qi, hg, b: (bias_class(qi), hg, 0, 0))],
        out_specs=pl.BlockSpec((1, TM, gw), lambda qi, hg, b: (b, qi + off, hg)),
        out_shape=jax.ShapeDtypeStruct((bsz, t, n), BF16),
        scratch_shapes=[pltpu.VMEM((2, (NAT_KBLOCKS + 1) * TM, TM), F32)],
        compiler_params=_params(("parallel", "parallel", "parallel")), name="nat",
    )(q, k, k, k, k, vt, vt, vt, vt, bias)


def _nat_bias_tables(rpb, rows):
    nblk = rows // NAT_QROWS
    kr_win = min(WIN_R, rows)
    col = np.arange(GRID_W)
    c_start = np.clip(col - WIN_C // 2, 0, GRID_W - WIN_C)
    cmask = (col[:, None] >= c_start[None, :]) & (col[:, None] < c_start[None, :] + WIN_C)
    dc = np.clip(col[:, None] - col[None, :] + WIN_C - 1, 0, 2 * WIN_C - 2)
    drs, rmasks = [], []
    for jblk in (0, 1, nblk - 1):
        base = int(np.clip(jblk - 1, 0, nblk - NAT_KBLOCKS))
        r = NAT_QROWS * jblk + np.arange(NAT_QROWS)
        r_start = np.clip(r - kr_win // 2, 0, rows - kr_win)
        kr = NAT_QROWS * base + np.arange(NAT_KBLOCKS * NAT_QROWS)
        rmasks.append((kr[:, None] >= r_start[None, :]) & (kr[:, None] < r_start[None, :] + kr_win))
        drs.append(np.clip(kr[:, None] - r[None, :] + WIN_R - 1, 0, 2 * WIN_R - 2))
    dr = np.stack(drs)
    rmask = np.stack(rmasks)
    a = rpb.astype(F32)[:, :, dc]
    t = a[:, dr]
    t = jnp.transpose(t, (1, 0, 2, 4, 3, 5))
    mask = rmask[:, None, :, None, :, None] & cmask[None, None, None, :, None, :]
    t = jnp.where(mask, t * LOG2E, NEG_INF)
    nk = NAT_KBLOCKS * NAT_QROWS * GRID_W
    return t.reshape(3, rpb.shape[0], nk, NAT_QROWS * GRID_W)


def _gqa_kernel(q_ref, k_ref, vt_ref, o_ref, s_ref, m_ref, l_ref, acc_ref, *, off, n_lat):
    is_ctx = pl.program_id(2) + off == 0
    group = GQA_HEADS // GQA_KV_HEADS
    qs = jnp.concatenate([q_ref[0, :, g * GQA_HD:(g + 1) * GQA_HD] for g in range(group)], axis=0)

    def scores(start, size, slot):
        s_ref[slot, :size, :] = _dot_nt(k_ref[0, start:start + size, :], qs)

    scores(0, TM, 0)
    scores(TM, GQA_KB, 1)
    st = s_ref[0, :TM, :]
    m = jnp.max(st, axis=0, keepdims=True)
    p = jnp.exp2(st - m)
    m_ref[...] = m
    l_ref[...] = jnp.sum(p, axis=0, keepdims=True)
    acc_ref[...] = _dot(vt_ref[0, :, :TM], p.astype(BF16))

    @pl.when(jnp.logical_not(is_ctx))
    def _():
        m = m_ref[...]
        l = l_ref[...]
        for j in range(n_lat):
            start = TM + j * GQA_KB
            if j + 1 < n_lat:
                scores(start + GQA_KB, GQA_KB, j % 2)
            st = s_ref[(j + 1) % 2]
            m_new = jnp.maximum(m, jnp.max(st, axis=0, keepdims=True))
            alpha = jnp.exp2(m - m_new)
            p = jnp.exp2(st - m_new)
            l = alpha * l + jnp.sum(p, axis=0, keepdims=True)
            acc_ref[...] = alpha * acc_ref[...] + _dot(vt_ref[0, :, start:start + GQA_KB], p.astype(BF16))
            m = m_new
        l_ref[...] = l

    ot = acc_ref[...] * (1.0 / l_ref[...])
    for g in range(group):
        o_ref[0, :, g * GQA_HD:(g + 1) * GQA_HD] = ot[:, g * TM:(g + 1) * TM].T.astype(BF16)


def _gqa(q, k, vt, need_ctx):
    bsz, t, nq = q.shape
    off = 0 if need_ctx else 1
    group = GQA_HEADS // GQA_KV_HEADS
    gw = group * GQA_HD
    assert (t - TM) % GQA_KB == 0
    return pl.pallas_call(
        functools.partial(_gqa_kernel, off=off, n_lat=(t - TM) // GQA_KB),
        grid=(bsz, GQA_KV_HEADS, t // TM - off),
        in_specs=[pl.BlockSpec((1, TM, gw), lambda b, h, qi: (b, qi + off, h)),
                  pl.BlockSpec((1, t, GQA_HD), lambda b, h, qi: (b, 0, h)),
                  pl.BlockSpec((1, GQA_HD, t), lambda b, h, qi: (b, h, 0))],
        out_specs=pl.BlockSpec((1, TM, gw), lambda b, h, qi: (b, qi + off, h)),
        out_shape=jax.ShapeDtypeStruct((bsz, t, nq), BF16),
        scratch_shapes=[pltpu.VMEM((2, GQA_KB, group * TM), F32),
                        pltpu.VMEM((1, group * TM), F32), pltpu.VMEM((1, group * TM), F32),
                        pltpu.VMEM((GQA_HD, group * TM), F32)],
        compiler_params=_params(("parallel", "parallel", "arbitrary")), name="gqa",
    )(q, k, vt)


def _merge_kernel(x_ref, gt_ref, y0_ref, y1_ref, y2_ref, sg_ref, wb_ref, wo_ref, o_ref):
    z = None
    for i, y_ref in enumerate((y0_ref, y1_ref, y2_ref)):
        zi = sg_ref[0, :, i * D_MODEL:(i + 1) * D_MODEL].astype(F32) * _dot(y_ref[0], wb_ref[i])
        z = zi if z is None else z + zi
    o_ref[0] = x_ref[0] + gt_ref[0] * _dot(z.astype(BF16), wo_ref[...])


def _merge(xall, mod, ys, sg, wb, wo, off):
    bsz, t, _ = xall.shape
    ctx_row = mod.shape[0] - 1
    return pl.pallas_call(
        _merge_kernel,
        grid=(bsz, t // TM - off),
        in_specs=[_row_spec(off, D_MODEL), _mod_spec(5, off, ctx_row),
                  _row_spec(off, D_MODEL), _row_spec(off, D_MODEL), _row_spec(off, D_MODEL),
                  _row_spec(off, 3 * D_MODEL), _whole(wb, 2), _whole(wo, 2)],
        out_specs=_row_spec(off, D_MODEL),
        out_shape=jax.ShapeDtypeStruct((bsz, t, D_MODEL), F32),
        compiler_params=_params(("parallel", "parallel")), name="merge",
    )(xall, mod, ys[0], ys[1], ys[2], sg, wb, wo)


def _block_diag_ones(head_dim):
    i = np.arange(MXU)
    return jnp.asarray((i[:, None] // head_dim) == (i[None, :] // head_dim), dtype=BF16)


def _chunk_tri(upper):
    i = np.arange(TM)
    same = (i[:, None] // GLA_CHUNK) == (i[None, :] // GLA_CHUNK)
    tri = (i[None, :] >= i[:, None]) if upper else (i[:, None] >= i[None, :])
    return jnp.asarray(same & tri, dtype=BF16)


def _rope_tables(ctx_len, seq):
    quarter = GQA_HD // 4
    freqs = ROPE_BASE ** (-np.arange(quarter, dtype=np.float64) / quarter)
    tok = np.arange(seq)
    ang_r = (tok // GRID_W)[:, None] * freqs
    ang_c = (tok % GRID_W)[:, None] * freqs
    ang = np.concatenate([ang_r, ang_r, ang_c, ang_c], axis=1)
    sign = np.tile(np.concatenate([-np.ones(quarter), np.ones(quarter)]), 2)
    cos = np.concatenate([np.ones((ctx_len, GQA_HD)), np.cos(ang)], axis=0)
    sin = np.concatenate([np.zeros((ctx_len, GQA_HD)), np.sin(ang) * sign], axis=0)
    return jnp.asarray(cos, F32), jnp.asarray(sin, F32)


def _layer_weights(l, ffn_w_in, ffn_w_out, w_in, gla_fg_w2, gla_fg_b, gla_norm_g, nat_q_norm, nat_k_norm,
                   gqa_q_norm, gqa_k_norm, w_branch, w_out):
    w = w_in[l]
    nqk = GLA_HEADS * GLA_DK
    nv = GLA_HEADS * GLA_DV
    o_fg = 2 * nqk + 2 * nv
    o_nat = o_fg + 2 * GLA_LR
    n = NAT_HEADS * NAT_HD
    o_gqa = o_nat + 3 * n
    nq = GQA_HEADS * GQA_HD
    nkv = GQA_KV_HEADS * GQA_HD
    o_gate = o_gqa + nq + 2 * nkv
    pad = jnp.zeros((D_MODEL, LANES - 2 * GLA_LR), w.dtype)
    w2 = gla_fg_w2[l]
    zeros_lr = jnp.zeros((GLA_LR, nqk), w2.dtype)
    zeros_rest = jnp.zeros((LANES - 2 * GLA_LR, nqk), w2.dtype)
    return {
        "ffn": [(ffn_w_in[l, i, :, :D_FF].astype(BF16), ffn_w_in[l, i, :, D_FF:].astype(BF16),
                 ffn_w_out[l, i].astype(BF16)) for i in range(2)],
        "w_gla": jnp.concatenate([w[:, :o_fg], w[:, o_fg:o_nat], pad], axis=1).astype(BF16),
        "w_nat_qk": w[:, o_nat:o_nat + 2 * n].astype(BF16),
        "w_nat_vt": w[:, o_nat + 2 * n:o_gqa].T.astype(BF16),
        "w_gqa_qk": w[:, o_gqa:o_gqa + nq + nkv].astype(BF16),
        "w_gqa_vt": w[:, o_gqa + nq + nkv:o_gate].T.astype(BF16),
        "w_gates": w[:, o_gate:].astype(BF16),
        "w2": jnp.concatenate([jnp.concatenate([w2[0], zeros_lr, zeros_rest], axis=0),
                               jnp.concatenate([zeros_lr, w2[1], zeros_rest], axis=0)], axis=1).astype(BF16),
        "b2": gla_fg_b[l].reshape(1, 2 * nqk),
        "tril": _chunk_tri(False),
        "triu": _chunk_tri(True),
        "gla_gain": gla_norm_g[l].reshape(1, GLA_DV),
        "nat_gq": jnp.tile(nat_q_norm[l], NAT_HEADS).reshape(1, n),
        "nat_gk": jnp.tile(nat_k_norm[l], NAT_HEADS).reshape(1, n),
        "gqa_gq": jnp.tile(gqa_q_norm[l], GQA_HEADS).reshape(1, nq),
        "gqa_gk": jnp.tile(gqa_k_norm[l], GQA_KV_HEADS).reshape(1, nkv),
        "ones64": _block_diag_ones(NAT_HD),
        "ones128": _block_diag_ones(GQA_HD),
        "w_branch": w_branch[l].astype(BF16),
        "w_out": w_out[l].astype(BF16),
    }


def kernel(x, c, ctx, c_ctx, w_mod, b_mod, norm_g, ffn_w_in, ffn_w_out, w_in, gla_fg_w2, gla_fg_b, gla_norm_g,
           nat_q_norm, nat_k_norm, nat_rpb, gqa_q_norm, gqa_k_norm, w_branch, w_out):
    bsz, seq, _ = x.shape
    ctx_len = ctx.shape[1]
    depth = w_mod.shape[0]
    assert seq % TM == 0 and ctx_len == TM and seq // TM >= NAT_KBLOCKS

    mod_rows = -(-(bsz + 1) // 8) * 8
    cvec = jnp.concatenate([c, c_ctx[None, :], jnp.zeros((mod_rows - bsz - 1, D_MODEL), c.dtype)], axis=0)
    mods = _mod_table(cvec, w_mod, b_mod)[:, :bsz + 1].reshape(depth, bsz + 1, 1, N_MOD * D_MODEL)
    rope_cos, rope_sin = _rope_tables(ctx_len, seq)

    xall = jnp.concatenate([ctx, x], axis=1)
    t = ctx_len + seq
    for l in range(depth):
        need_ctx = l < depth - 1
        off = 0 if need_ctx else 1
        lw = _layer_weights(l, ffn_w_in, ffn_w_out, w_in, gla_fg_w2, gla_fg_b, gla_norm_g, nat_q_norm,
                            nat_k_norm, gqa_q_norm, gqa_k_norm, w_branch, w_out)
        mod = mods[l]
        g = [norm_g[l, i].reshape(1, D_MODEL) for i in range(3)]
        xall = _ffn(xall, mod, g[0], *lw["ffn"][0], sub=0, off=0, out_off=0, out_tokens=t)
        (gq, gk, gv, gog, ggf, ggb), (nq_, nk_, nvt), (aq, ak, avt, sg) = _projections(
            xall, mod, g[1], lw, rope_cos, rope_sin)
        y_gla = _gla(gq, gk, gv, gog, ggf, ggb, lw, ctx_len)
        y_nat = _nat(nq_, nk_, nvt, _nat_bias_tables(nat_rpb[l], seq // GRID_W), need_ctx)
        y_gqa = _gqa(aq, ak, avt, need_ctx)
        xall = _merge(xall, mod, (y_gla, y_nat, y_gqa), sg, lw["w_branch"], lw["w_out"], off)
        if need_ctx:
            xall = _ffn(xall, mod, g[2], *lw["ffn"][1], sub=2, off=0, out_off=0, out_tokens=t)
        else:
            xall = _ffn(xall, mod, g[2], *lw["ffn"][1], sub=2, off=1, out_off=0, out_tokens=seq)
    return xall
```

```python
import functools

import numpy as np
import jax
import jax.numpy as jnp
from jax import lax
from jax.experimental import pallas as pl
from jax.experimental.pallas import tpu as pltpu

F32 = jnp.float32
BF16 = jnp.bfloat16

D_MODEL = 1024
GRID_W = 64
N_MOD = 9
D_FF = 2816
RMS_EPS = 1e-6
NEG_INF = -1e30

GLA_HEADS = 4
GLA_DK = 128
GLA_DV = 256
GLA_LR = 16
GLA_TAU = 16.0
GLA_CHUNK = 128

NAT_HEADS = 16
NAT_HD = 64
WIN_R = 8
WIN_C = 16
NAT_QROWS = 4
NAT_KBLOCKS = 3

GQA_HEADS = 8
GQA_KV_HEADS = 2
GQA_HD = 128
ROPE_BASE = 10000.0
GQA_KB = 512
LOG2E = 1.4426950408889634

TM = 256
LANES = 128
MXU = 256
VMEM_LIMIT = 56 * 1024 * 1024


def _dot(a, b):
    return jnp.dot(a, b, preferred_element_type=F32)


def _dot_nt(a, b):
    return lax.dot_general(a, b, (((1,), (1,)), ((), ())), preferred_element_type=F32)


def _dot_tn(a, b):
    return lax.dot_general(a, b, (((0,), (0,)), ((), ())), preferred_element_type=F32)


def _sigmoid(x):
    return 1.0 / (1.0 + jnp.exp(-x))


def _log_sigmoid(x):
    return -(jnp.maximum(-x, 0.0) + jnp.log(1.0 + jnp.exp(-jnp.abs(x))))


def _norm_mod(x, g, shift, scale):
    y = x * lax.rsqrt(jnp.mean(x * x, axis=-1, keepdims=True) + RMS_EPS) * g
    return y * (1.0 + scale) + shift


def _head_norm(t, gain, ones_bd, head_dim):
    sq = (t * t).astype(BF16)
    parts = [_dot(sq[:, i:i + MXU], ones_bd) for i in range(0, t.shape[1], MXU)]
    ss = parts[0] if len(parts) == 1 else jnp.concatenate(parts, axis=1)
    return t * lax.rsqrt(ss * (1.0 / head_dim) + RMS_EPS) * gain


def _params(semantics):
    return pltpu.CompilerParams(dimension_semantics=semantics, vmem_limit_bytes=VMEM_LIMIT)


def _whole(arr, ngrid):
    zeros = (0,) * arr.ndim
    return pl.BlockSpec(arr.shape, lambda *_: zeros, pipeline_mode=pl.Buffered(1))


def _row_spec(off, width):
    return pl.BlockSpec((1, TM, width), lambda b, j: (b, j + off, 0))


def _mod_spec(col, off, ctx_row):
    return pl.BlockSpec((1, 1, D_MODEL), lambda b, j: (jnp.where(j + off == 0, ctx_row, b), 0, col))


def _mod_kernel(c_ref, w_ref, b_ref, o_ref):
    c = c_ref[...]
    s = (c * _sigmoid(c)).astype(BF16)
    o_ref[0] = _dot(s, w_ref[0].astype(BF16)) + b_ref[0]


def _mod_table(cvec, w_mod, b_mod):
    depth = w_mod.shape[0]
    rows = cvec.shape[0]
    return pl.pallas_call(
        _mod_kernel,
        grid=(depth, N_MOD),
        in_specs=[pl.BlockSpec((rows, D_MODEL), lambda l, n: (0, 0)),
                  pl.BlockSpec((1, D_MODEL, D_MODEL), lambda l, n: (l, 0, n)),
                  pl.BlockSpec((1, 1, D_MODEL), lambda l, n: (l, 0, n))],
        out_specs=pl.BlockSpec((1, rows, D_MODEL), lambda l, n: (l, 0, n)),
        out_shape=jax.ShapeDtypeStruct((depth, rows, N_MOD * D_MODEL), F32),
        compiler_params=_params(("arbitrary", "arbitrary")),
        name="mod_table",
    )(cvec, w_mod, b_mod.reshape(depth, 1, N_MOD * D_MODEL))


def _ffn_kernel(x_ref, ctx_ref, sh_ref, sc_ref, gt_ref, g_ref, wa_ref, wb_ref, wd_ref, o_ref, *, split_input):
    x = jnp.where(pl.program_id(1) == 0, ctx_ref[0], x_ref[0]) if split_input else x_ref[0]
    u = _norm_mod(x, g_ref[...], sh_ref[0], sc_ref[0]).astype(BF16)
    a = _dot(u, wa_ref[...])
    b = _dot(u, wb_ref[...])
    h = (a * _sigmoid(a) * b).astype(BF16)
    o_ref[0] = x + (0.5 * gt_ref[0]) * _dot(h, wd_ref[...])


def _ffn(xall, mod, g, wa, wb, wd, sub, off, out_off, out_tokens, ctx=None):
    bsz, t, _ = xall.shape
    split = ctx is not None
    if split:
        t += ctx.shape[1]
        x_spec = pl.BlockSpec((1, TM, D_MODEL), lambda b, j: (b, jnp.maximum(j - 1, 0), 0))
        ctx_spec = pl.BlockSpec((1, TM, D_MODEL), lambda b, j: (b, 0, 0))
    else:
        ctx = xall
        x_spec = _row_spec(off, D_MODEL)
        ctx_spec = pl.BlockSpec((1, 8, D_MODEL), lambda b, j: (b, 0, 0))
    ctx_row = mod.shape[0] - 1
    return pl.pallas_call(
        functools.partial(_ffn_kernel, split_input=split),
        grid=(bsz, t // TM - off),
        in_specs=[x_spec, ctx_spec,
                  _mod_spec(3 * sub, off, ctx_row), _mod_spec(3 * sub + 1, off, ctx_row),
                  _mod_spec(3 * sub + 2, off, ctx_row),
                  _whole(g, 2), _whole(wa, 2), _whole(wb, 2), _whole(wd, 2)],
        out_specs=_row_spec(out_off, D_MODEL),
        out_shape=jax.ShapeDtypeStruct((bsz, out_tokens, D_MODEL), F32),
        compiler_params=_params(("parallel", "parallel")),
        name="ffn",
    )(xall, ctx, mod, mod, mod, g, wa, wb, wd)


def _chunk_cumsum(tri, la):
    hi = la.astype(BF16)
    lo = (la - hi.astype(F32)).astype(BF16)
    return _dot(tri, hi) + _dot(tri, lo)


def _proj_gla_kernel(x_ref, sh_ref, sc_ref, g_ref, w_ref, wfg_ref, w2_ref, b2_ref, tril_ref, triu_ref,
                     q_ref, k_ref, v_ref, og_ref, gf_ref, gb_ref):
    u = _norm_mod(x_ref[0], g_ref[...], sh_ref[0], sc_ref[0]).astype(BF16)
    nqk = GLA_HEADS * GLA_DK
    nv = GLA_HEADS * GLA_DV
    fg = _dot(u, wfg_ref[...]).astype(BF16)
    q_ref[0] = (_dot(u, w_ref[:, :nqk]) * GLA_DK ** -0.5).astype(BF16)
    z = _dot(fg, w2_ref[...]) + b2_ref[...]
    k_ref[0] = _dot(u, w_ref[:, nqk:2 * nqk]).astype(BF16)
    la = _log_sigmoid(z) * (1.0 / GLA_TAU)
    v_ref[0] = _dot(u, w_ref[:, 2 * nqk:2 * nqk + nv]).astype(BF16)
    gf_ref[0] = _chunk_cumsum(tril_ref[...], la[:, :nqk])
    gb_ref[0] = _chunk_cumsum(triu_ref[...], la[:, nqk:])
    og = _dot(u, w_ref[:, 2 * nqk + nv:])
    og_ref[0] = (og * _sigmoid(og)).astype(BF16)


def _proj_nat_kernel(x_ref, sh_ref, sc_ref, g_ref, wqk_ref, wvt_ref, gq_ref, gk_ref, ones_ref,
                     q_ref, k_ref, vt_ref):
    u = _norm_mod(x_ref[0], g_ref[...], sh_ref[0], sc_ref[0]).astype(BF16)
    p = _dot(u, wqk_ref[...])
    n = NAT_HEADS * NAT_HD
    ones_bd = ones_ref[...]
    q_ref[0] = (_head_norm(p[:, :n], gq_ref[...], ones_bd, NAT_HD) * (NAT_HD ** -0.5 * LOG2E)).astype(BF16)
    k_ref[0] = _head_norm(p[:, n:], gk_ref[...], ones_bd, NAT_HD).astype(BF16)
    vt_ref[0] = _dot_nt(wvt_ref[...], u).astype(BF16)


def _rope(t, cos, sin, lane):
    partner = jnp.where(lane % (GQA_HD // 2) < GQA_HD // 4,
                        pltpu.roll(t, GQA_HD - GQA_HD // 4, axis=1), pltpu.roll(t, GQA_HD // 4, axis=1))
    return t * cos + partner * sin


def _proj_gqa_kernel(x_ref, sh_ref, sc_ref, g_ref, wqk_ref, wvt_ref, wg_ref, gq_ref, gk_ref, ones_ref,
                     cos_ref, sin_ref, q_ref, k_ref, vt_ref, sg_ref):
    u = _norm_mod(x_ref[0], g_ref[...], sh_ref[0], sc_ref[0]).astype(BF16)
    p = _dot(u, wqk_ref[...])
    nq = GQA_HEADS * GQA_HD
    ones_bd = ones_ref[...]
    qn = _head_norm(p[:, :nq], gq_ref[...], ones_bd, GQA_HD)
    kn = _head_norm(p[:, nq:], gk_ref[...], ones_bd, GQA_HD)
    cos = cos_ref[...]
    sin = sin_ref[...]
    lane = lax.broadcasted_iota(jnp.int32, (TM, GQA_HD), 1)
    for h in range(GQA_HEADS):
        sl = slice(h * GQA_HD, (h + 1) * GQA_HD)
        q_ref[0, :, sl] = (_rope(qn[:, sl], cos, sin, lane) * (GQA_HD ** -0.5 * LOG2E)).astype(BF16)
    for h in range(GQA_KV_HEADS):
        sl = slice(h * GQA_HD, (h + 1) * GQA_HD)
        k_ref[0, :, sl] = _rope(kn[:, sl], cos, sin, lane).astype(BF16)
    vt_ref[0] = _dot_nt(wvt_ref[...], u).astype(BF16)
    sg_ref[0] = _sigmoid(_dot(u, wg_ref[...])).astype(BF16)


def _col_spec(rows, off):
    return pl.BlockSpec((1, rows, TM), lambda b, j: (b, 0, j + off))


def _projections(xall, mod, g, lw, rope_cos, rope_sin):
    bsz, t, _ = xall.shape
    ctx_row = mod.shape[0] - 1
    grid = (bsz, t // TM)
    common = [_row_spec(0, D_MODEL), _mod_spec(3, 0, ctx_row), _mod_spec(4, 0, ctx_row), _whole(g, 2)]
    sds = jax.ShapeDtypeStruct
    nqk = GLA_HEADS * GLA_DK
    nv = GLA_HEADS * GLA_DV

    gla = pl.pallas_call(
        _proj_gla_kernel, grid=grid,
        in_specs=common + [_whole(lw["w_gla"], 2), _whole(lw["w_fg"], 2), _whole(lw["w2"], 2), _whole(lw["b2"], 2),
                           _whole(lw["tril"], 2), _whole(lw["triu"], 2)],
        out_specs=[_row_spec(0, nqk), _row_spec(0, nqk), _row_spec(0, nv), _row_spec(0, nv),
                   _row_spec(0, nqk), _row_spec(0, nqk)],
        out_shape=[sds((bsz, t, nqk), BF16), sds((bsz, t, nqk), BF16), sds((bsz, t, nv), BF16),
                   sds((bsz, t, nv), BF16), sds((bsz, t, nqk), F32), sds((bsz, t, nqk), F32)],
        compiler_params=_params(("parallel", "parallel")), name="proj_gla",
    )(xall, mod, mod, g, lw["w_gla"], lw["w_fg"], lw["w2"], lw["b2"], lw["tril"], lw["triu"])

    n = NAT_HEADS * NAT_HD
    nat = pl.pallas_call(
        _proj_nat_kernel, grid=grid,
        in_specs=common + [_whole(lw["w_nat_qk"], 2), _whole(lw["w_nat_vt"], 2), _whole(lw["nat_gq"], 2),
                           _whole(lw["nat_gk"], 2), _whole(lw["ones64"], 2)],
        out_specs=[_row_spec(0, n), _row_spec(0, n), _col_spec(n, 0)],
        out_shape=[sds((bsz, t, n), BF16), sds((bsz, t, n), BF16), sds((bsz, n, t), BF16)],
        compiler_params=_params(("parallel", "parallel")), name="proj_nat",
    )(xall, mod, mod, g, lw["w_nat_qk"], lw["w_nat_vt"], lw["nat_gq"], lw["nat_gk"], lw["ones64"])

    nq = GQA_HEADS * GQA_HD
    nkv = GQA_KV_HEADS * GQA_HD
    tab = pl.BlockSpec((TM, GQA_HD), lambda b, j: (j, 0))
    gqa = pl.pallas_call(
        _proj_gqa_kernel, grid=grid,
        in_specs=common + [_whole(lw["w_gqa_qk"], 2), _whole(lw["w_gqa_vt"], 2), _whole(lw["w_gates"], 2),
                           _whole(lw["gqa_gq"], 2), _whole(lw["gqa_gk"], 2), _whole(lw["ones128"], 2),
                           tab, tab],
        out_specs=[_row_spec(0, nq), _row_spec(0, nkv), _col_spec(nkv, 0), _row_spec(0, 3 * D_MODEL)],
        out_shape=[sds((bsz, t, nq), BF16), sds((bsz, t, nkv), BF16), sds((bsz, nkv, t), BF16),
                   sds((bsz, t, 3 * D_MODEL), BF16)],
        compiler_params=_params(("parallel", "parallel")), name="proj_gqa",
    )(xall, mod, mod, g, lw["w_gqa_qk"], lw["w_gqa_vt"], lw["w_gates"], lw["gqa_gq"], lw["gqa_gk"],
      lw["ones128"], rope_cos, rope_sin)
    return gla, nat, gqa


GLA_UNROLL = 2


def _gla_chain(n, forward, q_ref, k_ref, v_ref, g_ref, o_ref, state, tri):
    c = GLA_CHUNK
    rows = pl.ds(pl.multiple_of(n * c, c), c)
    g = g_ref[0, rows, :]
    if forward:
        g_mid = g[c // 2 - 1:c // 2, :]
        g_edge = g[c - 1:c, :]
    else:
        g_mid = g[c // 2:c // 2 + 1, :]
        g_edge = g[0:1, :]
    q = q_ref[0, rows, :].astype(F32)
    k = k_ref[0, rows, :].astype(F32)
    v = v_ref[0, rows, :]
    q_in = (q * jnp.exp(g - g_mid)).astype(BF16)
    k_in = (k * jnp.exp(g_mid - g)).astype(BF16)
    q_x = (q * jnp.exp(g)).astype(BF16)
    k_end = (k * jnp.exp(g_edge - g)).astype(BF16)
    decay = jnp.exp(g_edge)
    yield
    att = _dot_nt(q_in, k_in)
    s_in = state["s"]
    state["s"] = s_in * decay + _dot_tn(v, k_end)
    yield
    att = jnp.where(tri, att, 0.0).astype(BF16)
    o = _dot(att, v) + _dot_nt(q_x, s_in.astype(BF16))
    yield
    o_ref[rows, :] = o


def _gla_kernel(q_ref, k_ref, v_ref, og_ref, gf_ref, gb_ref, gain_ref, y_ref,
                of_ref, ob_ref, sf_ref, sb_ref, *, n_chunks, n_ctx_chunks):
    c = GLA_CHUNK
    ri = lax.broadcasted_iota(jnp.int32, (c, c), 0)
    ci = lax.broadcasted_iota(jnp.int32, (c, c), 1)
    lower = ri >= ci
    upper = ci >= ri
    sf_ref[...] = jnp.zeros_like(sf_ref)
    sb_ref[...] = jnp.zeros_like(sb_ref)

    def scan(it, carry):
        fwd = {"s": sf_ref[...]}
        bwd = {"s": sb_ref[...]}
        chains = []
        for u in range(GLA_UNROLL):
            s = it * GLA_UNROLL + u
            nb = jnp.where(s < n_ctx_chunks, n_ctx_chunks - 1 - s, n_chunks - 1 - (s - n_ctx_chunks))
            chains.append(_gla_chain(s, True, q_ref, k_ref, v_ref, gf_ref, of_ref, fwd, lower))
            chains.append(_gla_chain(nb, False, q_ref, k_ref, v_ref, gb_ref, ob_ref, bwd, upper))
        while chains:
            alive = []
            for ch in chains:
                try:
                    next(ch)
                    alive.append(ch)
                except StopIteration:
                    pass
            chains = alive
        sf_ref[...] = fwd["s"]
        sb_ref[...] = bwd["s"]
        return carry

    lax.fori_loop(0, n_chunks // GLA_UNROLL, scan, 0)

    def finish(n, carry):
        rows = pl.ds(pl.multiple_of(n * c, c), c)
        o = of_ref[rows, :] + ob_ref[rows, :]
        y = o * lax.rsqrt(jnp.mean(o * o, axis=-1, keepdims=True) + RMS_EPS) * gain_ref[...]
        y_ref[0, rows, :] = (y * og_ref[0, rows, :].astype(F32)).astype(BF16)
        return carry

    lax.fori_loop(0, n_chunks, finish, 0, unroll=2)


def _gla(q, k, v, og, gf, gb, lw, ctx_len):
    bsz, t, _ = q.shape
    assert (t // GLA_CHUNK) % GLA_UNROLL == 0 and ctx_len % GLA_CHUNK == 0
    seq_spec = lambda w: pl.BlockSpec((1, t, w), lambda b, h: (b, 0, h))
    kern = functools.partial(_gla_kernel, n_chunks=t // GLA_CHUNK, n_ctx_chunks=ctx_len // GLA_CHUNK)
    return pl.pallas_call(
        kern, grid=(bsz, GLA_HEADS),
        in_specs=[seq_spec(GLA_DK), seq_spec(GLA_DK), seq_spec(GLA_DV), seq_spec(GLA_DV),
                  seq_spec(GLA_DK), seq_spec(GLA_DK), pl.BlockSpec((1, GLA_DV), lambda b, h: (0, 0))],
        out_specs=seq_spec(GLA_DV),
        out_shape=jax.ShapeDtypeStruct((bsz, t, GLA_HEADS * GLA_DV), BF16),
        scratch_shapes=[pltpu.VMEM((t, GLA_DV), F32), pltpu.VMEM((t, GLA_DV), F32),
                        pltpu.VMEM((GLA_DV, GLA_DK), F32), pltpu.VMEM((GLA_DV, GLA_DK), F32)],
        compiler_params=_params(("parallel", "parallel")), name="gla",
    )(q, k, v, og, gf, gb, lw["gla_gain"])


NAT_GROUP = 8


def _nat_kernel(q_ref, k0_ref, k1_ref, k2_ref, kc_ref, v0_ref, v1_ref, v2_ref, vc_ref, bias_ref, o_ref, s_ref,
                *, off):
    is_ctx = pl.program_id(0) + off == 0
    lane = lax.broadcasted_iota(jnp.int32, (TM, LANES), 1)
    row = lax.broadcasted_iota(jnp.int32, (LANES, TM), 0)
    n_heads = 2 * NAT_GROUP
    n_lat = NAT_KBLOCKS * TM

    def run(lat_keys, lat_vals):
        kcats, vlats = {}, {}

        def scores(i):
            hp, e = divmod(i, 2)
            sl = slice(hp * LANES, (hp + 1) * LANES)
            if hp not in kcats:
                kcats[hp] = jnp.concatenate([r[0, :, sl] for r in lat_keys + [kc_ref]], axis=0)
            q = q_ref[0, :, sl]
            qm = jnp.where((lane >= NAT_HD) if e else (lane < NAT_HD), q, jnp.zeros_like(q))
            s_ref[i % 2, :kcats[hp].shape[0], :] = _dot_nt(kcats[hp], qm)

        scores(0)
        outs = []
        for i in range(n_heads):
            if i + 1 < n_heads:
                scores(i + 1)
            hp, e = divmod(i, 2)
            sl = slice(hp * LANES, (hp + 1) * LANES)
            n_l = n_lat if lat_keys else 0
            s_ctx = s_ref[i % 2, n_l:n_l + TM, :]
            m = jnp.max(s_ctx, axis=0, keepdims=True)
            if lat_keys:
                s_lat = s_ref[i % 2, :n_lat, :] + bias_ref[i, 0]
                m = jnp.maximum(m, jnp.max(s_lat, axis=0, keepdims=True))
            p_ctx = jnp.exp2(s_ctx - m)
            l = jnp.sum(p_ctx, axis=0, keepdims=True)
            o = _dot(vc_ref[0, sl, :], p_ctx.astype(BF16))
            if lat_keys:
                if hp not in vlats:
                    vlats[hp] = jnp.concatenate([r[0, sl, :] for r in lat_vals], axis=1)
                p_lat = jnp.exp2(s_lat - m)
                l = l + jnp.sum(p_lat, axis=0, keepdims=True)
                o = o + _dot(vlats[hp], p_lat.astype(BF16))
            outs.append(o * (1.0 / l))
            if e == 1:
                ot = jnp.where(row < NAT_HD, outs[-2], outs[-1])
                o_ref[0, :, sl] = ot.T.astype(BF16)

    @pl.when(is_ctx)
    def _():
        run([], [])

    @pl.when(jnp.logical_not(is_ctx))
    def _():
        run([k0_ref, k1_ref, k2_ref], [v0_ref, v1_ref, v2_ref])


def _nat(q, k, vt, bias, need_ctx):
    bsz, t, n = q.shape
    off = 0 if need_ctx else 1
    nblk = t // TM - 1
    gw = NAT_GROUP * LANES
    ngroups = n // gw

    def kblock(qi, i):
        j = qi + off - 1
        return 1 + jnp.clip(j - 1, 0, nblk - NAT_KBLOCKS) + i

    def bias_class(qi):
        j = qi + off - 1
        return jnp.where(j <= 0, 0, jnp.where(j == nblk - 1, 2, 1))

    kspec = lambda i: pl.BlockSpec((1, TM, gw), lambda qi, hg, b: (b, kblock(qi, i), hg))
    vspec = lambda i: pl.BlockSpec((1, gw, TM), lambda qi, hg, b: (b, hg, kblock(qi, i)))
    return pl.pallas_call(
        functools.partial(_nat_kernel, off=off),
        grid=(t // TM - off, ngroups, bsz),
        in_specs=[pl.BlockSpec((1, TM, gw), lambda qi, hg, b: (b, qi + off, hg)),
                  kspec(0), kspec(1), kspec(2),
                  pl.BlockSpec((1, TM, gw), lambda qi, hg, b: (b, 0, hg)),
                  vspec(0), vspec(1), vspec(2),
                  pl.BlockSpec((1, gw, TM), lambda qi, hg, b: (b, hg, 0)),
                  pl.BlockSpec((2 * NAT_GROUP, 1, NAT_KBLOCKS * TM, TM),
                               lambda qi, hg, b: (hg, bias_class(qi), 0, 0))],
        out_specs=pl.BlockSpec((1, TM, gw), lambda qi, hg, b: (b, qi + off, hg)),
        out_shape=jax.ShapeDtypeStruct((bsz, t, n), BF16),
        scratch_shapes=[pltpu.VMEM((2, (NAT_KBLOCKS + 1) * TM, TM), F32)],
        compiler_params=_params(("parallel", "parallel", "parallel")), name="nat",
    )(q, k, k, k, k, vt, vt, vt, vt, bias)


def _nat_bias_tables(rpb, rows):
    nblk = rows // NAT_QROWS
    kr_win = min(WIN_R, rows)
    col = np.arange(GRID_W)
    c_start = np.clip(col - WIN_C // 2, 0, GRID_W - WIN_C)
    cmask = (col[:, None] >= c_start[None, :]) & (col[:, None] < c_start[None, :] + WIN_C)
    dc = np.clip(col[:, None] - col[None, :] + WIN_C - 1, 0, 2 * WIN_C - 2)
    drs, rmasks = [], []
    for jblk in (0, 1, nblk - 1):
        base = int(np.clip(jblk - 1, 0, nblk - NAT_KBLOCKS))
        r = NAT_QROWS * jblk + np.arange(NAT_QROWS)
        r_start = np.clip(r - kr_win // 2, 0, rows - kr_win)
        kr = NAT_QROWS * base + np.arange(NAT_KBLOCKS * NAT_QROWS)
        rmasks.append((kr[:, None] >= r_start[None, :]) & (kr[:, None] < r_start[None, :] + kr_win))
        drs.append(np.clip(kr[:, None] - r[None, :] + WIN_R - 1, 0, 2 * WIN_R - 2))
    drx = np.where(np.stack(rmasks), np.stack(drs), 2 * WIN_R - 1)
    dcx = np.where(cmask, dc, 2 * WIN_C - 1)
    nh = rpb.shape[0]
    ext = jnp.full((nh, 2 * WIN_R, 2 * WIN_C), NEG_INF, F32)
    ext = ext.at[:, :2 * WIN_R - 1, :2 * WIN_C - 1].set(rpb.astype(F32) * LOG2E)
    a = ext[:, :, dcx]
    t = jnp.concatenate([a[:, drx[:, :, ri]] for ri in range(NAT_QROWS)], axis=-1)
    nk = NAT_KBLOCKS * NAT_QROWS * GRID_W
    return t.reshape(nh, 3, nk, NAT_QROWS * GRID_W)


def _gqa_kernel(q_ref, k_ref, vt_ref, o_ref, s_ref, acc_ref, *, off, n_lat):
    is_ctx = pl.program_id(1) + off == 0
    group = GQA_HEADS // GQA_KV_HEADS

    def run(n_blocks):
        qs = {}

        def scores(h, start, size, slot):
            if h not in qs:
                heads = range(h * group, (h + 1) * group)
                qs[h] = jnp.concatenate([q_ref[0, :, g * GQA_HD:(g + 1) * GQA_HD] for g in heads], axis=0)
            s_ref[h, slot, :size, :] = _dot_nt(k_ref[0, start:start + size, h * GQA_HD:(h + 1) * GQA_HD], qs[h])

        def start_head(h):
            scores(h, 0, TM, 0)
            if n_blocks:
                scores(h, TM, GQA_KB, 1)

        start_head(0)
        for h in range(GQA_KV_HEADS):
            rows = slice(h * GQA_HD, (h + 1) * GQA_HD)
            if n_blocks == 0 and h + 1 < GQA_KV_HEADS:
                start_head(h + 1)
            st = s_ref[h, 0, :TM, :]
            m = jnp.max(st, axis=0, keepdims=True)
            p = jnp.exp2(st - m)
            l = jnp.sum(p, axis=0, keepdims=True)
            acc_ref[h] = _dot(vt_ref[0, rows, :TM], p.astype(BF16))
            for j in range(n_blocks):
                start = TM + j * GQA_KB
                if j + 1 < n_blocks:
                    scores(h, start + GQA_KB, GQA_KB, j % 2)
                elif h + 1 < GQA_KV_HEADS:
                    start_head(h + 1)
                st = s_ref[h, (j + 1) % 2]
                m_new = jnp.maximum(m, jnp.max(st, axis=0, keepdims=True))
                alpha = jnp.exp2(m - m_new)
                p = jnp.exp2(st - m_new)
                l = alpha * l + jnp.sum(p, axis=0, keepdims=True)
                acc_ref[h] = alpha * acc_ref[h] + _dot(vt_ref[0, rows, start:start + GQA_KB], p.astype(BF16))
                m = m_new
            ot = acc_ref[h] * (1.0 / l)
            for g in range(group):
                cols = slice((h * group + g) * GQA_HD, (h * group + g + 1) * GQA_HD)
                o_ref[0, :, cols] = ot[:, g * TM:(g + 1) * TM].T.astype(BF16)

    @pl.when(is_ctx)
    def _():
        run(0)

    @pl.when(jnp.logical_not(is_ctx))
    def _():
        run(n_lat)


def _gqa(q, k, vt, need_ctx):
    bsz, t, nq = q.shape
    nkv = k.shape[2]
    off = 0 if need_ctx else 1
    group = GQA_HEADS // GQA_KV_HEADS
    assert (t - TM) % GQA_KB == 0
    return pl.pallas_call(
        functools.partial(_gqa_kernel, off=off, n_lat=(t - TM) // GQA_KB),
        grid=(bsz, t // TM - off),
        in_specs=[pl.BlockSpec((1, TM, nq), lambda b, qi: (b, qi + off, 0)),
                  pl.BlockSpec((1, t, nkv), lambda b, qi: (b, 0, 0)),
                  pl.BlockSpec((1, nkv, t), lambda b, qi: (b, 0, 0))],
        out_specs=pl.BlockSpec((1, TM, nq), lambda b, qi: (b, qi + off, 0)),
        out_shape=jax.ShapeDtypeStruct((bsz, t, nq), BF16),
        scratch_shapes=[pltpu.VMEM((GQA_KV_HEADS, 2, GQA_KB, group * TM), F32),
                        pltpu.VMEM((GQA_KV_HEADS, GQA_HD, group * TM), F32)],
        compiler_params=_params(("parallel", "arbitrary")), name="gqa",
    )(q, k, vt)


def _merge_kernel(x_ref, gt_ref, y0_ref, y1_ref, y2_ref, sg_ref, wb_ref, wo_ref, o_ref):
    z = None
    for i, y_ref in enumerate((y0_ref, y1_ref, y2_ref)):
        zi = sg_ref[0, :, i * D_MODEL:(i + 1) * D_MODEL].astype(F32) * _dot(y_ref[0], wb_ref[i])
        z = zi if z is None else z + zi
    o_ref[0] = x_ref[0] + gt_ref[0] * _dot(z.astype(BF16), wo_ref[...])


def _merge(xall, mod, ys, sg, wb, wo, off):
    bsz, t, _ = xall.shape
    ctx_row = mod.shape[0] - 1
    return pl.pallas_call(
        _merge_kernel,
        grid=(bsz, t // TM - off),
        in_specs=[_row_spec(off, D_MODEL), _mod_spec(5, off, ctx_row),
                  _row_spec(off, D_MODEL), _row_spec(off, D_MODEL), _row_spec(off, D_MODEL),
                  _row_spec(off, 3 * D_MODEL), _whole(wb, 2), _whole(wo, 2)],
        out_specs=_row_spec(off, D_MODEL),
        out_shape=jax.ShapeDtypeStruct((bsz, t, D_MODEL), F32),
        compiler_params=_params(("parallel", "parallel")), name="merge",
    )(xall, mod, ys[0], ys[1], ys[2], sg, wb, wo)


def _block_diag_ones(head_dim):
    i = np.arange(MXU)
    return jnp.asarray((i[:, None] // head_dim) == (i[None, :] // head_dim), dtype=BF16)


def _chunk_tri(upper):
    i = np.arange(TM)
    same = (i[:, None] // GLA_CHUNK) == (i[None, :] // GLA_CHUNK)
    tri = (i[None, :] >= i[:, None]) if upper else (i[:, None] >= i[None, :])
    return jnp.asarray(same & tri, dtype=BF16)


def _rope_tables(ctx_len, seq):
    quarter = GQA_HD // 4
    freqs = ROPE_BASE ** (-np.arange(quarter, dtype=np.float64) / quarter)
    tok = np.arange(seq)
    ang_r = (tok // GRID_W)[:, None] * freqs
    ang_c = (tok % GRID_W)[:, None] * freqs
    ang = np.concatenate([ang_r, ang_r, ang_c, ang_c], axis=1)
    sign = np.tile(np.concatenate([-np.ones(quarter), np.ones(quarter)]), 2)
    cos = np.concatenate([np.ones((ctx_len, GQA_HD)), np.cos(ang)], axis=0)
    sin = np.concatenate([np.zeros((ctx_len, GQA_HD)), np.sin(ang) * sign], axis=0)
    return jnp.asarray(cos, F32), jnp.asarray(sin, F32)


def _layer_weights(l, ffn_w_in, ffn_w_out, w_in, gla_fg_w2, gla_fg_b, gla_norm_g, nat_q_norm, nat_k_norm,
                   gqa_q_norm, gqa_k_norm, w_branch, w_out):
    w = w_in[l]
    nqk = GLA_HEADS * GLA_DK
    nv = GLA_HEADS * GLA_DV
    o_fg = 2 * nqk + 2 * nv
    o_nat = o_fg + 2 * GLA_LR
    n = NAT_HEADS * NAT_HD
    o_gqa = o_nat + 3 * n
    nq = GQA_HEADS * GQA_HD
    nkv = GQA_KV_HEADS * GQA_HD
    o_gate = o_gqa + nq + 2 * nkv
    pad = jnp.zeros((D_MODEL, LANES - 2 * GLA_LR), w.dtype)
    w2 = gla_fg_w2[l]
    zeros_lr = jnp.zeros((GLA_LR, nqk), w2.dtype)
    zeros_rest = jnp.zeros((LANES - 2 * GLA_LR, nqk), w2.dtype)
    return {
        "ffn": [(ffn_w_in[l, i, :, :D_FF].astype(BF16), ffn_w_in[l, i, :, D_FF:].astype(BF16),
                 ffn_w_out[l, i].astype(BF16)) for i in range(2)],
        "w_gla": w[:, :o_fg].astype(BF16),
        "w_fg": jnp.concatenate([w[:, o_fg:o_nat], pad], axis=1).astype(BF16),
        "w_nat_qk": w[:, o_nat:o_nat + 2 * n].astype(BF16),
        "w_nat_vt": w[:, o_nat + 2 * n:o_gqa].T.astype(BF16),
        "w_gqa_qk": w[:, o_gqa:o_gqa + nq + nkv].astype(BF16),
        "w_gqa_vt": w[:, o_gqa + nq + nkv:o_gate].T.astype(BF16),
        "w_gates": w[:, o_gate:].astype(BF16),
        "w2": jnp.concatenate([jnp.concatenate([w2[0], zeros_lr, zeros_rest], axis=0),
                               jnp.concatenate([zeros_lr, w2[1], zeros_rest], axis=0)], axis=1).astype(BF16),
        "b2": gla_fg_b[l].reshape(1, 2 * nqk),
        "tril": _chunk_tri(False),
        "triu": _chunk_tri(True),
        "gla_gain": gla_norm_g[l].reshape(1, GLA_DV),
        "nat_gq": jnp.tile(nat_q_norm[l], NAT_HEADS).reshape(1, n),
        "nat_gk": jnp.tile(nat_k_norm[l], NAT_HEADS).reshape(1, n),
        "gqa_gq": jnp.tile(gqa_q_norm[l], GQA_HEADS).reshape(1, nq),
        "gqa_gk": jnp.tile(gqa_k_norm[l], GQA_KV_HEADS).reshape(1, nkv),
        "ones64": _block_diag_ones(NAT_HD),
        "ones128": _block_diag_ones(GQA_HD),
        "w_branch": w_branch[l].astype(BF16),
        "w_out": w_out[l].astype(BF16),
    }


def kernel(x, c, ctx, c_ctx, w_mod, b_mod, norm_g, ffn_w_in, ffn_w_out, w_in, gla_fg_w2, gla_fg_b, gla_norm_g,
           nat_q_norm, nat_k_norm, nat_rpb, gqa_q_norm, gqa_k_norm, w_branch, w_out):
    bsz, seq, _ = x.shape
    ctx_len = ctx.shape[1]
    depth = w_mod.shape[0]
    assert seq % TM == 0 and ctx_len == TM and seq // TM >= NAT_KBLOCKS

    mod_rows = -(-(bsz + 1) // 8) * 8
    cvec = jnp.concatenate([c, c_ctx[None, :], jnp.zeros((mod_rows - bsz - 1, D_MODEL), c.dtype)], axis=0)
    mods = _mod_table(cvec, w_mod, b_mod)[:, :bsz + 1].reshape(depth, bsz + 1, 1, N_MOD * D_MODEL)
    rope_cos, rope_sin = _rope_tables(ctx_len, seq)

    xall = x
    t = ctx_len + seq
    for l in range(depth):
        need_ctx = l < depth - 1
        off = 0 if need_ctx else 1
        lw = _layer_weights(l, ffn_w_in, ffn_w_out, w_in, gla_fg_w2, gla_fg_b, gla_norm_g, nat_q_norm,
                            nat_k_norm, gqa_q_norm, gqa_k_norm, w_branch, w_out)
        mod = mods[l]
        g = [norm_g[l, i].reshape(1, D_MODEL) for i in range(3)]
        xall = _ffn(xall, mod, g[0], *lw["ffn"][0], sub=0, off=0, out_off=0, out_tokens=t,
                    ctx=ctx if l == 0 else None)
        (gq, gk, gv, gog, ggf, ggb), (nq_, nk_, nvt), (aq, ak, avt, sg) = _projections(
            xall, mod, g[1], lw, rope_cos, rope_sin)
        y_gla = _gla(gq, gk, gv, gog, ggf, ggb, lw, ctx_len)
        y_nat = _nat(nq_, nk_, nvt, _nat_bias_tables(nat_rpb[l], seq // GRID_W), need_ctx)
        y_gqa = _gqa(aq, ak, avt, need_ctx)
        xall = _merge(xall, mod, (y_gla, y_nat, y_gqa), sg, lw["w_branch"], lw["w_out"], off)
        if need_ctx:
            xall = _ffn(xall, mod, g[2], *lw["ffn"][1], sub=2, off=0, out_off=0, out_tokens=t)
        else:
            xall = _ffn(xall, mod, g[2], *lw["ffn"][1], sub=2, off=1, out_off=0, out_tokens=seq)
    return xall
```

```python
import functools

import numpy as np
import jax
import jax.numpy as jnp
from jax import lax
from jax.experimental import pallas as pl
from jax.experimental.pallas import tpu as pltpu

F32 = jnp.float32
BF16 = jnp.bfloat16

D_MODEL = 1024
GRID_W = 64
N_MOD = 9
D_FF = 2816
RMS_EPS = 1e-6
NEG_INF = -1e30

GLA_HEADS = 4
GLA_DK = 128
GLA_DV = 256
GLA_LR = 16
GLA_TAU = 16.0
GLA_CHUNK = 128

NAT_HEADS = 16
NAT_HD = 64
WIN_R = 8
WIN_C = 16
NAT_QROWS = 4
NAT_KBLOCKS = 3

GQA_HEADS = 8
GQA_KV_HEADS = 2
GQA_HD = 128
ROPE_BASE = 10000.0
GQA_KB = 512
LOG2E = 1.4426950408889634

TM = 256
LANES = 128
MXU = 256
VMEM_LIMIT = 56 * 1024 * 1024


def _dot(a, b):
    return jnp.dot(a, b, preferred_element_type=F32)


def _dot_nt(a, b):
    return lax.dot_general(a, b, (((1,), (1,)), ((), ())), preferred_element_type=F32)


def _dot_tn(a, b):
    return lax.dot_general(a, b, (((0,), (0,)), ((), ())), preferred_element_type=F32)


def _sigmoid(x):
    return 1.0 / (1.0 + jnp.exp(-x))


def _log_sigmoid(x):
    return -(jnp.maximum(-x, 0.0) + jnp.log(1.0 + jnp.exp(-jnp.abs(x))))


def _norm_mod(x, g, shift, scale):
    y = x * lax.rsqrt(jnp.mean(x * x, axis=-1, keepdims=True) + RMS_EPS) * g
    return y * (1.0 + scale) + shift


def _head_norm(t, gain, ones_bd, head_dim):
    sq = (t * t).astype(BF16)
    parts = [_dot(sq[:, i:i + MXU], ones_bd) for i in range(0, t.shape[1], MXU)]
    ss = parts[0] if len(parts) == 1 else jnp.concatenate(parts, axis=1)
    return t * lax.rsqrt(ss * (1.0 / head_dim) + RMS_EPS) * gain


def _params(semantics):
    return pltpu.CompilerParams(dimension_semantics=semantics, vmem_limit_bytes=VMEM_LIMIT)


def _whole(arr, ngrid):
    zeros = (0,) * arr.ndim
    return pl.BlockSpec(arr.shape, lambda *_: zeros, pipeline_mode=pl.Buffered(1))


def _row_spec(off, width):
    return pl.BlockSpec((1, TM, width), lambda b, j: (b, j + off, 0))


def _mod_spec(col, off, ctx_row):
    return pl.BlockSpec((1, 1, D_MODEL), lambda b, j: (jnp.where(j + off == 0, ctx_row, b), 0, col))


def _mod_kernel(c_ref, w_ref, b_ref, o_ref):
    c = c_ref[...]
    s = (c * _sigmoid(c)).astype(BF16)
    o_ref[0] = _dot(s, w_ref[0].astype(BF16)) + b_ref[0]


def _mod_table(cvec, w_mod, b_mod):
    depth = w_mod.shape[0]
    rows = cvec.shape[0]
    return pl.pallas_call(
        _mod_kernel,
        grid=(depth, N_MOD),
        in_specs=[pl.BlockSpec((rows, D_MODEL), lambda l, n: (0, 0)),
                  pl.BlockSpec((1, D_MODEL, D_MODEL), lambda l, n: (l, 0, n)),
                  pl.BlockSpec((1, 1, D_MODEL), lambda l, n: (l, 0, n))],
        out_specs=pl.BlockSpec((1, rows, D_MODEL), lambda l, n: (l, 0, n)),
        out_shape=jax.ShapeDtypeStruct((depth, rows, N_MOD * D_MODEL), F32),
        compiler_params=_params(("arbitrary", "arbitrary")),
        name="mod_table",
    )(cvec, w_mod, b_mod.reshape(depth, 1, N_MOD * D_MODEL))


def _ffn_kernel(x_ref, ctx_ref, sh_ref, sc_ref, gt_ref, g_ref, wa_ref, wb_ref, wd_ref, o_ref, *, split_input):
    x = jnp.where(pl.program_id(1) == 0, ctx_ref[0], x_ref[0]) if split_input else x_ref[0]
    u = _norm_mod(x, g_ref[...], sh_ref[0], sc_ref[0]).astype(BF16)
    a = _dot(u, wa_ref[...])
    b = _dot(u, wb_ref[...])
    h = (a * _sigmoid(a) * b).astype(BF16)
    o_ref[0] = x + (0.5 * gt_ref[0]) * _dot(h, wd_ref[...])


def _ffn(xall, mod, g, wa, wb, wd, sub, off, out_off, out_tokens, ctx=None):
    bsz, t, _ = xall.shape
    split = ctx is not None
    if split:
        t += ctx.shape[1]
        x_spec = pl.BlockSpec((1, TM, D_MODEL), lambda b, j: (b, jnp.maximum(j - 1, 0), 0))
        ctx_spec = pl.BlockSpec((1, TM, D_MODEL), lambda b, j: (b, 0, 0))
    else:
        ctx = xall
        x_spec = _row_spec(off, D_MODEL)
        ctx_spec = pl.BlockSpec((1, 8, D_MODEL), lambda b, j: (b, 0, 0))
    ctx_row = mod.shape[0] - 1
    return pl.pallas_call(
        functools.partial(_ffn_kernel, split_input=split),
        grid=(bsz, t // TM - off),
        in_specs=[x_spec, ctx_spec,
                  _mod_spec(3 * sub, off, ctx_row), _mod_spec(3 * sub + 1, off, ctx_row),
                  _mod_spec(3 * sub + 2, off, ctx_row),
                  _whole(g, 2), _whole(wa, 2), _whole(wb, 2), _whole(wd, 2)],
        out_specs=_row_spec(out_off, D_MODEL),
        out_shape=jax.ShapeDtypeStruct((bsz, out_tokens, D_MODEL), F32),
        compiler_params=_params(("parallel", "parallel")),
        name="ffn",
    )(xall, ctx, mod, mod, mod, g, wa, wb, wd)


def _chunk_cumsum(tri, la):
    hi = la.astype(BF16)
    lo = (la - hi.astype(F32)).astype(BF16)
    return _dot(tri, hi) + _dot(tri, lo)


def _proj_gla_kernel(x_ref, sh_ref, sc_ref, g_ref, w_ref, wfg_ref, w2_ref, b2_ref, tril_ref, triu_ref,
                     q_ref, k_ref, v_ref, og_ref, gf_ref, gb_ref):
    u = _norm_mod(x_ref[0], g_ref[...], sh_ref[0], sc_ref[0]).astype(BF16)
    nqk = GLA_HEADS * GLA_DK
    nv = GLA_HEADS * GLA_DV
    fg = _dot(u, wfg_ref[...]).astype(BF16)
    q_ref[0] = (_dot(u, w_ref[:, :nqk]) * GLA_DK ** -0.5).astype(BF16)
    z = _dot(fg, w2_ref[...]) + b2_ref[...]
    k_ref[0] = _dot(u, w_ref[:, nqk:2 * nqk]).astype(BF16)
    la = _log_sigmoid(z) * (1.0 / GLA_TAU)
    v_ref[0] = _dot(u, w_ref[:, 2 * nqk:2 * nqk + nv]).astype(BF16)
    gf_ref[0] = _chunk_cumsum(tril_ref[...], la[:, :nqk])
    gb_ref[0] = _chunk_cumsum(triu_ref[...], la[:, nqk:])
    og = _dot(u, w_ref[:, 2 * nqk + nv:])
    og_ref[0] = (og * _sigmoid(og)).astype(BF16)


def _proj_nat_kernel(x_ref, sh_ref, sc_ref, g_ref, wqk_ref, wvt_ref, gq_ref, gk_ref, ones_ref,
                     q_ref, k_ref, vt_ref):
    u = _norm_mod(x_ref[0], g_ref[...], sh_ref[0], sc_ref[0]).astype(BF16)
    p = _dot(u, wqk_ref[...])
    n = NAT_HEADS * NAT_HD
    ones_bd = ones_ref[...]
    q_ref[0] = (_head_norm(p[:, :n], gq_ref[...], ones_bd, NAT_HD) * (NAT_HD ** -0.5 * LOG2E)).astype(BF16)
    k_ref[0] = _head_norm(p[:, n:], gk_ref[...], ones_bd, NAT_HD).astype(BF16)
    vt_ref[0] = _dot_nt(wvt_ref[...], u).astype(BF16)


def _rope(t, cos, sin, lane):
    partner = jnp.where(lane % (GQA_HD // 2) < GQA_HD // 4,
                        pltpu.roll(t, GQA_HD - GQA_HD // 4, axis=1), pltpu.roll(t, GQA_HD // 4, axis=1))
    return t * cos + partner * sin


def _proj_gqa_kernel(x_ref, sh_ref, sc_ref, g_ref, wqk_ref, wvt_ref, wg_ref, gq_ref, gk_ref, ones_ref,
                     cos_ref, sin_ref, q_ref, k_ref, vt_ref, sg_ref):
    u = _norm_mod(x_ref[0], g_ref[...], sh_ref[0], sc_ref[0]).astype(BF16)
    p = _dot(u, wqk_ref[...])
    nq = GQA_HEADS * GQA_HD
    ones_bd = ones_ref[...]
    qn = _head_norm(p[:, :nq], gq_ref[...], ones_bd, GQA_HD)
    kn = _head_norm(p[:, nq:], gk_ref[...], ones_bd, GQA_HD)
    cos = cos_ref[...]
    sin = sin_ref[...]
    lane = lax.broadcasted_iota(jnp.int32, (TM, GQA_HD), 1)
    for h in range(GQA_HEADS):
        sl = slice(h * GQA_HD, (h + 1) * GQA_HD)
        q_ref[0, :, sl] = (_rope(qn[:, sl], cos, sin, lane) * (GQA_HD ** -0.5 * LOG2E)).astype(BF16)
    for h in range(GQA_KV_HEADS):
        sl = slice(h * GQA_HD, (h + 1) * GQA_HD)
        k_ref[0, :, sl] = _rope(kn[:, sl], cos, sin, lane).astype(BF16)
    vt_ref[0] = _dot_nt(wvt_ref[...], u).astype(BF16)
    sg_ref[0] = _sigmoid(_dot(u, wg_ref[...])).astype(BF16)


def _col_spec(rows, off):
    return pl.BlockSpec((1, rows, TM), lambda b, j: (b, 0, j + off))


def _projections(xall, mod, g, lw, rope_cos, rope_sin):
    bsz, t, _ = xall.shape
    ctx_row = mod.shape[0] - 1
    grid = (bsz, t // TM)
    common = [_row_spec(0, D_MODEL), _mod_spec(3, 0, ctx_row), _mod_spec(4, 0, ctx_row), _whole(g, 2)]
    sds = jax.ShapeDtypeStruct
    nqk = GLA_HEADS * GLA_DK
    nv = GLA_HEADS * GLA_DV

    gla = pl.pallas_call(
        _proj_gla_kernel, grid=grid,
        in_specs=common + [_whole(lw["w_gla"], 2), _whole(lw["w_fg"], 2), _whole(lw["w2"], 2), _whole(lw["b2"], 2),
                           _whole(lw["tril"], 2), _whole(lw["triu"], 2)],
        out_specs=[_row_spec(0, nqk), _row_spec(0, nqk), _row_spec(0, nv), _row_spec(0, nv),
                   _row_spec(0, nqk), _row_spec(0, nqk)],
        out_shape=[sds((bsz, t, nqk), BF16), sds((bsz, t, nqk), BF16), sds((bsz, t, nv), BF16),
                   sds((bsz, t, nv), BF16), sds((bsz, t, nqk), F32), sds((bsz, t, nqk), F32)],
        compiler_params=_params(("parallel", "parallel")), name="proj_gla",
    )(xall, mod, mod, g, lw["w_gla"], lw["w_fg"], lw["w2"], lw["b2"], lw["tril"], lw["triu"])

    n = NAT_HEADS * NAT_HD
    nat = pl.pallas_call(
        _proj_nat_kernel, grid=grid,
        in_specs=common + [_whole(lw["w_nat_qk"], 2), _whole(lw["w_nat_vt"], 2), _whole(lw["nat_gq"], 2),
                           _whole(lw["nat_gk"], 2), _whole(lw["ones64"], 2)],
        out_specs=[_row_spec(0, n), _row_spec(0, n), _col_spec(n, 0)],
        out_shape=[sds((bsz, t, n), BF16), sds((bsz, t, n), BF16), sds((bsz, n, t), BF16)],
        compiler_params=_params(("parallel", "parallel")), name="proj_nat",
    )(xall, mod, mod, g, lw["w_nat_qk"], lw["w_nat_vt"], lw["nat_gq"], lw["nat_gk"], lw["ones64"])

    nq = GQA_HEADS * GQA_HD
    nkv = GQA_KV_HEADS * GQA_HD
    tab = pl.BlockSpec((TM, GQA_HD), lambda b, j: (j, 0))
    gqa = pl.pallas_call(
        _proj_gqa_kernel, grid=grid,
        in_specs=common + [_whole(lw["w_gqa_qk"], 2), _whole(lw["w_gqa_vt"], 2), _whole(lw["w_gates"], 2),
                           _whole(lw["gqa_gq"], 2), _whole(lw["gqa_gk"], 2), _whole(lw["ones128"], 2),
                           tab, tab],
        out_specs=[_row_spec(0, nq), _row_spec(0, nkv), _col_spec(nkv, 0), _row_spec(0, 3 * D_MODEL)],
        out_shape=[sds((bsz, t, nq), BF16), sds((bsz, t, nkv), BF16), sds((bsz, nkv, t), BF16),
                   sds((bsz, t, 3 * D_MODEL), BF16)],
        compiler_params=_params(("parallel", "parallel")), name="proj_gqa",
    )(xall, mod, mod, g, lw["w_gqa_qk"], lw["w_gqa_vt"], lw["w_gates"], lw["gqa_gq"], lw["gqa_gk"],
      lw["ones128"], rope_cos, rope_sin)
    return gla, nat, gqa


GLA_UNROLL = 2


def _gla_chain(n, forward, q_ref, k_ref, v_ref, g_ref, o_ref, state, tri):
    c = GLA_CHUNK
    rows = pl.ds(pl.multiple_of(n * c, c), c)
    g = g_ref[0, rows, :]
    if forward:
        g_mid = g[c // 2 - 1:c // 2, :]
        g_edge = g[c - 1:c, :]
    else:
        g_mid = g[c // 2:c // 2 + 1, :]
        g_edge = g[0:1, :]
    q = q_ref[0, rows, :].astype(F32)
    k = k_ref[0, rows, :].astype(F32)
    v = v_ref[0, rows, :]
    q_in = (q * jnp.exp(g - g_mid)).astype(BF16)
    k_in = (k * jnp.exp(g_mid - g)).astype(BF16)
    q_x = (q * jnp.exp(g)).astype(BF16)
    k_end = (k * jnp.exp(g_edge - g)).astype(BF16)
    decay = jnp.exp(g_edge)
    yield
    att = _dot_nt(q_in, k_in)
    s_in = state["s"]
    state["s"] = s_in * decay + _dot_tn(v, k_end)
    yield
    att = jnp.where(tri, att, 0.0).astype(BF16)
    o = _dot(att, v) + _dot_nt(q_x, s_in.astype(BF16))
    yield
    o_ref[rows, :] = o


def _gla_kernel(q_ref, k_ref, v_ref, og_ref, gf_ref, gb_ref, gain_ref, y_ref,
                of_ref, ob_ref, sf_ref, sb_ref, *, n_chunks, n_ctx_chunks):
    c = GLA_CHUNK
    ri = lax.broadcasted_iota(jnp.int32, (c, c), 0)
    ci = lax.broadcasted_iota(jnp.int32, (c, c), 1)
    lower = ri >= ci
    upper = ci >= ri
    sf_ref[...] = jnp.zeros_like(sf_ref)
    sb_ref[...] = jnp.zeros_like(sb_ref)

    def scan(it, carry):
        fwd = {"s": sf_ref[...]}
        bwd = {"s": sb_ref[...]}
        chains = []
        for u in range(GLA_UNROLL):
            s = it * GLA_UNROLL + u
            nb = jnp.where(s < n_ctx_chunks, n_ctx_chunks - 1 - s, n_chunks - 1 - (s - n_ctx_chunks))
            chains.append(_gla_chain(s, True, q_ref, k_ref, v_ref, gf_ref, of_ref, fwd, lower))
            chains.append(_gla_chain(nb, False, q_ref, k_ref, v_ref, gb_ref, ob_ref, bwd, upper))
        while chains:
            alive = []
            for ch in chains:
                try:
                    next(ch)
                    alive.append(ch)
                except StopIteration:
                    pass
            chains = alive
        sf_ref[...] = fwd["s"]
        sb_ref[...] = bwd["s"]
        return carry

    lax.fori_loop(0, n_chunks // GLA_UNROLL, scan, 0)

    def finish(n, carry):
        rows = pl.ds(pl.multiple_of(n * c, c), c)
        o = of_ref[rows, :] + ob_ref[rows, :]
        y = o * lax.rsqrt(jnp.mean(o * o, axis=-1, keepdims=True) + RMS_EPS) * gain_ref[...]
        y_ref[0, rows, :] = (y * og_ref[0, rows, :].astype(F32)).astype(BF16)
        return carry

    lax.fori_loop(0, n_chunks, finish, 0, unroll=2)


def _gla(q, k, v, og, gf, gb, lw, ctx_len):
    bsz, t, _ = q.shape
    assert (t // GLA_CHUNK) % GLA_UNROLL == 0 and ctx_len % GLA_CHUNK == 0
    seq_spec = lambda w: pl.BlockSpec((1, t, w), lambda b, h: (b, 0, h))
    kern = functools.partial(_gla_kernel, n_chunks=t // GLA_CHUNK, n_ctx_chunks=ctx_len // GLA_CHUNK)
    return pl.pallas_call(
        kern, grid=(bsz, GLA_HEADS),
        in_specs=[seq_spec(GLA_DK), seq_spec(GLA_DK), seq_spec(GLA_DV), seq_spec(GLA_DV),
                  seq_spec(GLA_DK), seq_spec(GLA_DK), pl.BlockSpec((1, GLA_DV), lambda b, h: (0, 0))],
        out_specs=seq_spec(GLA_DV),
        out_shape=jax.ShapeDtypeStruct((bsz, t, GLA_HEADS * GLA_DV), BF16),
        scratch_shapes=[pltpu.VMEM((t, GLA_DV), F32), pltpu.VMEM((t, GLA_DV), F32),
                        pltpu.VMEM((GLA_DV, GLA_DK), F32), pltpu.VMEM((GLA_DV, GLA_DK), F32)],
        compiler_params=_params(("parallel", "parallel")), name="gla",
    )(q, k, v, og, gf, gb, lw["gla_gain"])


NAT_GROUP = 8


def _nat_kernel(q_ref, k0_ref, k1_ref, k2_ref, kc_ref, v0_ref, v1_ref, v2_ref, vc_ref, bias_ref, o_ref, s_ref,
                *, off):
    is_ctx = pl.program_id(0) + off == 0
    lane = lax.broadcasted_iota(jnp.int32, (TM, LANES), 1)
    row = lax.broadcasted_iota(jnp.int32, (LANES, TM), 0)
    n_heads = 2 * NAT_GROUP
    n_lat = NAT_KBLOCKS * TM

    def run(lat_keys, lat_vals):
        kcats, vlats = {}, {}

        def scores(i):
            hp, e = divmod(i, 2)
            sl = slice(hp * LANES, (hp + 1) * LANES)
            if hp not in kcats:
                kcats[hp] = jnp.concatenate([r[0, :, sl] for r in lat_keys + [kc_ref]], axis=0)
            q = q_ref[0, :, sl]
            qm = jnp.where((lane >= NAT_HD) if e else (lane < NAT_HD), q, jnp.zeros_like(q))
            s_ref[i % 2, :kcats[hp].shape[0], :] = _dot_nt(kcats[hp], qm)

        scores(0)
        outs = []
        for i in range(n_heads):
            if i + 1 < n_heads:
                scores(i + 1)
            hp, e = divmod(i, 2)
            sl = slice(hp * LANES, (hp + 1) * LANES)
            n_l = n_lat if lat_keys else 0
            s_ctx = s_ref[i % 2, n_l:n_l + TM, :]
            m = jnp.max(s_ctx, axis=0, keepdims=True)
            if lat_keys:
                s_lat = s_ref[i % 2, :n_lat, :] + bias_ref[i, 0]
                m = jnp.maximum(m, jnp.max(s_lat, axis=0, keepdims=True))
            p_ctx = jnp.exp2(s_ctx - m)
            l = jnp.sum(p_ctx, axis=0, keepdims=True)
            o = _dot(vc_ref[0, sl, :], p_ctx.astype(BF16))
            if lat_keys:
                if hp not in vlats:
                    vlats[hp] = jnp.concatenate([r[0, sl, :] for r in lat_vals], axis=1)
                p_lat = jnp.exp2(s_lat - m)
                l = l + jnp.sum(p_lat, axis=0, keepdims=True)
                o = o + _dot(vlats[hp], p_lat.astype(BF16))
            outs.append(o * (1.0 / l))
            if e == 1:
                ot = jnp.where(row < NAT_HD, outs[-2], outs[-1])
                o_ref[0, :, sl] = ot.T.astype(BF16)

    @pl.when(is_ctx)
    def _():
        run([], [])

    @pl.when(jnp.logical_not(is_ctx))
    def _():
        run([k0_ref, k1_ref, k2_ref], [v0_ref, v1_ref, v2_ref])


def _nat(q, k, vt, bias, layer, need_ctx):
    bsz, t, n = q.shape
    off = 0 if need_ctx else 1
    nblk = t // TM - 1
    gw = NAT_GROUP * LANES
    ngroups = n // gw

    def kblock(qi, i):
        j = qi + off - 1
        return 1 + jnp.clip(j - 1, 0, nblk - NAT_KBLOCKS) + i

    def bias_class(qi):
        j = qi + off - 1
        return jnp.where(j <= 0, 0, jnp.where(j == nblk - 1, 2, 1))

    kspec = lambda i: pl.BlockSpec((1, TM, gw), lambda qi, hg, b: (b, kblock(qi, i), hg))
    vspec = lambda i: pl.BlockSpec((1, gw, TM), lambda qi, hg, b: (b, hg, kblock(qi, i)))
    return pl.pallas_call(
        functools.partial(_nat_kernel, off=off),
        grid=(t // TM - off, ngroups, bsz),
        in_specs=[pl.BlockSpec((1, TM, gw), lambda qi, hg, b: (b, qi + off, hg)),
                  kspec(0), kspec(1), kspec(2),
                  pl.BlockSpec((1, TM, gw), lambda qi, hg, b: (b, 0, hg)),
                  vspec(0), vspec(1), vspec(2),
                  pl.BlockSpec((1, gw, TM), lambda qi, hg, b: (b, hg, 0)),
                  pl.BlockSpec((2 * NAT_GROUP, 1, NAT_KBLOCKS * TM, TM),
                               lambda qi, hg, b: (layer * ngroups + hg, bias_class(qi), 0, 0))],
        out_specs=pl.BlockSpec((1, TM, gw), lambda qi, hg, b: (b, qi + off, hg)),
        out_shape=jax.ShapeDtypeStruct((bsz, t, n), BF16),
        scratch_shapes=[pltpu.VMEM((2, (NAT_KBLOCKS + 1) * TM, TM), F32)],
        compiler_params=_params(("parallel", "parallel", "parallel")), name="nat",
    )(q, k, k, k, k, vt, vt, vt, vt, bias)


def _nat_bias_tables(rpb, rows):
    nblk = rows // NAT_QROWS
    kr_win = min(WIN_R, rows)
    col = np.arange(GRID_W)
    c_start = np.clip(col - WIN_C // 2, 0, GRID_W - WIN_C)
    cmask = (col[:, None] >= c_start[None, :]) & (col[:, None] < c_start[None, :] + WIN_C)
    dc = np.clip(col[:, None] - col[None, :] + WIN_C - 1, 0, 2 * WIN_C - 2)
    drs, rmasks = [], []
    for jblk in (0, 1, nblk - 1):
        base = int(np.clip(jblk - 1, 0, nblk - NAT_KBLOCKS))
        r = NAT_QROWS * jblk + np.arange(NAT_QROWS)
        r_start = np.clip(r - kr_win // 2, 0, rows - kr_win)
        kr = NAT_QROWS * base + np.arange(NAT_KBLOCKS * NAT_QROWS)
        rmasks.append((kr[:, None] >= r_start[None, :]) & (kr[:, None] < r_start[None, :] + kr_win))
        drs.append(np.clip(kr[:, None] - r[None, :] + WIN_R - 1, 0, 2 * WIN_R - 2))
    drx = np.where(np.stack(rmasks), np.stack(drs), 2 * WIN_R - 1)
    dcx = np.where(cmask, dc, 2 * WIN_C - 1)
    nh = rpb.shape[0]
    ext = jnp.full((nh, 2 * WIN_R, 2 * WIN_C), NEG_INF, F32)
    ext = ext.at[:, :2 * WIN_R - 1, :2 * WIN_C - 1].set(rpb.astype(F32) * LOG2E)
    a = jnp.tile(ext[:, :, dcx], (1, 1, 1, NAT_QROWS))
    nk = NAT_KBLOCKS * NAT_QROWS * GRID_W
    return pl.pallas_call(
        functools.partial(_bias_kernel, drx=drx),
        grid=(nh,),
        in_specs=[pl.BlockSpec((1,) + a.shape[1:], lambda h: (h, 0, 0, 0))],
        out_specs=pl.BlockSpec((1, 3, nk, TM), lambda h: (h, 0, 0, 0)),
        out_shape=jax.ShapeDtypeStruct((nh, 3, nk, TM), F32),
        compiler_params=_params(("parallel",)), name="nat_bias",
    )(a)


def _bias_kernel(a_ref, o_ref, *, drx):
    lane_blk = lax.broadcasted_iota(jnp.int32, (GRID_W, TM), 1) // GRID_W
    n_cls, n_kri, n_ri = drx.shape
    for cls in range(n_cls):
        for kri in range(n_kri):
            strip = a_ref[0, int(drx[cls, kri, n_ri - 1])]
            for ri in range(n_ri - 1):
                strip = jnp.where(lane_blk == ri, a_ref[0, int(drx[cls, kri, ri])], strip)
            o_ref[0, cls, kri * GRID_W:(kri + 1) * GRID_W, :] = strip


def _gqa_kernel(q_ref, k_ref, vt_ref, o_ref, s_ref, acc_ref, *, off, n_lat):
    is_ctx = pl.program_id(1) + off == 0
    group = GQA_HEADS // GQA_KV_HEADS

    def run(n_blocks):
        qs = {}

        def scores(h, start, size, slot):
            if h not in qs:
                heads = range(h * group, (h + 1) * group)
                qs[h] = jnp.concatenate([q_ref[0, :, g * GQA_HD:(g + 1) * GQA_HD] for g in heads], axis=0)
            s_ref[h, slot, :size, :] = _dot_nt(k_ref[0, start:start + size, h * GQA_HD:(h + 1) * GQA_HD], qs[h])

        def start_head(h):
            scores(h, 0, TM, 0)
            if n_blocks:
                scores(h, TM, GQA_KB, 1)

        start_head(0)
        for h in range(GQA_KV_HEADS):
            rows = slice(h * GQA_HD, (h + 1) * GQA_HD)
            if n_blocks == 0 and h + 1 < GQA_KV_HEADS:
                start_head(h + 1)
            st = s_ref[h, 0, :TM, :]
            m = jnp.max(st, axis=0, keepdims=True)
            p = jnp.exp2(st - m)
            l = jnp.sum(p, axis=0, keepdims=True)
            acc_ref[h] = _dot(vt_ref[0, rows, :TM], p.astype(BF16))
            for j in range(n_blocks):
                start = TM + j * GQA_KB
                if j + 1 < n_blocks:
                    scores(h, start + GQA_KB, GQA_KB, j % 2)
                elif h + 1 < GQA_KV_HEADS:
                    start_head(h + 1)
                st = s_ref[h, (j + 1) % 2]
                m_new = jnp.maximum(m, jnp.max(st, axis=0, keepdims=True))
                alpha = jnp.exp2(m - m_new)
                p = jnp.exp2(st - m_new)
                l = alpha * l + jnp.sum(p, axis=0, keepdims=True)
                acc_ref[h] = alpha * acc_ref[h] + _dot(vt_ref[0, rows, start:start + GQA_KB], p.astype(BF16))
                m = m_new
            ot = acc_ref[h] * (1.0 / l)
            for g in range(group):
                cols = slice((h * group + g) * GQA_HD, (h * group + g + 1) * GQA_HD)
                o_ref[0, :, cols] = ot[:, g * TM:(g + 1) * TM].T.astype(BF16)

    @pl.when(is_ctx)
    def _():
        run(0)

    @pl.when(jnp.logical_not(is_ctx))
    def _():
        run(n_lat)


def _gqa(q, k, vt, need_ctx):
    bsz, t, nq = q.shape
    nkv = k.shape[2]
    off = 0 if need_ctx else 1
    group = GQA_HEADS // GQA_KV_HEADS
    assert (t - TM) % GQA_KB == 0
    return pl.pallas_call(
        functools.partial(_gqa_kernel, off=off, n_lat=(t - TM) // GQA_KB),
        grid=(bsz, t // TM - off),
        in_specs=[pl.BlockSpec((1, TM, nq), lambda b, qi: (b, qi + off, 0)),
                  pl.BlockSpec((1, t, nkv), lambda b, qi: (b, 0, 0)),
                  pl.BlockSpec((1, nkv, t), lambda b, qi: (b, 0, 0))],
        out_specs=pl.BlockSpec((1, TM, nq), lambda b, qi: (b, qi + off, 0)),
        out_shape=jax.ShapeDtypeStruct((bsz, t, nq), BF16),
        scratch_shapes=[pltpu.VMEM((GQA_KV_HEADS, 2, GQA_KB, group * TM), F32),
                        pltpu.VMEM((GQA_KV_HEADS, GQA_HD, group * TM), F32)],
        compiler_params=_params(("parallel", "arbitrary")), name="gqa",
    )(q, k, vt)


def _merge_kernel(x_ref, gt_ref, y0_ref, y1_ref, y2_ref, sg_ref, wb_ref, wo_ref, o_ref):
    z = None
    for i, y_ref in enumerate((y0_ref, y1_ref, y2_ref)):
        zi = sg_ref[0, :, i * D_MODEL:(i + 1) * D_MODEL].astype(F32) * _dot(y_ref[0], wb_ref[i])
        z = zi if z is None else z + zi
    o_ref[0] = x_ref[0] + gt_ref[0] * _dot(z.astype(BF16), wo_ref[...])


def _merge(xall, mod, ys, sg, wb, wo, off):
    bsz, t, _ = xall.shape
    ctx_row = mod.shape[0] - 1
    return pl.pallas_call(
        _merge_kernel,
        grid=(bsz, t // TM - off),
        in_specs=[_row_spec(off, D_MODEL), _mod_spec(5, off, ctx_row),
                  _row_spec(off, D_MODEL), _row_spec(off, D_MODEL), _row_spec(off, D_MODEL),
                  _row_spec(off, 3 * D_MODEL), _whole(wb, 2), _whole(wo, 2)],
        out_specs=_row_spec(off, D_MODEL),
        out_shape=jax.ShapeDtypeStruct((bsz, t, D_MODEL), F32),
        compiler_params=_params(("parallel", "parallel")), name="merge",
    )(xall, mod, ys[0], ys[1], ys[2], sg, wb, wo)


def _block_diag_ones(head_dim):
    i = np.arange(MXU)
    return jnp.asarray((i[:, None] // head_dim) == (i[None, :] // head_dim), dtype=BF16)


def _chunk_tri(upper):
    i = np.arange(TM)
    same = (i[:, None] // GLA_CHUNK) == (i[None, :] // GLA_CHUNK)
    tri = (i[None, :] >= i[:, None]) if upper else (i[:, None] >= i[None, :])
    return jnp.asarray(same & tri, dtype=BF16)


def _rope_tables(ctx_len, seq):
    quarter = GQA_HD // 4
    freqs = ROPE_BASE ** (-np.arange(quarter, dtype=np.float64) / quarter)
    tok = np.arange(seq)
    ang_r = (tok // GRID_W)[:, None] * freqs
    ang_c = (tok % GRID_W)[:, None] * freqs
    ang = np.concatenate([ang_r, ang_r, ang_c, ang_c], axis=1)
    sign = np.tile(np.concatenate([-np.ones(quarter), np.ones(quarter)]), 2)
    cos = np.concatenate([np.ones((ctx_len, GQA_HD)), np.cos(ang)], axis=0)
    sin = np.concatenate([np.zeros((ctx_len, GQA_HD)), np.sin(ang) * sign], axis=0)
    return jnp.asarray(cos, F32), jnp.asarray(sin, F32)


def _layer_weights(l, ffn_w_in, ffn_w_out, w_in, gla_fg_w2, gla_fg_b, gla_norm_g, nat_q_norm, nat_k_norm,
                   gqa_q_norm, gqa_k_norm, w_branch, w_out):
    w = w_in[l]
    nqk = GLA_HEADS * GLA_DK
    nv = GLA_HEADS * GLA_DV
    o_fg = 2 * nqk + 2 * nv
    o_nat = o_fg + 2 * GLA_LR
    n = NAT_HEADS * NAT_HD
    o_gqa = o_nat + 3 * n
    nq = GQA_HEADS * GQA_HD
    nkv = GQA_KV_HEADS * GQA_HD
    o_gate = o_gqa + nq + 2 * nkv
    pad = jnp.zeros((D_MODEL, LANES - 2 * GLA_LR), w.dtype)
    w2 = gla_fg_w2[l]
    zeros_lr = jnp.zeros((GLA_LR, nqk), w2.dtype)
    zeros_rest = jnp.zeros((LANES - 2 * GLA_LR, nqk), w2.dtype)
    return {
        "ffn": [(ffn_w_in[l, i, :, :D_FF].astype(BF16), ffn_w_in[l, i, :, D_FF:].astype(BF16),
                 ffn_w_out[l, i].astype(BF16)) for i in range(2)],
        "w_gla": w[:, :o_fg].astype(BF16),
        "w_fg": jnp.concatenate([w[:, o_fg:o_nat], pad], axis=1).astype(BF16),
        "w_nat_qk": w[:, o_nat:o_nat + 2 * n].astype(BF16),
        "w_nat_vt": w[:, o_nat + 2 * n:o_gqa].T.astype(BF16),
        "w_gqa_qk": w[:, o_gqa:o_gqa + nq + nkv].astype(BF16),
        "w_gqa_vt": w[:, o_gqa + nq + nkv:o_gate].T.astype(BF16),
        "w_gates": w[:, o_gate:].astype(BF16),
        "w2": jnp.concatenate([jnp.concatenate([w2[0], zeros_lr, zeros_rest], axis=0),
                               jnp.concatenate([zeros_lr, w2[1], zeros_rest], axis=0)], axis=1).astype(BF16),
        "b2": gla_fg_b[l].reshape(1, 2 * nqk),
        "tril": _chunk_tri(False),
        "triu": _chunk_tri(True),
        "gla_gain": gla_norm_g[l].reshape(1, GLA_DV),
        "nat_gq": jnp.tile(nat_q_norm[l], NAT_HEADS).reshape(1, n),
        "nat_gk": jnp.tile(nat_k_norm[l], NAT_HEADS).reshape(1, n),
        "gqa_gq": jnp.tile(gqa_q_norm[l], GQA_HEADS).reshape(1, nq),
        "gqa_gk": jnp.tile(gqa_k_norm[l], GQA_KV_HEADS).reshape(1, nkv),
        "ones64": _block_diag_ones(NAT_HD),
        "ones128": _block_diag_ones(GQA_HD),
        "w_branch": w_branch[l].astype(BF16),
        "w_out": w_out[l].astype(BF16),
    }


def kernel(x, c, ctx, c_ctx, w_mod, b_mod, norm_g, ffn_w_in, ffn_w_out, w_in, gla_fg_w2, gla_fg_b, gla_norm_g,
           nat_q_norm, nat_k_norm, nat_rpb, gqa_q_norm, gqa_k_norm, w_branch, w_out):
    bsz, seq, _ = x.shape
    ctx_len = ctx.shape[1]
    depth = w_mod.shape[0]
    assert seq % TM == 0 and ctx_len == TM and seq // TM >= NAT_KBLOCKS

    mod_rows = -(-(bsz + 1) // 8) * 8
    cvec = jnp.concatenate([c, c_ctx[None, :], jnp.zeros((mod_rows - bsz - 1, D_MODEL), c.dtype)], axis=0)
    mods = _mod_table(cvec, w_mod, b_mod)[:, :bsz + 1].reshape(depth, bsz + 1, 1, N_MOD * D_MODEL)
    rope_cos, rope_sin = _rope_tables(ctx_len, seq)
    nat_bias = _nat_bias_tables(nat_rpb.reshape((-1,) + nat_rpb.shape[2:]), seq // GRID_W)

    xall = x
    t = ctx_len + seq
    for l in range(depth):
        need_ctx = l < depth - 1
        off = 0 if need_ctx else 1
        lw = _layer_weights(l, ffn_w_in, ffn_w_out, w_in, gla_fg_w2, gla_fg_b, gla_norm_g, nat_q_norm,
                            nat_k_norm, gqa_q_norm, gqa_k_norm, w_branch, w_out)
        mod = mods[l]
        g = [norm_g[l, i].reshape(1, D_MODEL) for i in range(3)]
        xall = _ffn(xall, mod, g[0], *lw["ffn"][0], sub=0, off=0, out_off=0, out_tokens=t,
                    ctx=ctx if l == 0 else None)
        (gq, gk, gv, gog, ggf, ggb), (nq_, nk_, nvt), (aq, ak, avt, sg) = _projections(
            xall, mod, g[1], lw, rope_cos, rope_sin)
        y_gla = _gla(gq, gk, gv, gog, ggf, ggb, lw, ctx_len)
        y_nat = _nat(nq_, nk_, nvt, nat_bias, l, need_ctx)
        y_gqa = _gqa(aq, ak, avt, need_ctx)
        xall = _merge(xall, mod, (y_gla, y_nat, y_gqa), sg, lw["w_branch"], lw["w_out"], off)
        if need_ctx:
            xall = _ffn(xall, mod, g[2], *lw["ffn"][1], sub=2, off=0, out_off=0, out_tokens=t)
        else:
            xall = _ffn(xall, mod, g[2], *lw["ffn"][1], sub=2, off=1, out_off=0, out_tokens=seq)
    return xall
```

```python
import functools

import numpy as np
import jax
import jax.numpy as jnp
from jax import lax
from jax.experimental import pallas as pl
from jax.experimental.pallas import tpu as pltpu

F32 = jnp.float32
BF16 = jnp.bfloat16

D_MODEL = 1024
GRID_W = 64
N_MOD = 9
D_FF = 2816
RMS_EPS = 1e-6
NEG_INF = -1e30

GLA_HEADS = 4
GLA_DK = 128
GLA_DV = 256
GLA_LR = 16
GLA_TAU = 16.0
GLA_CHUNK = 128

NAT_HEADS = 16
NAT_HD = 64
WIN_R = 8
WIN_C = 16
NAT_QROWS = 4
NAT_KBLOCKS = 3

GQA_HEADS = 8
GQA_KV_HEADS = 2
GQA_HD = 128
ROPE_BASE = 10000.0
GQA_KB = 512
LOG2E = 1.4426950408889634

TM = 256
LANES = 128
MXU = 256
VMEM_LIMIT = 56 * 1024 * 1024


def _dot(a, b):
    return jnp.dot(a, b, preferred_element_type=F32)


def _dot_nt(a, b):
    return lax.dot_general(a, b, (((1,), (1,)), ((), ())), preferred_element_type=F32)


def _dot_tn(a, b):
    return lax.dot_general(a, b, (((0,), (0,)), ((), ())), preferred_element_type=F32)


def _sigmoid(x):
    return 1.0 / (1.0 + jnp.exp(-x))


def _log_sigmoid(x):
    return -(jnp.maximum(-x, 0.0) + jnp.log(1.0 + jnp.exp(-jnp.abs(x))))


def _norm_mod(x, g, shift, scale):
    y = x * lax.rsqrt(jnp.mean(x * x, axis=-1, keepdims=True) + RMS_EPS) * g
    return y * (1.0 + scale) + shift


def _head_norm(t, gain, ones_bd, head_dim):
    sq = (t * t).astype(BF16)
    parts = [_dot(sq[:, i:i + MXU], ones_bd) for i in range(0, t.shape[1], MXU)]
    ss = parts[0] if len(parts) == 1 else jnp.concatenate(parts, axis=1)
    return t * lax.rsqrt(ss * (1.0 / head_dim) + RMS_EPS) * gain


def _params(semantics):
    return pltpu.CompilerParams(dimension_semantics=semantics, vmem_limit_bytes=VMEM_LIMIT)


def _whole(arr, ngrid):
    zeros = (0,) * arr.ndim
    return pl.BlockSpec(arr.shape, lambda *_: zeros, pipeline_mode=pl.Buffered(1))


def _row_spec(off, width):
    return pl.BlockSpec((1, TM, width), lambda b, j: (b, j + off, 0))


def _mod_spec(col, off, ctx_row):
    return pl.BlockSpec((1, 1, D_MODEL), lambda b, j: (jnp.where(j + off == 0, ctx_row, b), 0, col))


def _mod_kernel(c_ref, w_ref, b_ref, o_ref):
    c = c_ref[...]
    s = (c * _sigmoid(c)).astype(BF16)
    o_ref[0] = _dot(s, w_ref[0].astype(BF16)) + b_ref[0]


def _mod_table(cvec, w_mod, b_mod):
    depth = w_mod.shape[0]
    rows = cvec.shape[0]
    return pl.pallas_call(
        _mod_kernel,
        grid=(depth, N_MOD),
        in_specs=[pl.BlockSpec((rows, D_MODEL), lambda l, n: (0, 0)),
                  pl.BlockSpec((1, D_MODEL, D_MODEL), lambda l, n: (l, 0, n)),
                  pl.BlockSpec((1, 1, D_MODEL), lambda l, n: (l, 0, n))],
        out_specs=pl.BlockSpec((1, rows, D_MODEL), lambda l, n: (l, 0, n)),
        out_shape=jax.ShapeDtypeStruct((depth, rows, N_MOD * D_MODEL), F32),
        compiler_params=_params(("arbitrary", "arbitrary")),
        name="mod_table",
    )(cvec, w_mod, b_mod.reshape(depth, 1, N_MOD * D_MODEL))


def _next_tile(b, j, nb, nj):
    wrap = j + 1 == nj
    last = jnp.logical_and(wrap, b + 1 == nb)
    return jnp.where(jnp.logical_and(wrap, jnp.logical_not(last)), b + 1, b), \
        jnp.where(wrap, jnp.where(last, j, 0), j + 1)


def _ffn_kernel(x_ref, ctx_ref, xn_ref, ctxn_ref, sh_ref, sc_ref, gt_ref, shn_ref, scn_ref, g_ref,
                wa_ref, wb_ref, wd_ref, o_ref, h_ref, *, split_input):
    b, j = pl.program_id(0), pl.program_id(1)

    def tile(x_r, ctx_r, jj):
        return jnp.where(jj == 0, ctx_r[0], x_r[0]) if split_input else x_r[0]

    def hidden(xt, shift, scale):
        u = _norm_mod(xt, g_ref[...], shift, scale).astype(BF16)
        a = _dot(u, wa_ref[...])
        return (a * _sigmoid(a) * _dot(u, wb_ref[...])).astype(BF16)

    @pl.when(jnp.logical_and(b == 0, j == 0))
    def _():
        h_ref[...] = hidden(tile(x_ref, ctx_ref, j), sh_ref[0], sc_ref[0])

    o_ref[0] = tile(x_ref, ctx_ref, j) + (0.5 * gt_ref[0]) * _dot(h_ref[...], wd_ref[...])
    _, jn = _next_tile(b, j, pl.num_programs(0), pl.num_programs(1))
    h_ref[...] = hidden(tile(xn_ref, ctxn_ref, jn), shn_ref[0], scn_ref[0])


def _ffn(xall, mod, g, wa, wb, wd, sub, off, out_off, out_tokens, ctx=None):
    bsz, t, _ = xall.shape
    split = ctx is not None
    if split:
        t += ctx.shape[1]
    nj = t // TM - off
    ctx_row = mod.shape[0] - 1
    cur = lambda b, j: (b, j)
    nxt = lambda b, j: _next_tile(b, j, bsz, nj)

    def specs(at):
        if split:
            x_spec = pl.BlockSpec((1, TM, D_MODEL), lambda b, j: (at(b, j)[0], jnp.maximum(at(b, j)[1] - 1, 0), 0))
            ctx_spec = pl.BlockSpec((1, TM, D_MODEL), lambda b, j: (at(b, j)[0], 0, 0))
        else:
            x_spec = pl.BlockSpec((1, TM, D_MODEL), lambda b, j: (at(b, j)[0], at(b, j)[1] + off, 0))
            ctx_spec = pl.BlockSpec((1, 8, D_MODEL), lambda b, j: (at(b, j)[0], 0, 0))
        return [x_spec, ctx_spec]

    def mod_spec(col, at):
        return pl.BlockSpec((1, 1, D_MODEL),
                            lambda b, j: (jnp.where(at(b, j)[1] + off == 0, ctx_row, at(b, j)[0]), 0, col))

    if not split:
        ctx = xall
    return pl.pallas_call(
        functools.partial(_ffn_kernel, split_input=split),
        grid=(bsz, nj),
        in_specs=specs(cur) + specs(nxt) +
                 [mod_spec(3 * sub, cur), mod_spec(3 * sub + 1, cur), mod_spec(3 * sub + 2, cur),
                  mod_spec(3 * sub, nxt), mod_spec(3 * sub + 1, nxt),
                  _whole(g, 2), _whole(wa, 2), _whole(wb, 2), _whole(wd, 2)],
        out_specs=_row_spec(out_off, D_MODEL),
        out_shape=jax.ShapeDtypeStruct((bsz, out_tokens, D_MODEL), F32),
        scratch_shapes=[pltpu.VMEM((TM, wd.shape[0]), BF16)],
        compiler_params=_params(("arbitrary", "arbitrary")),
        name="ffn",
    )(xall, ctx, xall, ctx, mod, mod, mod, mod, mod, g, wa, wb, wd)


def _chunk_cumsum(tri, la):
    hi = la.astype(BF16)
    lo = (la - hi.astype(F32)).astype(BF16)
    return _dot(tri, hi) + _dot(tri, lo)


def _proj_gla_kernel(x_ref, sh_ref, sc_ref, g_ref, w_ref, wfg_ref, w2_ref, b2_ref, tril_ref, triu_ref,
                     q_ref, k_ref, v_ref, og_ref, gf_ref, gb_ref):
    u = _norm_mod(x_ref[0], g_ref[...], sh_ref[0], sc_ref[0]).astype(BF16)
    nqk = GLA_HEADS * GLA_DK
    nv = GLA_HEADS * GLA_DV
    fg = _dot(u, wfg_ref[...]).astype(BF16)
    q_ref[0] = (_dot(u, w_ref[:, :nqk]) * GLA_DK ** -0.5).astype(BF16)
    z = _dot(fg, w2_ref[...]) + b2_ref[...]
    k_ref[0] = _dot(u, w_ref[:, nqk:2 * nqk]).astype(BF16)
    la = _log_sigmoid(z) * (1.0 / GLA_TAU)
    v_ref[0] = _dot(u, w_ref[:, 2 * nqk:2 * nqk + nv]).astype(BF16)
    gf_ref[0] = _chunk_cumsum(tril_ref[...], la[:, :nqk])
    gb_ref[0] = _chunk_cumsum(triu_ref[...], la[:, nqk:])
    og = _dot(u, w_ref[:, 2 * nqk + nv:])
    og_ref[0] = (og * _sigmoid(og)).astype(BF16)


def _proj_nat_kernel(x_ref, sh_ref, sc_ref, g_ref, wqk_ref, wvt_ref, gq_ref, gk_ref, ones_ref,
                     q_ref, k_ref, vt_ref):
    u = _norm_mod(x_ref[0], g_ref[...], sh_ref[0], sc_ref[0]).astype(BF16)
    p = _dot(u, wqk_ref[...])
    n = NAT_HEADS * NAT_HD
    ones_bd = ones_ref[...]
    q_ref[0] = (_head_norm(p[:, :n], gq_ref[...], ones_bd, NAT_HD) * (NAT_HD ** -0.5 * LOG2E)).astype(BF16)
    k_ref[0] = _head_norm(p[:, n:], gk_ref[...], ones_bd, NAT_HD).astype(BF16)
    vt_ref[0] = _dot_nt(wvt_ref[...], u).astype(BF16)


def _rope(t, cos, sin, lane):
    partner = jnp.where(lane % (GQA_HD // 2) < GQA_HD // 4,
                        pltpu.roll(t, GQA_HD - GQA_HD // 4, axis=1), pltpu.roll(t, GQA_HD // 4, axis=1))
    return t * cos + partner * sin


def _proj_gqa_kernel(x_ref, sh_ref, sc_ref, g_ref, wqk_ref, wvt_ref, wg_ref, gq_ref, gk_ref, ones_ref,
                     cos_ref, sin_ref, q_ref, k_ref, vt_ref, sg_ref):
    u = _norm_mod(x_ref[0], g_ref[...], sh_ref[0], sc_ref[0]).astype(BF16)
    p = _dot(u, wqk_ref[...])
    nq = GQA_HEADS * GQA_HD
    ones_bd = ones_ref[...]
    qn = _head_norm(p[:, :nq], gq_ref[...], ones_bd, GQA_HD)
    kn = _head_norm(p[:, nq:], gk_ref[...], ones_bd, GQA_HD)
    cos = cos_ref[...]
    sin = sin_ref[...]
    lane = lax.broadcasted_iota(jnp.int32, (TM, GQA_HD), 1)
    for h in range(GQA_HEADS):
        sl = slice(h * GQA_HD, (h + 1) * GQA_HD)
        q_ref[0, :, sl] = (_rope(qn[:, sl], cos, sin, lane) * (GQA_HD ** -0.5 * LOG2E)).astype(BF16)
    for h in range(GQA_KV_HEADS):
        sl = slice(h * GQA_HD, (h + 1) * GQA_HD)
        k_ref[0, :, sl] = _rope(kn[:, sl], cos, sin, lane).astype(BF16)
    vt_ref[0] = _dot_nt(wvt_ref[...], u).astype(BF16)
    sg_ref[0] = _sigmoid(_dot(u, wg_ref[...])).astype(BF16)


def _col_spec(rows, off):
    return pl.BlockSpec((1, rows, TM), lambda b, j: (b, 0, j + off))


def _projections(xall, mod, g, lw, rope_cos, rope_sin):
    bsz, t, _ = xall.shape
    ctx_row = mod.shape[0] - 1
    grid = (bsz, t // TM)
    common = [_row_spec(0, D_MODEL), _mod_spec(3, 0, ctx_row), _mod_spec(4, 0, ctx_row), _whole(g, 2)]
    sds = jax.ShapeDtypeStruct
    nqk = GLA_HEADS * GLA_DK
    nv = GLA_HEADS * GLA_DV

    gla = pl.pallas_call(
        _proj_gla_kernel, grid=grid,
        in_specs=common + [_whole(lw["w_gla"], 2), _whole(lw["w_fg"], 2), _whole(lw["w2"], 2), _whole(lw["b2"], 2),
                           _whole(lw["tril"], 2), _whole(lw["triu"], 2)],
        out_specs=[_row_spec(0, nqk), _row_spec(0, nqk), _row_spec(0, nv), _row_spec(0, nv),
                   _row_spec(0, nqk), _row_spec(0, nqk)],
        out_shape=[sds((bsz, t, nqk), BF16), sds((bsz, t, nqk), BF16), sds((bsz, t, nv), BF16),
                   sds((bsz, t, nv), BF16), sds((bsz, t, nqk), F32), sds((bsz, t, nqk), F32)],
        compiler_params=_params(("parallel", "parallel")), name="proj_gla",
    )(xall, mod, mod, g, lw["w_gla"], lw["w_fg"], lw["w2"], lw["b2"], lw["tril"], lw["triu"])

    n = NAT_HEADS * NAT_HD
    nat = pl.pallas_call(
        _proj_nat_kernel, grid=grid,
        in_specs=common + [_whole(lw["w_nat_qk"], 2), _whole(lw["w_nat_vt"], 2), _whole(lw["nat_gq"], 2),
                           _whole(lw["nat_gk"], 2), _whole(lw["ones64"], 2)],
        out_specs=[_row_spec(0, n), _row_spec(0, n), _col_spec(n, 0)],
        out_shape=[sds((bsz, t, n), BF16), sds((bsz, t, n), BF16), sds((bsz, n, t), BF16)],
        compiler_params=_params(("parallel", "parallel")), name="proj_nat",
    )(xall, mod, mod, g, lw["w_nat_qk"], lw["w_nat_vt"], lw["nat_gq"], lw["nat_gk"], lw["ones64"])

    nq = GQA_HEADS * GQA_HD
    nkv = GQA_KV_HEADS * GQA_HD
    tab = pl.BlockSpec((TM, GQA_HD), lambda b, j: (j, 0))
    gqa = pl.pallas_call(
        _proj_gqa_kernel, grid=grid,
        in_specs=common + [_whole(lw["w_gqa_qk"], 2), _whole(lw["w_gqa_vt"], 2), _whole(lw["w_gates"], 2),
                           _whole(lw["gqa_gq"], 2), _whole(lw["gqa_gk"], 2), _whole(lw["ones128"], 2),
                           tab, tab],
        out_specs=[_row_spec(0, nq), _row_spec(0, nkv), _col_spec(nkv, 0), _row_spec(0, 3 * D_MODEL)],
        out_shape=[sds((bsz, t, nq), BF16), sds((bsz, t, nkv), BF16), sds((bsz, nkv, t), BF16),
                   sds((bsz, t, 3 * D_MODEL), BF16)],
        compiler_params=_params(("parallel", "parallel")), name="proj_gqa",
    )(xall, mod, mod, g, lw["w_gqa_qk"], lw["w_gqa_vt"], lw["w_gates"], lw["gqa_gq"], lw["gqa_gk"],
      lw["ones128"], rope_cos, rope_sin)
    return gla, nat, gqa


GLA_UNROLL = 2


def _gla_chain(n, forward, q_ref, k_ref, v_ref, g_ref, o_ref, state, tri):
    c = GLA_CHUNK
    rows = pl.ds(pl.multiple_of(n * c, c), c)
    g = g_ref[0, rows, :]
    if forward:
        g_mid = g[c // 2 - 1:c // 2, :]
        g_edge = g[c - 1:c, :]
    else:
        g_mid = g[c // 2:c // 2 + 1, :]
        g_edge = g[0:1, :]
    q = q_ref[0, rows, :].astype(F32)
    k = k_ref[0, rows, :].astype(F32)
    v = v_ref[0, rows, :]
    q_in = (q * jnp.exp(g - g_mid)).astype(BF16)
    k_in = (k * jnp.exp(g_mid - g)).astype(BF16)
    q_x = (q * jnp.exp(g)).astype(BF16)
    k_end = (k * jnp.exp(g_edge - g)).astype(BF16)
    decay = jnp.exp(g_edge)
    yield
    att = _dot_nt(q_in, k_in)
    s_in = state["s"]
    state["s"] = s_in * decay + _dot_tn(v, k_end)
    yield
    att = jnp.where(tri, att, 0.0).astype(BF16)
    o = _dot(att, v) + _dot_nt(q_x, s_in.astype(BF16))
    yield
    o_ref[rows, :] = o


def _gla_kernel(q_ref, k_ref, v_ref, og_ref, gf_ref, gb_ref, gain_ref, y_ref,
                of_ref, ob_ref, sf_ref, sb_ref, *, n_chunks, n_ctx_chunks):
    c = GLA_CHUNK
    ri = lax.broadcasted_iota(jnp.int32, (c, c), 0)
    ci = lax.broadcasted_iota(jnp.int32, (c, c), 1)
    lower = ri >= ci
    upper = ci >= ri
    sf_ref[...] = jnp.zeros_like(sf_ref)
    sb_ref[...] = jnp.zeros_like(sb_ref)

    def scan(it, carry):
        fwd = {"s": sf_ref[...]}
        bwd = {"s": sb_ref[...]}
        chains = []
        for u in range(GLA_UNROLL):
            s = it * GLA_UNROLL + u
            nb = jnp.where(s < n_ctx_chunks, n_ctx_chunks - 1 - s, n_chunks - 1 - (s - n_ctx_chunks))
            chains.append(_gla_chain(s, True, q_ref, k_ref, v_ref, gf_ref, of_ref, fwd, lower))
            chains.append(_gla_chain(nb, False, q_ref, k_ref, v_ref, gb_ref, ob_ref, bwd, upper))
        while chains:
            alive = []
            for ch in chains:
                try:
                    next(ch)
                    alive.append(ch)
                except StopIteration:
                    pass
            chains = alive
        sf_ref[...] = fwd["s"]
        sb_ref[...] = bwd["s"]
        return carry

    lax.fori_loop(0, n_chunks // GLA_UNROLL, scan, 0)

    def finish(n, carry):
        rows = pl.ds(pl.multiple_of(n * c, c), c)
        o = of_ref[rows, :] + ob_ref[rows, :]
        y = o * lax.rsqrt(jnp.mean(o * o, axis=-1, keepdims=True) + RMS_EPS) * gain_ref[...]
        y_ref[0, rows, :] = (y * og_ref[0, rows, :].astype(F32)).astype(BF16)
        return carry

    lax.fori_loop(0, n_chunks, finish, 0, unroll=2)


def _gla(q, k, v, og, gf, gb, lw, ctx_len):
    bsz, t, _ = q.shape
    assert (t // GLA_CHUNK) % GLA_UNROLL == 0 and ctx_len % GLA_CHUNK == 0
    seq_spec = lambda w: pl.BlockSpec((1, t, w), lambda b, h: (b, 0, h))
    kern = functools.partial(_gla_kernel, n_chunks=t // GLA_CHUNK, n_ctx_chunks=ctx_len // GLA_CHUNK)
    return pl.pallas_call(
        kern, grid=(bsz, GLA_HEADS),
        in_specs=[seq_spec(GLA_DK), seq_spec(GLA_DK), seq_spec(GLA_DV), seq_spec(GLA_DV),
                  seq_spec(GLA_DK), seq_spec(GLA_DK), pl.BlockSpec((1, GLA_DV), lambda b, h: (0, 0))],
        out_specs=seq_spec(GLA_DV),
        out_shape=jax.ShapeDtypeStruct((bsz, t, GLA_HEADS * GLA_DV), BF16),
        scratch_shapes=[pltpu.VMEM((t, GLA_DV), F32), pltpu.VMEM((t, GLA_DV), F32),
                        pltpu.VMEM((GLA_DV, GLA_DK), F32), pltpu.VMEM((GLA_DV, GLA_DK), F32)],
        compiler_params=_params(("parallel", "parallel")), name="gla",
    )(q, k, v, og, gf, gb, lw["gla_gain"])


NAT_GROUP = 8


def _nat_kernel(q_ref, k0_ref, k1_ref, k2_ref, kc_ref, v0_ref, v1_ref, v2_ref, vc_ref, bias_ref, o_ref, s_ref,
                *, off):
    is_ctx = pl.program_id(0) + off == 0
    lane = lax.broadcasted_iota(jnp.int32, (TM, LANES), 1)
    row = lax.broadcasted_iota(jnp.int32, (LANES, TM), 0)
    n_heads = 2 * NAT_GROUP
    n_lat = NAT_KBLOCKS * TM

    def run(lat_keys, lat_vals):
        kcats, vlats = {}, {}

        def scores(i):
            hp, e = divmod(i, 2)
            sl = slice(hp * LANES, (hp + 1) * LANES)
            if hp not in kcats:
                refs = lat_keys + [kc_ref]
                half = (len(refs) + 1) // 2
                kcats[hp] = [jnp.concatenate([r[0, :, sl] for r in part], axis=0)
                             for part in (refs[:half], refs[half:]) if part]
            q = q_ref[0, :, sl]
            qm = jnp.where((lane >= NAT_HD) if e else (lane < NAT_HD), q, jnp.zeros_like(q))
            at = 0
            for kpart in kcats[hp]:
                s_ref[i % 2, at:at + kpart.shape[0], :] = _dot_nt(kpart, qm)
                at += kpart.shape[0]

        scores(0)
        outs = []
        for i in range(n_heads):
            if i + 1 < n_heads:
                scores(i + 1)
            hp, e = divmod(i, 2)
            sl = slice(hp * LANES, (hp + 1) * LANES)
            n_l = n_lat if lat_keys else 0
            s_ctx = s_ref[i % 2, n_l:n_l + TM, :]
            m = jnp.max(s_ctx, axis=0, keepdims=True)
            if lat_keys:
                s_lat = s_ref[i % 2, :n_lat, :] + bias_ref[i, 0]
                m = jnp.maximum(m, jnp.max(s_lat, axis=0, keepdims=True))
            p_ctx = jnp.exp2(s_ctx - m)
            l = jnp.sum(p_ctx, axis=0, keepdims=True)
            o = _dot(vc_ref[0, sl, :], p_ctx.astype(BF16))
            if lat_keys:
                if hp not in vlats:
                    vlats[hp] = jnp.concatenate([r[0, sl, :] for r in lat_vals], axis=1)
                p_lat = jnp.exp2(s_lat - m)
                l = l + jnp.sum(p_lat, axis=0, keepdims=True)
                o = o + _dot(vlats[hp], p_lat.astype(BF16))
            outs.append(o * (1.0 / l))
            if e == 1:
                ot = jnp.where(row < NAT_HD, outs[-2], outs[-1])
                o_ref[0, :, sl] = ot.T.astype(BF16)

    @pl.when(is_ctx)
    def _():
        run([], [])

    @pl.when(jnp.logical_not(is_ctx))
    def _():
        run([k0_ref, k1_ref, k2_ref], [v0_ref, v1_ref, v2_ref])


def _nat(q, k, vt, bias, layer, need_ctx):
    bsz, t, n = q.shape
    off = 0 if need_ctx else 1
    nblk = t // TM - 1
    gw = NAT_GROUP * LANES
    ngroups = n // gw

    def kblock(qi, i):
        j = qi + off - 1
        return 1 + jnp.clip(j - 1, 0, nblk - NAT_KBLOCKS) + i

    def bias_class(qi):
        j = qi + off - 1
        return jnp.where(j <= 0, 0, jnp.where(j == nblk - 1, 2, 1))

    kspec = lambda i: pl.BlockSpec((1, TM, gw), lambda qi, hg, b: (b, kblock(qi, i), hg))
    vspec = lambda i: pl.BlockSpec((1, gw, TM), lambda qi, hg, b: (b, hg, kblock(qi, i)))
    return pl.pallas_call(
        functools.partial(_nat_kernel, off=off),
        grid=(t // TM - off, ngroups, bsz),
        in_specs=[pl.BlockSpec((1, TM, gw), lambda qi, hg, b: (b, qi + off, hg)),
                  kspec(0), kspec(1), kspec(2),
                  pl.BlockSpec((1, TM, gw), lambda qi, hg, b: (b, 0, hg)),
                  vspec(0), vspec(1), vspec(2),
                  pl.BlockSpec((1, gw, TM), lambda qi, hg, b: (b, hg, 0)),
                  pl.BlockSpec((2 * NAT_GROUP, 1, NAT_KBLOCKS * TM, TM),
                               lambda qi, hg, b: (layer * ngroups + hg, bias_class(qi), 0, 0))],
        out_specs=pl.BlockSpec((1, TM, gw), lambda qi, hg, b: (b, qi + off, hg)),
        out_shape=jax.ShapeDtypeStruct((bsz, t, n), BF16),
        scratch_shapes=[pltpu.VMEM((2, (NAT_KBLOCKS + 1) * TM, TM), F32)],
        compiler_params=_params(("parallel", "parallel", "parallel")), name="nat",
    )(q, k, k, k, k, vt, vt, vt, vt, bias)


def _nat_bias_tables(rpb, rows):
    nblk = rows // NAT_QROWS
    kr_win = min(WIN_R, rows)
    col = np.arange(GRID_W)
    c_start = np.clip(col - WIN_C // 2, 0, GRID_W - WIN_C)
    cmask = (col[:, None] >= c_start[None, :]) & (col[:, None] < c_start[None, :] + WIN_C)
    dc = np.clip(col[:, None] - col[None, :] + WIN_C - 1, 0, 2 * WIN_C - 2)
    drs, rmasks = [], []
    for jblk in (0, 1, nblk - 1):
        base = int(np.clip(jblk - 1, 0, nblk - NAT_KBLOCKS))
        r = NAT_QROWS * jblk + np.arange(NAT_QROWS)
        r_start = np.clip(r - kr_win // 2, 0, rows - kr_win)
        kr = NAT_QROWS * base + np.arange(NAT_KBLOCKS * NAT_QROWS)
        rmasks.append((kr[:, None] >= r_start[None, :]) & (kr[:, None] < r_start[None, :] + kr_win))
        drs.append(np.clip(kr[:, None] - r[None, :] + WIN_R - 1, 0, 2 * WIN_R - 2))
    drx = np.where(np.stack(rmasks), np.stack(drs), 2 * WIN_R - 1)
    dcx = np.where(cmask, dc, 2 * WIN_C - 1)
    nh = rpb.shape[0]
    ext = jnp.full((nh, 2 * WIN_R, 2 * WIN_C), NEG_INF, F32)
    ext = ext.at[:, :2 * WIN_R - 1, :2 * WIN_C - 1].set(rpb.astype(F32) * LOG2E)
    a = jnp.tile(ext[:, :, dcx], (1, 1, 1, NAT_QROWS))
    nk = NAT_KBLOCKS * NAT_QROWS * GRID_W
    return pl.pallas_call(
        functools.partial(_bias_kernel, drx=drx),
        grid=(nh,),
        in_specs=[pl.BlockSpec((1,) + a.shape[1:], lambda h: (h, 0, 0, 0))],
        out_specs=pl.BlockSpec((1, 3, nk, TM), lambda h: (h, 0, 0, 0)),
        out_shape=jax.ShapeDtypeStruct((nh, 3, nk, TM), F32),
        compiler_params=_params(("parallel",)), name="nat_bias",
    )(a)


def _bias_kernel(a_ref, o_ref, *, drx):
    lane_blk = lax.broadcasted_iota(jnp.int32, (GRID_W, TM), 1) // GRID_W
    n_cls, n_kri, n_ri = drx.shape
    for cls in range(n_cls):
        for kri in range(n_kri):
            strip = a_ref[0, int(drx[cls, kri, n_ri - 1])]
            for ri in range(n_ri - 1):
                strip = jnp.where(lane_blk == ri, a_ref[0, int(drx[cls, kri, ri])], strip)
            o_ref[0, cls, kri * GRID_W:(kri + 1) * GRID_W, :] = strip


def _gqa_kernel(q_ref, k_ref, vt_ref, o_ref, s_ref, acc_ref, *, off, n_lat):
    is_ctx = pl.program_id(1) + off == 0
    group = GQA_HEADS // GQA_KV_HEADS

    def run(n_blocks):
        qs = {}

        def scores(h, start, size, slot):
            if h not in qs:
                heads = range(h * group, (h + 1) * group)
                qs[h] = jnp.concatenate([q_ref[0, :, g * GQA_HD:(g + 1) * GQA_HD] for g in heads], axis=0)
            s_ref[h, slot, :size, :] = _dot_nt(k_ref[0, start:start + size, h * GQA_HD:(h + 1) * GQA_HD], qs[h])

        def start_head(h):
            scores(h, 0, TM, 0)
            if n_blocks:
                scores(h, TM, GQA_KB, 1)

        start_head(0)
        for h in range(GQA_KV_HEADS):
            rows = slice(h * GQA_HD, (h + 1) * GQA_HD)
            if n_blocks == 0 and h + 1 < GQA_KV_HEADS:
                start_head(h + 1)
            st = s_ref[h, 0, :TM, :]
            m = jnp.max(st, axis=0, keepdims=True)
            p = jnp.exp2(st - m)
            l = jnp.sum(p, axis=0, keepdims=True)
            acc_ref[h] = _dot(vt_ref[0, rows, :TM], p.astype(BF16))
            for j in range(n_blocks):
                start = TM + j * GQA_KB
                if j + 1 < n_blocks:
                    scores(h, start + GQA_KB, GQA_KB, j % 2)
                elif h + 1 < GQA_KV_HEADS:
                    start_head(h + 1)
                st = s_ref[h, (j + 1) % 2]
                m_new = jnp.maximum(m, jnp.max(st, axis=0, keepdims=True))
                alpha = jnp.exp2(m - m_new)
                p = jnp.exp2(st - m_new)
                l = alpha * l + jnp.sum(p, axis=0, keepdims=True)
                acc_ref[h] = alpha * acc_ref[h] + _dot(vt_ref[0, rows, start:start + GQA_KB], p.astype(BF16))
                m = m_new
            ot = acc_ref[h] * (1.0 / l)
            for g in range(group):
                cols = slice((h * group + g) * GQA_HD, (h * group + g + 1) * GQA_HD)
                o_ref[0, :, cols] = ot[:, g * TM:(g + 1) * TM].T.astype(BF16)

    @pl.when(is_ctx)
    def _():
        run(0)

    @pl.when(jnp.logical_not(is_ctx))
    def _():
        run(n_lat)


def _gqa(q, k, vt, need_ctx):
    bsz, t, nq = q.shape
    nkv = k.shape[2]
    off = 0 if need_ctx else 1
    group = GQA_HEADS // GQA_KV_HEADS
    assert (t - TM) % GQA_KB == 0
    return pl.pallas_call(
        functools.partial(_gqa_kernel, off=off, n_lat=(t - TM) // GQA_KB),
        grid=(bsz, t // TM - off),
        in_specs=[pl.BlockSpec((1, TM, nq), lambda b, qi: (b, qi + off, 0)),
                  pl.BlockSpec((1, t, nkv), lambda b, qi: (b, 0, 0)),
                  pl.BlockSpec((1, nkv, t), lambda b, qi: (b, 0, 0))],
        out_specs=pl.BlockSpec((1, TM, nq), lambda b, qi: (b, qi + off, 0)),
        out_shape=jax.ShapeDtypeStruct((bsz, t, nq), BF16),
        scratch_shapes=[pltpu.VMEM((GQA_KV_HEADS, 2, GQA_KB, group * TM), F32),
                        pltpu.VMEM((GQA_KV_HEADS, GQA_HD, group * TM), F32)],
        compiler_params=_params(("parallel", "arbitrary")), name="gqa",
    )(q, k, vt)


def _merge_kernel(x_ref, gt_ref, y0_ref, y1_ref, y2_ref, sg_ref, wb_ref, wo_ref, o_ref):
    z = None
    for i, y_ref in enumerate((y0_ref, y1_ref, y2_ref)):
        zi = sg_ref[0, :, i * D_MODEL:(i + 1) * D_MODEL].astype(F32) * _dot(y_ref[0], wb_ref[i])
        z = zi if z is None else z + zi
    o_ref[0] = x_ref[0] + gt_ref[0] * _dot(z.astype(BF16), wo_ref[...])


def _merge(xall, mod, ys, sg, wb, wo, off):
    bsz, t, _ = xall.shape
    ctx_row = mod.shape[0] - 1
    return pl.pallas_call(
        _merge_kernel,
        grid=(bsz, t // TM - off),
        in_specs=[_row_spec(off, D_MODEL), _mod_spec(5, off, ctx_row),
                  _row_spec(off, D_MODEL), _row_spec(off, D_MODEL), _row_spec(off, D_MODEL),
                  _row_spec(off, 3 * D_MODEL), _whole(wb, 2), _whole(wo, 2)],
        out_specs=_row_spec(off, D_MODEL),
        out_shape=jax.ShapeDtypeStruct((bsz, t, D_MODEL), F32),
        compiler_params=_params(("parallel", "parallel")), name="merge",
    )(xall, mod, ys[0], ys[1], ys[2], sg, wb, wo)


def _block_diag_ones(head_dim):
    i = np.arange(MXU)
    return jnp.asarray((i[:, None] // head_dim) == (i[None, :] // head_dim), dtype=BF16)


def _chunk_tri(upper):
    i = np.arange(TM)
    same = (i[:, None] // GLA_CHUNK) == (i[None, :] // GLA_CHUNK)
    tri = (i[None, :] >= i[:, None]) if upper else (i[:, None] >= i[None, :])
    return jnp.asarray(same & tri, dtype=BF16)


def _rope_tables(ctx_len, seq):
    quarter = GQA_HD // 4
    freqs = ROPE_BASE ** (-np.arange(quarter, dtype=np.float64) / quarter)
    tok = np.arange(seq)
    ang_r = (tok // GRID_W)[:, None] * freqs
    ang_c = (tok % GRID_W)[:, None] * freqs
    ang = np.concatenate([ang_r, ang_r, ang_c, ang_c], axis=1)
    sign = np.tile(np.concatenate([-np.ones(quarter), np.ones(quarter)]), 2)
    cos = np.concatenate([np.ones((ctx_len, GQA_HD)), np.cos(ang)], axis=0)
    sin = np.concatenate([np.zeros((ctx_len, GQA_HD)), np.sin(ang) * sign], axis=0)
    return jnp.asarray(cos, F32), jnp.asarray(sin, F32)


def _layer_weights(l, ffn_w_in, ffn_w_out, w_in, gla_fg_w2, gla_fg_b, gla_norm_g, nat_q_norm, nat_k_norm,
                   gqa_q_norm, gqa_k_norm, w_branch, w_out):
    w = w_in[l]
    nqk = GLA_HEADS * GLA_DK
    nv = GLA_HEADS * GLA_DV
    o_fg = 2 * nqk + 2 * nv
    o_nat = o_fg + 2 * GLA_LR
    n = NAT_HEADS * NAT_HD
    o_gqa = o_nat + 3 * n
    nq = GQA_HEADS * GQA_HD
    nkv = GQA_KV_HEADS * GQA_HD
    o_gate = o_gqa + nq + 2 * nkv
    pad = jnp.zeros((D_MODEL, LANES - 2 * GLA_LR), w.dtype)
    w2 = gla_fg_w2[l]
    zeros_lr = jnp.zeros((GLA_LR, nqk), w2.dtype)
    zeros_rest = jnp.zeros((LANES - 2 * GLA_LR, nqk), w2.dtype)
    return {
        "ffn": [(ffn_w_in[l, i, :, :D_FF].astype(BF16), ffn_w_in[l, i, :, D_FF:].astype(BF16),
                 ffn_w_out[l, i].astype(BF16)) for i in range(2)],
        "w_gla": w[:, :o_fg].astype(BF16),
        "w_fg": jnp.concatenate([w[:, o_fg:o_nat], pad], axis=1).astype(BF16),
        "w_nat_qk": w[:, o_nat:o_nat + 2 * n].astype(BF16),
        "w_nat_vt": w[:, o_nat + 2 * n:o_gqa].T.astype(BF16),
        "w_gqa_qk": w[:, o_gqa:o_gqa + nq + nkv].astype(BF16),
        "w_gqa_vt": w[:, o_gqa + nq + nkv:o_gate].T.astype(BF16),
        "w_gates": w[:, o_gate:].astype(BF16),
        "w2": jnp.concatenate([jnp.concatenate([w2[0], zeros_lr, zeros_rest], axis=0),
                               jnp.concatenate([zeros_lr, w2[1], zeros_rest], axis=0)], axis=1).astype(BF16),
        "b2": gla_fg_b[l].reshape(1, 2 * nqk),
        "tril": _chunk_tri(False),
        "triu": _chunk_tri(True),
        "gla_gain": gla_norm_g[l].reshape(1, GLA_DV),
        "nat_gq": jnp.tile(nat_q_norm[l], NAT_HEADS).reshape(1, n),
        "nat_gk": jnp.tile(nat_k_norm[l], NAT_HEADS).reshape(1, n),
        "gqa_gq": jnp.tile(gqa_q_norm[l], GQA_HEADS).reshape(1, nq),
        "gqa_gk": jnp.tile(gqa_k_norm[l], GQA_KV_HEADS).reshape(1, nkv),
        "ones64": _block_diag_ones(NAT_HD),
        "ones128": _block_diag_ones(GQA_HD),
        "w_branch": w_branch[l].astype(BF16),
        "w_out": w_out[l].astype(BF16),
    }


def kernel(x, c, ctx, c_ctx, w_mod, b_mod, norm_g, ffn_w_in, ffn_w_out, w_in, gla_fg_w2, gla_fg_b, gla_norm_g,
           nat_q_norm, nat_k_norm, nat_rpb, gqa_q_norm, gqa_k_norm, w_branch, w_out):
    bsz, seq, _ = x.shape
    ctx_len = ctx.shape[1]
    depth = w_mod.shape[0]
    assert seq % TM == 0 and ctx_len == TM and seq // TM >= NAT_KBLOCKS

    mod_rows = -(-(bsz + 1) // 8) * 8
    cvec = jnp.concatenate([c, c_ctx[None, :], jnp.zeros((mod_rows - bsz - 1, D_MODEL), c.dtype)], axis=0)
    mods = _mod_table(cvec, w_mod, b_mod)[:, :bsz + 1].reshape(depth, bsz + 1, 1, N_MOD * D_MODEL)
    rope_cos, rope_sin = _rope_tables(ctx_len, seq)
    nat_bias = _nat_bias_tables(nat_rpb.reshape((-1,) + nat_rpb.shape[2:]), seq // GRID_W)

    xall = x
    t = ctx_len + seq
    for l in range(depth):
        need_ctx = l < depth - 1
        off = 0 if need_ctx else 1
        lw = _layer_weights(l, ffn_w_in, ffn_w_out, w_in, gla_fg_w2, gla_fg_b, gla_norm_g, nat_q_norm,
                            nat_k_norm, gqa_q_norm, gqa_k_norm, w_branch, w_out)
        mod = mods[l]
        g = [norm_g[l, i].reshape(1, D_MODEL) for i in range(3)]
        xall = _ffn(xall, mod, g[0], *lw["ffn"][0], sub=0, off=0, out_off=0, out_tokens=t,
                    ctx=ctx if l == 0 else None)
        (gq, gk, gv, gog, ggf, ggb), (nq_, nk_, nvt), (aq, ak, avt, sg) = _projections(
            xall, mod, g[1], lw, rope_cos, rope_sin)
        y_gla = _gla(gq, gk, gv, gog, ggf, ggb, lw, ctx_len)
        y_nat = _nat(nq_, nk_, nvt, nat_bias, l, need_ctx)
        y_gqa = _gqa(aq, ak, avt, need_ctx)
        xall = _merge(xall, mod, (y_gla, y_nat, y_gqa), sg, lw["w_branch"], lw["w_out"], off)
        if need_ctx:
            xall = _ffn(xall, mod, g[2], *lw["ffn"][1], sub=2, off=0, out_off=0, out_tokens=t)
        else:
            xall = _ffn(xall, mod, g[2], *lw["ffn"][1], sub=2, off=1, out_off=0, out_tokens=seq)
    return xall
```

```python
import functools

import numpy as np
import jax
import jax.numpy as jnp
from jax import lax
from jax.experimental import pallas as pl
from jax.experimental.pallas import tpu as pltpu

F32 = jnp.float32
BF16 = jnp.bfloat16

D_MODEL = 1024
GRID_W = 64
N_MOD = 9
D_FF = 2816
RMS_EPS = 1e-6
NEG_INF = -1e30

GLA_HEADS = 4
GLA_DK = 128
GLA_DV = 256
GLA_LR = 16
GLA_TAU = 16.0
GLA_CHUNK = 128

NAT_HEADS = 16
NAT_HD = 64
WIN_R = 8
WIN_C = 16
NAT_QROWS = 4
NAT_KBLOCKS = 3

GQA_HEADS = 8
GQA_KV_HEADS = 2
GQA_HD = 128
ROPE_BASE = 10000.0
GQA_KB = 512
LOG2E = 1.4426950408889634

TM = 256
LANES = 128
MXU = 256
VMEM_LIMIT = 56 * 1024 * 1024


def _dot(a, b):
    return jnp.dot(a, b, preferred_element_type=F32)


def _dot_nt(a, b):
    return lax.dot_general(a, b, (((1,), (1,)), ((), ())), preferred_element_type=F32)


def _dot_tn(a, b):
    return lax.dot_general(a, b, (((0,), (0,)), ((), ())), preferred_element_type=F32)


def _sigmoid(x):
    return 1.0 / (1.0 + jnp.exp(-x))


def _log_sigmoid(x):
    return -(jnp.maximum(-x, 0.0) + jnp.log(1.0 + jnp.exp(-jnp.abs(x))))


def _norm_mod(x, g, shift, scale):
    y = x * lax.rsqrt(jnp.mean(x * x, axis=-1, keepdims=True) + RMS_EPS) * g
    return y * (1.0 + scale) + shift


def _head_norm(t, gain, ones_bd, head_dim):
    sq = (t * t).astype(BF16)
    parts = [_dot(sq[:, i:i + MXU], ones_bd) for i in range(0, t.shape[1], MXU)]
    ss = parts[0] if len(parts) == 1 else jnp.concatenate(parts, axis=1)
    return t * lax.rsqrt(ss * (1.0 / head_dim) + RMS_EPS) * gain


def _params(semantics):
    return pltpu.CompilerParams(dimension_semantics=semantics, vmem_limit_bytes=VMEM_LIMIT)


def _whole(arr, ngrid):
    zeros = (0,) * arr.ndim
    return pl.BlockSpec(arr.shape, lambda *_: zeros, pipeline_mode=pl.Buffered(1))


def _row_spec(off, width):
    return pl.BlockSpec((1, TM, width), lambda b, j: (b, j + off, 0))


def _mod_spec(col, off, ctx_row):
    return pl.BlockSpec((1, 1, D_MODEL), lambda b, j: (jnp.where(j + off == 0, ctx_row, b), 0, col))


def _mod_kernel(c_ref, w_ref, b_ref, o_ref):
    c = c_ref[...]
    s = (c * _sigmoid(c)).astype(BF16)
    o_ref[0] = _dot(s, w_ref[0].astype(BF16)) + b_ref[0]


def _mod_table(cvec, w_mod, b_mod):
    depth = w_mod.shape[0]
    rows = cvec.shape[0]
    return pl.pallas_call(
        _mod_kernel,
        grid=(depth, N_MOD),
        in_specs=[pl.BlockSpec((rows, D_MODEL), lambda l, n: (0, 0)),
                  pl.BlockSpec((1, D_MODEL, D_MODEL), lambda l, n: (l, 0, n)),
                  pl.BlockSpec((1, 1, D_MODEL), lambda l, n: (l, 0, n))],
        out_specs=pl.BlockSpec((1, rows, D_MODEL), lambda l, n: (l, 0, n)),
        out_shape=jax.ShapeDtypeStruct((depth, rows, N_MOD * D_MODEL), F32),
        compiler_params=_params(("arbitrary", "arbitrary")),
        name="mod_table",
    )(cvec, w_mod, b_mod.reshape(depth, 1, N_MOD * D_MODEL))


def _next_tile(b, j, nb, nj):
    wrap = j + 1 == nj
    last = jnp.logical_and(wrap, b + 1 == nb)
    return jnp.where(jnp.logical_and(wrap, jnp.logical_not(last)), b + 1, b), \
        jnp.where(wrap, jnp.where(last, j, 0), j + 1)


def _ffn_kernel(x_ref, ctx_ref, xn_ref, ctxn_ref, sh_ref, sc_ref, gt_ref, shn_ref, scn_ref, g_ref,
                wa_ref, wb_ref, wd_ref, o_ref, h_ref, *, split_input):
    b, j = pl.program_id(0), pl.program_id(1)

    def tile(x_r, ctx_r, jj):
        return jnp.where(jj == 0, ctx_r[0], x_r[0]) if split_input else x_r[0]

    def hidden(xt, shift, scale):
        u = _norm_mod(xt, g_ref[...], shift, scale).astype(BF16)
        a = _dot(u, wa_ref[...])
        return (a * _sigmoid(a) * _dot(u, wb_ref[...])).astype(BF16)

    @pl.when(jnp.logical_and(b == 0, j == 0))
    def _():
        h_ref[...] = hidden(tile(x_ref, ctx_ref, j), sh_ref[0], sc_ref[0])

    o_ref[0] = tile(x_ref, ctx_ref, j) + (0.5 * gt_ref[0]) * _dot(h_ref[...], wd_ref[...])
    _, jn = _next_tile(b, j, pl.num_programs(0), pl.num_programs(1))
    h_ref[...] = hidden(tile(xn_ref, ctxn_ref, jn), shn_ref[0], scn_ref[0])


def _ffn(xall, mod, g, wa, wb, wd, sub, off, out_off, out_tokens, ctx=None):
    bsz, t, _ = xall.shape
    split = ctx is not None
    if split:
        t += ctx.shape[1]
    nj = t // TM - off
    ctx_row = mod.shape[0] - 1
    cur = lambda b, j: (b, j)
    nxt = lambda b, j: _next_tile(b, j, bsz, nj)

    def specs(at):
        if split:
            x_spec = pl.BlockSpec((1, TM, D_MODEL), lambda b, j: (at(b, j)[0], jnp.maximum(at(b, j)[1] - 1, 0), 0))
            ctx_spec = pl.BlockSpec((1, TM, D_MODEL), lambda b, j: (at(b, j)[0], 0, 0))
        else:
            x_spec = pl.BlockSpec((1, TM, D_MODEL), lambda b, j: (at(b, j)[0], at(b, j)[1] + off, 0))
            ctx_spec = pl.BlockSpec((1, 8, D_MODEL), lambda b, j: (at(b, j)[0], 0, 0))
        return [x_spec, ctx_spec]

    def mod_spec(col, at):
        return pl.BlockSpec((1, 1, D_MODEL),
                            lambda b, j: (jnp.where(at(b, j)[1] + off == 0, ctx_row, at(b, j)[0]), 0, col))

    if not split:
        ctx = xall
    return pl.pallas_call(
        functools.partial(_ffn_kernel, split_input=split),
        grid=(bsz, nj),
        in_specs=specs(cur) + specs(nxt) +
                 [mod_spec(3 * sub, cur), mod_spec(3 * sub + 1, cur), mod_spec(3 * sub + 2, cur),
                  mod_spec(3 * sub, nxt), mod_spec(3 * sub + 1, nxt),
                  _whole(g, 2), _whole(wa, 2), _whole(wb, 2), _whole(wd, 2)],
        out_specs=_row_spec(out_off, D_MODEL),
        out_shape=jax.ShapeDtypeStruct((bsz, out_tokens, D_MODEL), F32),
        scratch_shapes=[pltpu.VMEM((TM, wd.shape[0]), BF16)],
        compiler_params=_params(("arbitrary", "arbitrary")),
        name="ffn",
    )(xall, ctx, xall, ctx, mod, mod, mod, mod, mod, g, wa, wb, wd)


def _chunk_cumsum(tri, la):
    hi = la.astype(BF16)
    lo = (la - hi.astype(F32)).astype(BF16)
    return _dot(tri, hi) + _dot(tri, lo)


def _proj_gla_part(u, w_ref, wfg_ref, w2_ref, b2_ref, tril_ref, triu_ref,
                   q_ref, k_ref, v_ref, og_ref, gf_ref, gb_ref):
    nqk = GLA_HEADS * GLA_DK
    nv = GLA_HEADS * GLA_DV
    fg = _dot(u, wfg_ref[...]).astype(BF16)
    q_ref[0] = (_dot(u, w_ref[:, :nqk]) * GLA_DK ** -0.5).astype(BF16)
    z = _dot(fg, w2_ref[...]) + b2_ref[...]
    k_ref[0] = _dot(u, w_ref[:, nqk:2 * nqk]).astype(BF16)
    la = _log_sigmoid(z) * (1.0 / GLA_TAU)
    v_ref[0] = _dot(u, w_ref[:, 2 * nqk:2 * nqk + nv]).astype(BF16)
    gf_ref[0] = _chunk_cumsum(tril_ref[...], la[:, :nqk])
    gb_ref[0] = _chunk_cumsum(triu_ref[...], la[:, nqk:])
    og = _dot(u, w_ref[:, 2 * nqk + nv:])
    og_ref[0] = (og * _sigmoid(og)).astype(BF16)


def _proj_nat_part(u, wqk_ref, wvt_ref, gq_ref, gk_ref, ones_ref, q_ref, k_ref, vt_ref):
    p = _dot(u, wqk_ref[...])
    n = NAT_HEADS * NAT_HD
    ones_bd = ones_ref[...]
    q_ref[0] = (_head_norm(p[:, :n], gq_ref[...], ones_bd, NAT_HD) * (NAT_HD ** -0.5 * LOG2E)).astype(BF16)
    k_ref[0] = _head_norm(p[:, n:], gk_ref[...], ones_bd, NAT_HD).astype(BF16)
    vt_ref[0] = _dot_nt(wvt_ref[...], u).astype(BF16)


def _rope(t, cos, sin, lane):
    partner = jnp.where(lane % (GQA_HD // 2) < GQA_HD // 4,
                        pltpu.roll(t, GQA_HD - GQA_HD // 4, axis=1), pltpu.roll(t, GQA_HD // 4, axis=1))
    return t * cos + partner * sin


def _proj_gqa_part(u, wqk_ref, wvt_ref, wg_ref, gq_ref, gk_ref, ones_ref, cos_ref, sin_ref,
                   q_ref, k_ref, vt_ref, sg_ref):
    p = _dot(u, wqk_ref[...])
    nq = GQA_HEADS * GQA_HD
    ones_bd = ones_ref[...]
    qn = _head_norm(p[:, :nq], gq_ref[...], ones_bd, GQA_HD)
    kn = _head_norm(p[:, nq:], gk_ref[...], ones_bd, GQA_HD)
    cos = cos_ref[...]
    sin = sin_ref[...]
    lane = lax.broadcasted_iota(jnp.int32, (TM, GQA_HD), 1)
    for h in range(GQA_HEADS):
        sl = slice(h * GQA_HD, (h + 1) * GQA_HD)
        q_ref[0, :, sl] = (_rope(qn[:, sl], cos, sin, lane) * (GQA_HD ** -0.5 * LOG2E)).astype(BF16)
    for h in range(GQA_KV_HEADS):
        sl = slice(h * GQA_HD, (h + 1) * GQA_HD)
        k_ref[0, :, sl] = _rope(kn[:, sl], cos, sin, lane).astype(BF16)
    vt_ref[0] = _dot_nt(wvt_ref[...], u).astype(BF16)
    sg_ref[0] = _sigmoid(_dot(u, wg_ref[...])).astype(BF16)


N_PROJ_GLA_IN, N_PROJ_NAT_IN, N_PROJ_GQA_IN = 6, 5, 8
N_PROJ_GLA_OUT, N_PROJ_NAT_OUT, N_PROJ_GQA_OUT = 6, 3, 4


def _proj_kernel(*refs):
    it = iter(refs)
    take = lambda count: [next(it) for _ in range(count)]
    x_ref, sh_ref, sc_ref, g_ref = take(4)
    gla_in, nat_in, gqa_in = take(N_PROJ_GLA_IN), take(N_PROJ_NAT_IN), take(N_PROJ_GQA_IN)
    gla_out, nat_out, gqa_out = take(N_PROJ_GLA_OUT), take(N_PROJ_NAT_OUT), take(N_PROJ_GQA_OUT)
    u_ref, = take(1)

    @pl.when(pl.program_id(0) == 0)
    def _():
        u_ref[...] = jnp.zeros_like(u_ref)

    u_prev = u_ref[...]
    _proj_nat_part(u_prev, *nat_in, *nat_out)
    u = _norm_mod(x_ref[0], g_ref[...], sh_ref[0], sc_ref[0]).astype(BF16)
    _proj_gqa_part(u_prev, *gqa_in, *gqa_out)
    _proj_gla_part(u, *gla_in, *gla_out)
    u_ref[...] = u


def _projections(xall, mod, g, lw, rope_cos, rope_sin):
    bsz, t, _ = xall.shape
    ctx_row = mod.shape[0] - 1
    nt = t // TM
    n_tiles = bsz * nt
    lead = lambda s: jnp.minimum(s, n_tiles - 1)
    lag = lambda s: jnp.maximum(s - 1, 0)

    def rows(at, width):
        return pl.BlockSpec((1, TM, width), lambda s: (at(s) // nt, at(s) % nt, 0))

    def cols(at, height):
        return pl.BlockSpec((1, height, TM), lambda s: (at(s) // nt, 0, at(s) % nt))

    def mod_spec(col):
        return pl.BlockSpec((1, 1, D_MODEL),
                            lambda s: (jnp.where(lead(s) % nt == 0, ctx_row, lead(s) // nt), 0, col))

    whole = lambda a: _whole(a, 1)
    sds = jax.ShapeDtypeStruct
    nqk = GLA_HEADS * GLA_DK
    nv = GLA_HEADS * GLA_DV
    n = NAT_HEADS * NAT_HD
    nq = GQA_HEADS * GQA_HD
    nkv = GQA_KV_HEADS * GQA_HD
    tab = pl.BlockSpec((TM, GQA_HD), lambda s: (lag(s) % nt, 0))
    gla_in = [lw["w_gla"], lw["w_fg"], lw["w2"], lw["b2"], lw["tril"], lw["triu"]]
    nat_in = [lw["w_nat_qk"], lw["w_nat_vt"], lw["nat_gq"], lw["nat_gk"], lw["ones64"]]
    gqa_in = [lw["w_gqa_qk"], lw["w_gqa_vt"], lw["w_gates"], lw["gqa_gq"], lw["gqa_gk"], lw["ones128"]]
    assert (len(gla_in), len(nat_in), len(gqa_in) + 2) == (N_PROJ_GLA_IN, N_PROJ_NAT_IN, N_PROJ_GQA_IN)
    outs = pl.pallas_call(
        _proj_kernel, grid=(n_tiles + 1,),
        in_specs=[rows(lead, D_MODEL), mod_spec(3), mod_spec(4), whole(g)]
                 + [whole(a) for a in gla_in + nat_in + gqa_in] + [tab, tab],
        out_specs=[rows(lead, nqk), rows(lead, nqk), rows(lead, nv), rows(lead, nv), rows(lead, nqk), rows(lead, nqk),
                   rows(lag, n), rows(lag, n), cols(lag, n),
                   rows(lag, nq), rows(lag, nkv), cols(lag, nkv), rows(lag, 3 * D_MODEL)],
        out_shape=[sds((bsz, t, nqk), BF16), sds((bsz, t, nqk), BF16), sds((bsz, t, nv), BF16),
                   sds((bsz, t, nv), BF16), sds((bsz, t, nqk), F32), sds((bsz, t, nqk), F32),
                   sds((bsz, t, n), BF16), sds((bsz, t, n), BF16), sds((bsz, n, t), BF16),
                   sds((bsz, t, nq), BF16), sds((bsz, t, nkv), BF16), sds((bsz, nkv, t), BF16),
                   sds((bsz, t, 3 * D_MODEL), BF16)],
        scratch_shapes=[pltpu.VMEM((TM, D_MODEL), BF16)],
        compiler_params=_params(("arbitrary",)), name="proj",
    )(xall, mod, mod, g, *gla_in, *nat_in, *gqa_in, rope_cos, rope_sin)
    return outs[:6], outs[6:9], outs[9:]


GLA_UNROLL = 2


def _gla_chain(n, forward, q_ref, k_ref, v_ref, g_ref, o_ref, state, tri):
    c = GLA_CHUNK
    rows = pl.ds(pl.multiple_of(n * c, c), c)
    g = g_ref[0, rows, :]
    if forward:
        g_mid = g[c // 2 - 1:c // 2, :]
        g_edge = g[c - 1:c, :]
    else:
        g_mid = g[c // 2:c // 2 + 1, :]
        g_edge = g[0:1, :]
    q = q_ref[0, rows, :].astype(F32)
    k = k_ref[0, rows, :].astype(F32)
    v = v_ref[0, rows, :]
    q_in = (q * jnp.exp(g - g_mid)).astype(BF16)
    k_in = (k * jnp.exp(g_mid - g)).astype(BF16)
    q_x = (q * jnp.exp(g)).astype(BF16)
    k_end = (k * jnp.exp(g_edge - g)).astype(BF16)
    decay = jnp.exp(g_edge)
    yield
    att = _dot_nt(q_in, k_in)
    s_in = state["s"]
    state["s"] = s_in * decay + _dot_tn(v, k_end)
    yield
    att = jnp.where(tri, att, 0.0).astype(BF16)
    o = _dot(att, v) + _dot_nt(q_x, s_in.astype(BF16))
    yield
    o_ref[rows, :] = o


def _gla_kernel(q_ref, k_ref, v_ref, og_ref, gf_ref, gb_ref, gain_ref, y_ref,
                of_ref, ob_ref, sf_ref, sb_ref, *, n_chunks, n_ctx_chunks):
    c = GLA_CHUNK
    ri = lax.broadcasted_iota(jnp.int32, (c, c), 0)
    ci = lax.broadcasted_iota(jnp.int32, (c, c), 1)
    lower = ri >= ci
    upper = ci >= ri
    sf_ref[...] = jnp.zeros_like(sf_ref)
    sb_ref[...] = jnp.zeros_like(sb_ref)

    def scan(it, carry):
        fwd = {"s": sf_ref[...]}
        bwd = {"s": sb_ref[...]}
        chains = []
        for u in range(GLA_UNROLL):
            s = it * GLA_UNROLL + u
            nb = jnp.where(s < n_ctx_chunks, n_ctx_chunks - 1 - s, n_chunks - 1 - (s - n_ctx_chunks))
            chains.append(_gla_chain(s, True, q_ref, k_ref, v_ref, gf_ref, of_ref, fwd, lower))
            chains.append(_gla_chain(nb, False, q_ref, k_ref, v_ref, gb_ref, ob_ref, bwd, upper))
        while chains:
            alive = []
            for ch in chains:
                try:
                    next(ch)
                    alive.append(ch)
                except StopIteration:
                    pass
            chains = alive
        sf_ref[...] = fwd["s"]
        sb_ref[...] = bwd["s"]
        return carry

    lax.fori_loop(0, n_chunks // GLA_UNROLL, scan, 0)

    def finish(n, carry):
        rows = pl.ds(pl.multiple_of(n * c, c), c)
        o = of_ref[rows, :] + ob_ref[rows, :]
        y = o * lax.rsqrt(jnp.mean(o * o, axis=-1, keepdims=True) + RMS_EPS) * gain_ref[...]
        y_ref[0, rows, :] = (y * og_ref[0, rows, :].astype(F32)).astype(BF16)
        return carry

    lax.fori_loop(0, n_chunks, finish, 0, unroll=2)


def _gla(q, k, v, og, gf, gb, lw, ctx_len):
    bsz, t, _ = q.shape
    assert (t // GLA_CHUNK) % GLA_UNROLL == 0 and ctx_len % GLA_CHUNK == 0
    seq_spec = lambda w: pl.BlockSpec((1, t, w), lambda b, h: (b, 0, h))
    kern = functools.partial(_gla_kernel, n_chunks=t // GLA_CHUNK, n_ctx_chunks=ctx_len // GLA_CHUNK)
    return pl.pallas_call(
        kern, grid=(bsz, GLA_HEADS),
        in_specs=[seq_spec(GLA_DK), seq_spec(GLA_DK), seq_spec(GLA_DV), seq_spec(GLA_DV),
                  seq_spec(GLA_DK), seq_spec(GLA_DK), pl.BlockSpec((1, GLA_DV), lambda b, h: (0, 0))],
        out_specs=seq_spec(GLA_DV),
        out_shape=jax.ShapeDtypeStruct((bsz, t, GLA_HEADS * GLA_DV), BF16),
        scratch_shapes=[pltpu.VMEM((t, GLA_DV), F32), pltpu.VMEM((t, GLA_DV), F32),
                        pltpu.VMEM((GLA_DV, GLA_DK), F32), pltpu.VMEM((GLA_DV, GLA_DK), F32)],
        compiler_params=_params(("parallel", "parallel")), name="gla",
    )(q, k, v, og, gf, gb, lw["gla_gain"])


NAT_GROUP = 8


def _nat_kernel(q_ref, k0_ref, k1_ref, k2_ref, kc_ref, v0_ref, v1_ref, v2_ref, vc_ref, bias_ref, o_ref, s_ref,
                *, off):
    is_ctx = pl.program_id(0) + off == 0
    lane = lax.broadcasted_iota(jnp.int32, (TM, LANES), 1)
    row = lax.broadcasted_iota(jnp.int32, (LANES, TM), 0)
    n_heads = 2 * NAT_GROUP
    n_lat = NAT_KBLOCKS * TM

    def run(lat_keys, lat_vals):
        kcats, vlats = {}, {}

        def scores(i):
            hp, e = divmod(i, 2)
            sl = slice(hp * LANES, (hp + 1) * LANES)
            if hp not in kcats:
                refs = lat_keys + [kc_ref]
                half = (len(refs) + 1) // 2
                kcats[hp] = [jnp.concatenate([r[0, :, sl] for r in part], axis=0)
                             for part in (refs[:half], refs[half:]) if part]
            q = q_ref[0, :, sl]
            qm = jnp.where((lane >= NAT_HD) if e else (lane < NAT_HD), q, jnp.zeros_like(q))
            at = 0
            for kpart in kcats[hp]:
                s_ref[i % 2, at:at + kpart.shape[0], :] = _dot_nt(kpart, qm)
                at += kpart.shape[0]

        scores(0)
        outs = []
        for i in range(n_heads):
            if i + 1 < n_heads:
                scores(i + 1)
            hp, e = divmod(i, 2)
            sl = slice(hp * LANES, (hp + 1) * LANES)
            n_l = n_lat if lat_keys else 0
            s_ctx = s_ref[i % 2, n_l:n_l + TM, :]
            m = jnp.max(s_ctx, axis=0, keepdims=True)
            if lat_keys:
                s_lat = s_ref[i % 2, :n_lat, :] + bias_ref[i, 0]
                m = jnp.maximum(m, jnp.max(s_lat, axis=0, keepdims=True))
            p_ctx = jnp.exp2(s_ctx - m)
            l = jnp.sum(p_ctx, axis=0, keepdims=True)
            o = _dot(vc_ref[0, sl, :], p_ctx.astype(BF16))
            if lat_keys:
                if hp not in vlats:
                    vlats[hp] = jnp.concatenate([r[0, sl, :] for r in lat_vals], axis=1)
                p_lat = jnp.exp2(s_lat - m)
                l = l + jnp.sum(p_lat, axis=0, keepdims=True)
                o = o + _dot(vlats[hp], p_lat.astype(BF16))
            outs.append(o * (1.0 / l))
            if e == 1:
                ot = jnp.where(row < NAT_HD, outs[-2], outs[-1])
                o_ref[0, :, sl] = ot.T.astype(BF16)

    @pl.when(is_ctx)
    def _():
        run([], [])

    @pl.when(jnp.logical_not(is_ctx))
    def _():
        run([k0_ref, k1_ref, k2_ref], [v0_ref, v1_ref, v2_ref])


def _nat(q, k, vt, bias, layer, need_ctx):
    bsz, t, n = q.shape
    off = 0 if need_ctx else 1
    nblk = t // TM - 1
    gw = NAT_GROUP * LANES
    ngroups = n // gw

    def kblock(qi, i):
        j = qi + off - 1
        return 1 + jnp.clip(j - 1, 0, nblk - NAT_KBLOCKS) + i

    def bias_class(qi):
        j = qi + off - 1
        return jnp.where(j <= 0, 0, jnp.where(j == nblk - 1, 2, 1))

    kspec = lambda i: pl.BlockSpec((1, TM, gw), lambda qi, hg, b: (b, kblock(qi, i), hg))
    vspec = lambda i: pl.BlockSpec((1, gw, TM), lambda qi, hg, b: (b, hg, kblock(qi, i)))
    return pl.pallas_call(
        functools.partial(_nat_kernel, off=off),
        grid=(t // TM - off, ngroups, bsz),
        in_specs=[pl.BlockSpec((1, TM, gw), lambda qi, hg, b: (b, qi + off, hg)),
                  kspec(0), kspec(1), kspec(2),
                  pl.BlockSpec((1, TM, gw), lambda qi, hg, b: (b, 0, hg)),
                  vspec(0), vspec(1), vspec(2),
                  pl.BlockSpec((1, gw, TM), lambda qi, hg, b: (b, hg, 0)),
                  pl.BlockSpec((2 * NAT_GROUP, 1, NAT_KBLOCKS * TM, TM),
                               lambda qi, hg, b: (layer * ngroups + hg, bias_class(qi), 0, 0))],
        out_specs=pl.BlockSpec((1, TM, gw), lambda qi, hg, b: (b, qi + off, hg)),
        out_shape=jax.ShapeDtypeStruct((bsz, t, n), BF16),
        scratch_shapes=[pltpu.VMEM((2, (NAT_KBLOCKS + 1) * TM, TM), F32)],
        compiler_params=_params(("parallel", "parallel", "parallel")), name="nat",
    )(q, k, k, k, k, vt, vt, vt, vt, bias)


def _nat_bias_tables(rpb, rows):
    nblk = rows // NAT_QROWS
    kr_win = min(WIN_R, rows)
    col = np.arange(GRID_W)
    c_start = np.clip(col - WIN_C // 2, 0, GRID_W - WIN_C)
    cmask = (col[:, None] >= c_start[None, :]) & (col[:, None] < c_start[None, :] + WIN_C)
    dc = np.clip(col[:, None] - col[None, :] + WIN_C - 1, 0, 2 * WIN_C - 2)
    drs, rmasks = [], []
    for jblk in (0, 1, nblk - 1):
        base = int(np.clip(jblk - 1, 0, nblk - NAT_KBLOCKS))
        r = NAT_QROWS * jblk + np.arange(NAT_QROWS)
        r_start = np.clip(r - kr_win // 2, 0, rows - kr_win)
        kr = NAT_QROWS * base + np.arange(NAT_KBLOCKS * NAT_QROWS)
        rmasks.append((kr[:, None] >= r_start[None, :]) & (kr[:, None] < r_start[None, :] + kr_win))
        drs.append(np.clip(kr[:, None] - r[None, :] + WIN_R - 1, 0, 2 * WIN_R - 2))
    drx = np.where(np.stack(rmasks), np.stack(drs), 2 * WIN_R - 1)
    dcx = np.where(cmask, dc, 2 * WIN_C - 1)
    nh = rpb.shape[0]
    ext = jnp.full((nh, 2 * WIN_R, 2 * WIN_C), NEG_INF, F32)
    ext = ext.at[:, :2 * WIN_R - 1, :2 * WIN_C - 1].set(rpb.astype(F32) * LOG2E)
    pick = np.tile(dcx[:, None, :], (1, NAT_QROWS, 1)).reshape(-1)
    onehot = jnp.asarray(np.arange(2 * WIN_C)[:, None] == pick[None, :], F32)
    a = jnp.einsum("hdj,jn->hdn", ext, onehot, precision=lax.Precision.HIGHEST)
    a = a.reshape(nh, 2 * WIN_R, GRID_W, NAT_QROWS * GRID_W)
    nk = NAT_KBLOCKS * NAT_QROWS * GRID_W
    return pl.pallas_call(
        functools.partial(_bias_kernel, drx=drx),
        grid=(nh,),
        in_specs=[pl.BlockSpec((1,) + a.shape[1:], lambda h: (h, 0, 0, 0))],
        out_specs=pl.BlockSpec((1, 3, nk, TM), lambda h: (h, 0, 0, 0)),
        out_shape=jax.ShapeDtypeStruct((nh, 3, nk, TM), F32),
        compiler_params=_params(("parallel",)), name="nat_bias",
    )(a)


def _bias_kernel(a_ref, o_ref, *, drx):
    lane_blk = lax.broadcasted_iota(jnp.int32, (GRID_W, TM), 1) // GRID_W
    n_cls, n_kri, n_ri = drx.shape
    for cls in range(n_cls):
        for kri in range(n_kri):
            strip = a_ref[0, int(drx[cls, kri, n_ri - 1])]
            for ri in range(n_ri - 1):
                strip = jnp.where(lane_blk == ri, a_ref[0, int(drx[cls, kri, ri])], strip)
            o_ref[0, cls, kri * GRID_W:(kri + 1) * GRID_W, :] = strip


def _gqa_kernel(q_ref, k_ref, vt_ref, o_ref, s_ref, acc_ref, *, off, n_lat):
    is_ctx = pl.program_id(1) + off == 0
    group = GQA_HEADS // GQA_KV_HEADS

    def run(n_blocks):
        qs = {}

        def scores(h, start, size, slot):
            if h not in qs:
                heads = range(h * group, (h + 1) * group)
                qs[h] = jnp.concatenate([q_ref[0, :, g * GQA_HD:(g + 1) * GQA_HD] for g in heads], axis=0)
            s_ref[h, slot, :size, :] = _dot_nt(k_ref[0, start:start + size, h * GQA_HD:(h + 1) * GQA_HD], qs[h])

        def start_head(h):
            scores(h, 0, TM, 0)
            if n_blocks:
                scores(h, TM, GQA_KB, 1)

        start_head(0)
        for h in range(GQA_KV_HEADS):
            rows = slice(h * GQA_HD, (h + 1) * GQA_HD)
            if n_blocks == 0 and h + 1 < GQA_KV_HEADS:
                start_head(h + 1)
            st = s_ref[h, 0, :TM, :]
            m = jnp.max(st, axis=0, keepdims=True)
            p = jnp.exp2(st - m)
            l = jnp.sum(p, axis=0, keepdims=True)
            acc_ref[h] = _dot(vt_ref[0, rows, :TM], p.astype(BF16))
            for j in range(n_blocks):
                start = TM + j * GQA_KB
                if j + 1 < n_blocks:
                    scores(h, start + GQA_KB, GQA_KB, j % 2)
                elif h + 1 < GQA_KV_HEADS:
                    start_head(h + 1)
                st = s_ref[h, (j + 1) % 2]
                m_new = jnp.maximum(m, jnp.max(st, axis=0, keepdims=True))
                alpha = jnp.exp2(m - m_new)
                p = jnp.exp2(st - m_new)
                l = alpha * l + jnp.sum(p, axis=0, keepdims=True)
                acc_ref[h] = alpha * acc_ref[h] + _dot(vt_ref[0, rows, start:start + GQA_KB], p.astype(BF16))
                m = m_new
            ot = acc_ref[h] * (1.0 / l)
            for g in range(group):
                cols = slice((h * group + g) * GQA_HD, (h * group + g + 1) * GQA_HD)
                o_ref[0, :, cols] = ot[:, g * TM:(g + 1) * TM].T.astype(BF16)

    @pl.when(is_ctx)
    def _():
        run(0)

    @pl.when(jnp.logical_not(is_ctx))
    def _():
        run(n_lat)


def _gqa(q, k, vt, need_ctx):
    bsz, t, nq = q.shape
    nkv = k.shape[2]
    off = 0 if need_ctx else 1
    group = GQA_HEADS // GQA_KV_HEADS
    assert (t - TM) % GQA_KB == 0
    return pl.pallas_call(
        functools.partial(_gqa_kernel, off=off, n_lat=(t - TM) // GQA_KB),
        grid=(bsz, t // TM - off),
        in_specs=[pl.BlockSpec((1, TM, nq), lambda b, qi: (b, qi + off, 0)),
                  pl.BlockSpec((1, t, nkv), lambda b, qi: (b, 0, 0)),
                  pl.BlockSpec((1, nkv, t), lambda b, qi: (b, 0, 0))],
        out_specs=pl.BlockSpec((1, TM, nq), lambda b, qi: (b, qi + off, 0)),
        out_shape=jax.ShapeDtypeStruct((bsz, t, nq), BF16),
        scratch_shapes=[pltpu.VMEM((GQA_KV_HEADS, 2, GQA_KB, group * TM), F32),
                        pltpu.VMEM((GQA_KV_HEADS, GQA_HD, group * TM), F32)],
        compiler_params=_params(("parallel", "arbitrary")), name="gqa",
    )(q, k, vt)


def _merge_kernel(x_ref, gt_ref, y0_ref, y1_ref, y2_ref, sg_ref, wb_ref, wo_ref, o_ref):
    z = None
    for i, y_ref in enumerate((y0_ref, y1_ref, y2_ref)):
        zi = sg_ref[0, :, i * D_MODEL:(i + 1) * D_MODEL].astype(F32) * _dot(y_ref[0], wb_ref[i])
        z = zi if z is None else z + zi
    o_ref[0] = x_ref[0] + gt_ref[0] * _dot(z.astype(BF16), wo_ref[...])


def _merge(xall, mod, ys, sg, wb, wo, off):
    bsz, t, _ = xall.shape
    ctx_row = mod.shape[0] - 1
    return pl.pallas_call(
        _merge_kernel,
        grid=(bsz, t // TM - off),
        in_specs=[_row_spec(off, D_MODEL), _mod_spec(5, off, ctx_row),
                  _row_spec(off, D_MODEL), _row_spec(off, D_MODEL), _row_spec(off, D_MODEL),
                  _row_spec(off, 3 * D_MODEL), _whole(wb, 2), _whole(wo, 2)],
        out_specs=_row_spec(off, D_MODEL),
        out_shape=jax.ShapeDtypeStruct((bsz, t, D_MODEL), F32),
        compiler_params=_params(("parallel", "parallel")), name="merge",
    )(xall, mod, ys[0], ys[1], ys[2], sg, wb, wo)


def _block_diag_ones(head_dim):
    i = np.arange(MXU)
    return jnp.asarray((i[:, None] // head_dim) == (i[None, :] // head_dim), dtype=BF16)


def _chunk_tri(upper):
    i = np.arange(TM)
    same = (i[:, None] // GLA_CHUNK) == (i[None, :] // GLA_CHUNK)
    tri = (i[None, :] >= i[:, None]) if upper else (i[:, None] >= i[None, :])
    return jnp.asarray(same & tri, dtype=BF16)


def _rope_tables(ctx_len, seq):
    quarter = GQA_HD // 4
    freqs = ROPE_BASE ** (-np.arange(quarter, dtype=np.float64) / quarter)
    tok = np.arange(seq)
    ang_r = (tok // GRID_W)[:, None] * freqs
    ang_c = (tok % GRID_W)[:, None] * freqs
    ang = np.concatenate([ang_r, ang_r, ang_c, ang_c], axis=1)
    sign = np.tile(np.concatenate([-np.ones(quarter), np.ones(quarter)]), 2)
    cos = np.concatenate([np.ones((ctx_len, GQA_HD)), np.cos(ang)], axis=0)
    sin = np.concatenate([np.zeros((ctx_len, GQA_HD)), np.sin(ang) * sign], axis=0)
    return jnp.asarray(cos, F32), jnp.asarray(sin, F32)


def _layer_weights(l, ffn_w_in, ffn_w_out, w_in, gla_fg_w2, gla_fg_b, gla_norm_g, nat_q_norm, nat_k_norm,
                   gqa_q_norm, gqa_k_norm, w_branch, w_out):
    w = w_in[l]
    nqk = GLA_HEADS * GLA_DK
    nv = GLA_HEADS * GLA_DV
    o_fg = 2 * nqk + 2 * nv
    o_nat = o_fg + 2 * GLA_LR
    n = NAT_HEADS * NAT_HD
    o_gqa = o_nat + 3 * n
    nq = GQA_HEADS * GQA_HD
    nkv = GQA_KV_HEADS * GQA_HD
    o_gate = o_gqa + nq + 2 * nkv
    pad = jnp.zeros((D_MODEL, LANES - 2 * GLA_LR), w.dtype)
    w2 = gla_fg_w2[l]
    zeros_lr = jnp.zeros((GLA_LR, nqk), w2.dtype)
    zeros_rest = jnp.zeros((LANES - 2 * GLA_LR, nqk), w2.dtype)
    return {
        "ffn": [(ffn_w_in[l, i, :, :D_FF].astype(BF16), ffn_w_in[l, i, :, D_FF:].astype(BF16),
                 ffn_w_out[l, i].astype(BF16)) for i in range(2)],
        "w_gla": w[:, :o_fg].astype(BF16),
        "w_fg": jnp.concatenate([w[:, o_fg:o_nat], pad], axis=1).astype(BF16),
        "w_nat_qk": w[:, o_nat:o_nat + 2 * n].astype(BF16),
        "w_nat_vt": w[:, o_nat + 2 * n:o_gqa].T.astype(BF16),
        "w_gqa_qk": w[:, o_gqa:o_gqa + nq + nkv].astype(BF16),
        "w_gqa_vt": w[:, o_gqa + nq + nkv:o_gate].T.astype(BF16),
        "w_gates": w[:, o_gate:].astype(BF16),
        "w2": jnp.concatenate([jnp.concatenate([w2[0], zeros_lr, zeros_rest], axis=0),
                               jnp.concatenate([zeros_lr, w2[1], zeros_rest], axis=0)], axis=1).astype(BF16),
        "b2": gla_fg_b[l].reshape(1, 2 * nqk),
        "tril": _chunk_tri(False),
        "triu": _chunk_tri(True),
        "gla_gain": gla_norm_g[l].reshape(1, GLA_DV),
        "nat_gq": jnp.tile(nat_q_norm[l], NAT_HEADS).reshape(1, n),
        "nat_gk": jnp.tile(nat_k_norm[l], NAT_HEADS).reshape(1, n),
        "gqa_gq": jnp.tile(gqa_q_norm[l], GQA_HEADS).reshape(1, nq),
        "gqa_gk": jnp.tile(gqa_k_norm[l], GQA_KV_HEADS).reshape(1, nkv),
        "ones64": _block_diag_ones(NAT_HD),
        "ones128": _block_diag_ones(GQA_HD),
        "w_branch": w_branch[l].astype(BF16),
        "w_out": w_out[l].astype(BF16),
    }


def kernel(x, c, ctx, c_ctx, w_mod, b_mod, norm_g, ffn_w_in, ffn_w_out, w_in, gla_fg_w2, gla_fg_b, gla_norm_g,
           nat_q_norm, nat_k_norm, nat_rpb, gqa_q_norm, gqa_k_norm, w_branch, w_out):
    bsz, seq, _ = x.shape
    ctx_len = ctx.shape[1]
    depth = w_mod.shape[0]
    assert seq % TM == 0 and ctx_len == TM and seq // TM >= NAT_KBLOCKS

    mod_rows = -(-(bsz + 1) // 8) * 8
    cvec = jnp.concatenate([c, c_ctx[None, :], jnp.zeros((mod_rows - bsz - 1, D_MODEL), c.dtype)], axis=0)
    mods = _mod_table(cvec, w_mod, b_mod)[:, :bsz + 1].reshape(depth, bsz + 1, 1, N_MOD * D_MODEL)
    rope_cos, rope_sin = _rope_tables(ctx_len, seq)
    nat_bias = _nat_bias_tables(nat_rpb.reshape((-1,) + nat_rpb.shape[2:]), seq // GRID_W)

    xall = x
    t = ctx_len + seq
    for l in range(depth):
        need_ctx = l < depth - 1
        off = 0 if need_ctx else 1
        lw = _layer_weights(l, ffn_w_in, ffn_w_out, w_in, gla_fg_w2, gla_fg_b, gla_norm_g, nat_q_norm,
                            nat_k_norm, gqa_q_norm, gqa_k_norm, w_branch, w_out)
        mod = mods[l]
        g = [norm_g[l, i].reshape(1, D_MODEL) for i in range(3)]
        xall = _ffn(xall, mod, g[0], *lw["ffn"][0], sub=0, off=0, out_off=0, out_tokens=t,
                    ctx=ctx if l == 0 else None)
        (gq, gk, gv, gog, ggf, ggb), (nq_, nk_, nvt), (aq, ak, avt, sg) = _projections(
            xall, mod, g[1], lw, rope_cos, rope_sin)
        y_gla = _gla(gq, gk, gv, gog, ggf, ggb, lw, ctx_len)
        y_nat = _nat(nq_, nk_, nvt, nat_bias, l, need_ctx)
        y_gqa = _gqa(aq, ak, avt, need_ctx)
        xall = _merge(xall, mod, (y_gla, y_nat, y_gqa), sg, lw["w_branch"], lw["w_out"], off)
        if need_ctx:
            xall = _ffn(xall, mod, g[2], *lw["ffn"][1], sub=2, off=0, out_off=0, out_tokens=t)
        else:
            xall = _ffn(xall, mod, g[2], *lw["ffn"][1], sub=2, off=1, out_off=0, out_tokens=seq)
    return xall
```

```python
import functools

import numpy as np
import jax
import jax.numpy as jnp
from jax import lax
from jax.experimental import pallas as pl
from jax.experimental.pallas import tpu as pltpu

F32 = jnp.float32
BF16 = jnp.bfloat16

D_MODEL = 1024
GRID_W = 64
N_MOD = 9
D_FF = 2816
RMS_EPS = 1e-6
NEG_INF = -1e30

GLA_HEADS = 4
GLA_DK = 128
GLA_DV = 256
GLA_LR = 16
GLA_TAU = 16.0
GLA_CHUNK = 128

NAT_HEADS = 16
NAT_HD = 64
WIN_R = 8
WIN_C = 16
NAT_QROWS = 4
NAT_KBLOCKS = 3

GQA_HEADS = 8
GQA_KV_HEADS = 2
GQA_HD = 128
ROPE_BASE = 10000.0
GQA_KB = 512
LOG2E = 1.4426950408889634

TM = 256
LANES = 128
MXU = 256
VMEM_LIMIT = 56 * 1024 * 1024


def _dot(a, b):
    return jnp.dot(a, b, preferred_element_type=F32)


def _dot_nt(a, b):
    return lax.dot_general(a, b, (((1,), (1,)), ((), ())), preferred_element_type=F32)


def _dot_tn(a, b):
    return lax.dot_general(a, b, (((0,), (0,)), ((), ())), preferred_element_type=F32)


def _sigmoid(x):
    return 1.0 / (1.0 + jnp.exp(-x))


def _log_sigmoid(x):
    return -(jnp.maximum(-x, 0.0) + jnp.log(1.0 + jnp.exp(-jnp.abs(x))))


def _norm_mod(x, g, shift, scale):
    y = x * lax.rsqrt(jnp.mean(x * x, axis=-1, keepdims=True) + RMS_EPS) * g
    return y * (1.0 + scale) + shift


def _head_norm(t, gain, ones_bd, head_dim):
    sq = (t * t).astype(BF16)
    parts = [_dot(sq[:, i:i + MXU], ones_bd) for i in range(0, t.shape[1], MXU)]
    ss = parts[0] if len(parts) == 1 else jnp.concatenate(parts, axis=1)
    return t * lax.rsqrt(ss * (1.0 / head_dim) + RMS_EPS) * gain


def _params(semantics):
    return pltpu.CompilerParams(dimension_semantics=semantics, vmem_limit_bytes=VMEM_LIMIT)


def _whole(arr, ngrid):
    zeros = (0,) * arr.ndim
    return pl.BlockSpec(arr.shape, lambda *_: zeros, pipeline_mode=pl.Buffered(1))


def _row_spec(off, width):
    return pl.BlockSpec((1, TM, width), lambda b, j: (b, j + off, 0))


def _mod_spec(col, off, ctx_row):
    return pl.BlockSpec((1, 1, D_MODEL), lambda b, j: (jnp.where(j + off == 0, ctx_row, b), 0, col))


def _mod_kernel(c_ref, w_ref, b_ref, o_ref):
    c = c_ref[...]
    s = (c * _sigmoid(c)).astype(BF16)
    o_ref[0] = _dot(s, w_ref[0].astype(BF16)) + b_ref[0]


def _mod_table(cvec, w_mod, b_mod):
    depth = w_mod.shape[0]
    rows = cvec.shape[0]
    return pl.pallas_call(
        _mod_kernel,
        grid=(depth, N_MOD),
        in_specs=[pl.BlockSpec((rows, D_MODEL), lambda l, n: (0, 0)),
                  pl.BlockSpec((1, D_MODEL, D_MODEL), lambda l, n: (l, 0, n)),
                  pl.BlockSpec((1, 1, D_MODEL), lambda l, n: (l, 0, n))],
        out_specs=pl.BlockSpec((1, rows, D_MODEL), lambda l, n: (l, 0, n)),
        out_shape=jax.ShapeDtypeStruct((depth, rows, N_MOD * D_MODEL), F32),
        compiler_params=_params(("arbitrary", "arbitrary")),
        name="mod_table",
    )(cvec, w_mod, b_mod.reshape(depth, 1, N_MOD * D_MODEL))


def _next_tile(b, j, nb, nj):
    wrap = j + 1 == nj
    last = jnp.logical_and(wrap, b + 1 == nb)
    return jnp.where(jnp.logical_and(wrap, jnp.logical_not(last)), b + 1, b), \
        jnp.where(wrap, jnp.where(last, j, 0), j + 1)


def _ffn_kernel(x_ref, ctx_ref, xn_ref, ctxn_ref, sh_ref, sc_ref, gt_ref, shn_ref, scn_ref, g_ref,
                wa_ref, wb_ref, wd_ref, o_ref, h_ref, *, split_input):
    b, j = pl.program_id(0), pl.program_id(1)

    def tile(x_r, ctx_r, jj):
        return jnp.where(jj == 0, ctx_r[0], x_r[0]) if split_input else x_r[0]

    def hidden(xt, shift, scale):
        u = _norm_mod(xt, g_ref[...], shift, scale).astype(BF16)
        a = _dot(u, wa_ref[...])
        return (a * _sigmoid(a) * _dot(u, wb_ref[...])).astype(BF16)

    @pl.when(jnp.logical_and(b == 0, j == 0))
    def _():
        h_ref[...] = hidden(tile(x_ref, ctx_ref, j), sh_ref[0], sc_ref[0])

    o_ref[0] = tile(x_ref, ctx_ref, j) + (0.5 * gt_ref[0]) * _dot(h_ref[...], wd_ref[...])
    _, jn = _next_tile(b, j, pl.num_programs(0), pl.num_programs(1))
    h_ref[...] = hidden(tile(xn_ref, ctxn_ref, jn), shn_ref[0], scn_ref[0])


def _ffn(xall, mod, g, wa, wb, wd, sub, off, out_off, out_tokens, ctx=None):
    bsz, t, _ = xall.shape
    split = ctx is not None
    if split:
        t += ctx.shape[1]
    nj = t // TM - off
    ctx_row = mod.shape[0] - 1
    cur = lambda b, j: (b, j)
    nxt = lambda b, j: _next_tile(b, j, bsz, nj)

    def specs(at):
        if split:
            x_spec = pl.BlockSpec((1, TM, D_MODEL), lambda b, j: (at(b, j)[0], jnp.maximum(at(b, j)[1] - 1, 0), 0))
            ctx_spec = pl.BlockSpec((1, TM, D_MODEL), lambda b, j: (at(b, j)[0], 0, 0))
        else:
            x_spec = pl.BlockSpec((1, TM, D_MODEL), lambda b, j: (at(b, j)[0], at(b, j)[1] + off, 0))
            ctx_spec = pl.BlockSpec((1, 8, D_MODEL), lambda b, j: (at(b, j)[0], 0, 0))
        return [x_spec, ctx_spec]

    def mod_spec(col, at):
        return pl.BlockSpec((1, 1, D_MODEL),
                            lambda b, j: (jnp.where(at(b, j)[1] + off == 0, ctx_row, at(b, j)[0]), 0, col))

    if not split:
        ctx = xall
    return pl.pallas_call(
        functools.partial(_ffn_kernel, split_input=split),
        grid=(bsz, nj),
        in_specs=specs(cur) + specs(nxt) +
                 [mod_spec(3 * sub, cur), mod_spec(3 * sub + 1, cur), mod_spec(3 * sub + 2, cur),
                  mod_spec(3 * sub, nxt), mod_spec(3 * sub + 1, nxt),
                  _whole(g, 2), _whole(wa, 2), _whole(wb, 2), _whole(wd, 2)],
        out_specs=_row_spec(out_off, D_MODEL),
        out_shape=jax.ShapeDtypeStruct((bsz, out_tokens, D_MODEL), F32),
        scratch_shapes=[pltpu.VMEM((TM, wd.shape[0]), BF16)],
        compiler_params=_params(("arbitrary", "arbitrary")),
        name="ffn",
    )(xall, ctx, xall, ctx, mod, mod, mod, mod, mod, g, wa, wb, wd)


def _chunk_cumsum(tri, la):
    hi = la.astype(BF16)
    lo = (la - hi.astype(F32)).astype(BF16)
    return _dot(tri, hi) + _dot(tri, lo)


def _proj_gla_part(u, w_ref, wfg_ref, w2_ref, b2_ref, tril_ref, triu_ref,
                   q_ref, k_ref, v_ref, og_ref, gf_ref, gb_ref):
    nqk = GLA_HEADS * GLA_DK
    nv = GLA_HEADS * GLA_DV
    fg = _dot(u, wfg_ref[...]).astype(BF16)
    q_ref[0] = (_dot(u, w_ref[:, :nqk]) * GLA_DK ** -0.5).astype(BF16)
    z = _dot(fg, w2_ref[...]) + b2_ref[...]
    k_ref[0] = _dot(u, w_ref[:, nqk:2 * nqk]).astype(BF16)
    la = _log_sigmoid(z) * (1.0 / GLA_TAU)
    v_ref[0] = _dot(u, w_ref[:, 2 * nqk:2 * nqk + nv]).astype(BF16)
    gf_ref[0] = _chunk_cumsum(tril_ref[...], la[:, :nqk])
    gb_ref[0] = _chunk_cumsum(triu_ref[...], la[:, nqk:])
    og = _dot(u, w_ref[:, 2 * nqk + nv:])
    og_ref[0] = (og * _sigmoid(og)).astype(BF16)


def _proj_nat_part(u, wqk_ref, wvt_ref, gq_ref, gk_ref, ones_ref, q_ref, k_ref, vt_ref):
    p = _dot(u, wqk_ref[...])
    n = NAT_HEADS * NAT_HD
    ones_bd = ones_ref[...]
    q_ref[0] = (_head_norm(p[:, :n], gq_ref[...], ones_bd, NAT_HD) * (NAT_HD ** -0.5 * LOG2E)).astype(BF16)
    k_ref[0] = _head_norm(p[:, n:], gk_ref[...], ones_bd, NAT_HD).astype(BF16)
    vt_ref[0] = _dot_nt(wvt_ref[...], u).astype(BF16)


def _rope(t, cos, sin, lane):
    partner = jnp.where(lane % (GQA_HD // 2) < GQA_HD // 4,
                        pltpu.roll(t, GQA_HD - GQA_HD // 4, axis=1), pltpu.roll(t, GQA_HD // 4, axis=1))
    return t * cos + partner * sin


def _proj_gqa_part(u, wqk_ref, wvt_ref, wg_ref, gq_ref, gk_ref, ones_ref, cos_ref, sin_ref,
                   q_ref, k_ref, vt_ref, sg_ref):
    p = _dot(u, wqk_ref[...])
    nq = GQA_HEADS * GQA_HD
    ones_bd = ones_ref[...]
    qn = _head_norm(p[:, :nq], gq_ref[...], ones_bd, GQA_HD)
    kn = _head_norm(p[:, nq:], gk_ref[...], ones_bd, GQA_HD)
    cos = cos_ref[...]
    sin = sin_ref[...]
    lane = lax.broadcasted_iota(jnp.int32, (TM, GQA_HD), 1)
    for h in range(GQA_HEADS):
        sl = slice(h * GQA_HD, (h + 1) * GQA_HD)
        q_ref[0, :, sl] = (_rope(qn[:, sl], cos, sin, lane) * (GQA_HD ** -0.5 * LOG2E)).astype(BF16)
    for h in range(GQA_KV_HEADS):
        sl = slice(h * GQA_HD, (h + 1) * GQA_HD)
        k_ref[0, :, sl] = _rope(kn[:, sl], cos, sin, lane).astype(BF16)
    vt_ref[0] = _dot_nt(wvt_ref[...], u).astype(BF16)
    sg_ref[0] = _sigmoid(_dot(u, wg_ref[...])).astype(BF16)


N_PROJ_GLA_IN, N_PROJ_NAT_IN, N_PROJ_GQA_IN = 6, 5, 8
N_PROJ_GLA_OUT, N_PROJ_NAT_OUT, N_PROJ_GQA_OUT = 6, 3, 4


def _proj_kernel(*refs):
    it = iter(refs)
    take = lambda count: [next(it) for _ in range(count)]
    x_ref, sh_ref, sc_ref, g_ref = take(4)
    gla_in, nat_in, gqa_in = take(N_PROJ_GLA_IN), take(N_PROJ_NAT_IN), take(N_PROJ_GQA_IN)
    gla_out, nat_out, gqa_out = take(N_PROJ_GLA_OUT), take(N_PROJ_NAT_OUT), take(N_PROJ_GQA_OUT)
    u_ref, = take(1)

    @pl.when(pl.program_id(0) == 0)
    def _():
        u_ref[...] = jnp.zeros_like(u_ref)

    u_prev = u_ref[...]
    _proj_nat_part(u_prev, *nat_in, *nat_out)
    u = _norm_mod(x_ref[0], g_ref[...], sh_ref[0], sc_ref[0]).astype(BF16)
    _proj_gqa_part(u_prev, *gqa_in, *gqa_out)
    _proj_gla_part(u, *gla_in, *gla_out)
    u_ref[...] = u


def _projections(xall, mod, g, lw, rope_cos, rope_sin):
    bsz, t, _ = xall.shape
    ctx_row = mod.shape[0] - 1
    nt = t // TM
    n_tiles = bsz * nt
    lead = lambda s: jnp.minimum(s, n_tiles - 1)
    lag = lambda s: jnp.maximum(s - 1, 0)

    def rows(at, width):
        return pl.BlockSpec((1, TM, width), lambda s: (at(s) // nt, at(s) % nt, 0))

    def cols(at, height):
        return pl.BlockSpec((1, height, TM), lambda s: (at(s) // nt, 0, at(s) % nt))

    def mod_spec(col):
        return pl.BlockSpec((1, 1, D_MODEL),
                            lambda s: (jnp.where(lead(s) % nt == 0, ctx_row, lead(s) // nt), 0, col))

    whole = lambda a: _whole(a, 1)
    sds = jax.ShapeDtypeStruct
    nqk = GLA_HEADS * GLA_DK
    nv = GLA_HEADS * GLA_DV
    n = NAT_HEADS * NAT_HD
    nq = GQA_HEADS * GQA_HD
    nkv = GQA_KV_HEADS * GQA_HD
    tab = pl.BlockSpec((TM, GQA_HD), lambda s: (lag(s) % nt, 0))
    gla_in = [lw["w_gla"], lw["w_fg"], lw["w2"], lw["b2"], lw["tril"], lw["triu"]]
    nat_in = [lw["w_nat_qk"], lw["w_nat_vt"], lw["nat_gq"], lw["nat_gk"], lw["ones64"]]
    gqa_in = [lw["w_gqa_qk"], lw["w_gqa_vt"], lw["w_gates"], lw["gqa_gq"], lw["gqa_gk"], lw["ones128"]]
    assert (len(gla_in), len(nat_in), len(gqa_in) + 2) == (N_PROJ_GLA_IN, N_PROJ_NAT_IN, N_PROJ_GQA_IN)
    outs = pl.pallas_call(
        _proj_kernel, grid=(n_tiles + 1,),
        in_specs=[rows(lead, D_MODEL), mod_spec(3), mod_spec(4), whole(g)]
                 + [whole(a) for a in gla_in + nat_in + gqa_in] + [tab, tab],
        out_specs=[rows(lead, nqk), rows(lead, nqk), rows(lead, nv), rows(lead, nv), rows(lead, nqk), rows(lead, nqk),
                   rows(lag, n), rows(lag, n), cols(lag, n),
                   rows(lag, nq), rows(lag, nkv), cols(lag, nkv), rows(lag, 3 * D_MODEL)],
        out_shape=[sds((bsz, t, nqk), BF16), sds((bsz, t, nqk), BF16), sds((bsz, t, nv), BF16),
                   sds((bsz, t, nv), BF16), sds((bsz, t, nqk), F32), sds((bsz, t, nqk), F32),
                   sds((bsz, t, n), BF16), sds((bsz, t, n), BF16), sds((bsz, n, t), BF16),
                   sds((bsz, t, nq), BF16), sds((bsz, t, nkv), BF16), sds((bsz, nkv, t), BF16),
                   sds((bsz, t, 3 * D_MODEL), BF16)],
        scratch_shapes=[pltpu.VMEM((TM, D_MODEL), BF16)],
        compiler_params=_params(("arbitrary",)), name="proj",
    )(xall, mod, mod, g, *gla_in, *nat_in, *gqa_in, rope_cos, rope_sin)
    return outs[:6], outs[6:9], outs[9:]


GLA_UNROLL = 4


def _gla_chain(n, forward, q_ref, k_ref, v_ref, g_ref, o_ref, state, tri, emit):
    c = GLA_CHUNK
    rows = pl.ds(pl.multiple_of(jnp.int32(n) * c, c), c)
    g = g_ref[0, rows, :]
    if forward:
        g_mid = g[c // 2 - 1:c // 2, :]
        g_edge = g[c - 1:c, :]
    else:
        g_mid = g[c // 2:c // 2 + 1, :]
        g_edge = g[0:1, :]
    q = q_ref[0, rows, :].astype(F32)
    k = k_ref[0, rows, :].astype(F32)
    v = v_ref[0, rows, :]
    q_in = (q * jnp.exp(g - g_mid)).astype(BF16)
    k_in = (k * jnp.exp(g_mid - g)).astype(BF16)
    q_x = (q * jnp.exp(g)).astype(BF16)
    k_end = (k * jnp.exp(g_edge - g)).astype(BF16)
    decay = jnp.exp(g_edge)
    yield
    att = _dot_nt(q_in, k_in)
    s_in = state["s"]
    state["s"] = s_in * decay + _dot_tn(v, k_end)
    yield
    att = jnp.where(tri, att, 0.0).astype(BF16)
    o = _dot(att, v) + _dot_nt(q_x, s_in.astype(BF16))
    yield
    if emit is None:
        o_ref[rows, :] = o
    else:
        emit(rows, o)


def _gla_kernel(q_ref, k_ref, v_ref, og_ref, gf_ref, gb_ref, gain_ref, y_ref,
                of_ref, ob_ref, sf_ref, sb_ref, *, n_chunks, n_ctx_chunks):
    c = GLA_CHUNK
    ri = lax.broadcasted_iota(jnp.int32, (c, c), 0)
    ci = lax.broadcasted_iota(jnp.int32, (c, c), 1)
    lower = ri >= ci
    upper = ci >= ri
    sf_ref[...] = jnp.zeros_like(sf_ref)
    sb_ref[...] = jnp.zeros_like(sb_ref)

    def finish(rows, o):
        y = o * lax.rsqrt(jnp.mean(o * o, axis=-1, keepdims=True) + RMS_EPS) * gain_ref[...]
        y_ref[0, rows, :] = (y * og_ref[0, rows, :].astype(F32)).astype(BF16)

    def scan(it, first, unroll, completing):
        fwd = {"s": sf_ref[...]}
        bwd = {"s": sb_ref[...]}
        emit_f = (lambda rows, o: finish(rows, o + ob_ref[rows, :])) if completing else None
        emit_b = (lambda rows, o: finish(rows, o + of_ref[rows, :])) if completing else None
        chains = []
        for u in range(unroll):
            s = first + it * unroll + u
            nb = jnp.where(s < n_ctx_chunks, n_ctx_chunks - 1 - s, n_chunks - 1 - (s - n_ctx_chunks))
            chains.append(_gla_chain(s, True, q_ref, k_ref, v_ref, gf_ref, of_ref, fwd, lower, emit_f))
            chains.append(_gla_chain(nb, False, q_ref, k_ref, v_ref, gb_ref, ob_ref, bwd, upper, emit_b))
        while chains:
            alive = []
            for ch in chains:
                try:
                    next(ch)
                    alive.append(ch)
                except StopIteration:
                    pass
            chains = alive
        sf_ref[...] = fwd["s"]
        sb_ref[...] = bwd["s"]

    half = (n_chunks - n_ctx_chunks) // 2
    scan(0, 0, n_ctx_chunks, False)
    for n in range(n_ctx_chunks):
        rows = pl.ds(n * c, c)
        finish(rows, of_ref[rows, :] + ob_ref[rows, :])
    lax.fori_loop(0, half // GLA_UNROLL,
                  lambda it, carry: scan(it, n_ctx_chunks, GLA_UNROLL, False) or carry, 0)
    lax.fori_loop(0, half // GLA_UNROLL,
                  lambda it, carry: scan(it, n_ctx_chunks + half, GLA_UNROLL, True) or carry, 0)


def _gla(q, k, v, og, gf, gb, lw, ctx_len):
    bsz, t, _ = q.shape
    assert (t - ctx_len) % (2 * GLA_UNROLL * GLA_CHUNK) == 0 and ctx_len % GLA_CHUNK == 0
    seq_spec = lambda w: pl.BlockSpec((1, t, w), lambda b, h: (b, 0, h))
    kern = functools.partial(_gla_kernel, n_chunks=t // GLA_CHUNK, n_ctx_chunks=ctx_len // GLA_CHUNK)
    return pl.pallas_call(
        kern, grid=(bsz, GLA_HEADS),
        in_specs=[seq_spec(GLA_DK), seq_spec(GLA_DK), seq_spec(GLA_DV), seq_spec(GLA_DV),
                  seq_spec(GLA_DK), seq_spec(GLA_DK), pl.BlockSpec((1, GLA_DV), lambda b, h: (0, 0))],
        out_specs=seq_spec(GLA_DV),
        out_shape=jax.ShapeDtypeStruct((bsz, t, GLA_HEADS * GLA_DV), BF16),
        scratch_shapes=[pltpu.VMEM((t, GLA_DV), F32), pltpu.VMEM((t, GLA_DV), F32),
                        pltpu.VMEM((GLA_DV, GLA_DK), F32), pltpu.VMEM((GLA_DV, GLA_DK), F32)],
        compiler_params=_params(("parallel", "parallel")), name="gla",
    )(q, k, v, og, gf, gb, lw["gla_gain"])


NAT_GROUP = 8


def _nat_kernel(q_ref, k0_ref, k1_ref, k2_ref, kc_ref, v0_ref, v1_ref, v2_ref, vc_ref, bias_ref, o_ref, s_ref,
                *, off):
    is_ctx = pl.program_id(0) + off == 0
    lane = lax.broadcasted_iota(jnp.int32, (TM, LANES), 1)
    row = lax.broadcasted_iota(jnp.int32, (LANES, TM), 0)
    n_heads = 2 * NAT_GROUP
    n_lat = NAT_KBLOCKS * TM

    def run(lat_keys, lat_vals):
        kcats, vlats = {}, {}

        def scores(i):
            hp, e = divmod(i, 2)
            sl = slice(hp * LANES, (hp + 1) * LANES)
            if hp not in kcats:
                refs = lat_keys + [kc_ref]
                half = (len(refs) + 1) // 2
                kcats[hp] = [jnp.concatenate([r[0, :, sl] for r in part], axis=0)
                             for part in (refs[:half], refs[half:]) if part]
            q = q_ref[0, :, sl]
            qm = jnp.where((lane >= NAT_HD) if e else (lane < NAT_HD), q, jnp.zeros_like(q))
            at = 0
            for kpart in kcats[hp]:
                s_ref[i % 2, at:at + kpart.shape[0], :] = _dot_nt(kpart, qm)
                at += kpart.shape[0]

        scores(0)
        outs = []
        for i in range(n_heads):
            if i + 1 < n_heads:
                scores(i + 1)
            hp, e = divmod(i, 2)
            sl = slice(hp * LANES, (hp + 1) * LANES)
            n_l = n_lat if lat_keys else 0
            s_ctx = s_ref[i % 2, n_l:n_l + TM, :]
            m = jnp.max(s_ctx, axis=0, keepdims=True)
            if lat_keys:
                s_lat = s_ref[i % 2, :n_lat, :] + bias_ref[i, 0]
                m = jnp.maximum(m, jnp.max(s_lat, axis=0, keepdims=True))
            p_ctx = jnp.exp2(s_ctx - m)
            l = jnp.sum(p_ctx, axis=0, keepdims=True)
            o = _dot(vc_ref[0, sl, :], p_ctx.astype(BF16))
            if lat_keys:
                if hp not in vlats:
                    vlats[hp] = jnp.concatenate([r[0, sl, :] for r in lat_vals], axis=1)
                p_lat = jnp.exp2(s_lat - m)
                l = l + jnp.sum(p_lat, axis=0, keepdims=True)
                o = o + _dot(vlats[hp], p_lat.astype(BF16))
            outs.append(o * (1.0 / l))
            if e == 1:
                ot = jnp.where(row < NAT_HD, outs[-2], outs[-1])
                o_ref[0, :, sl] = ot.T.astype(BF16)

    @pl.when(is_ctx)
    def _():
        run([], [])

    @pl.when(jnp.logical_not(is_ctx))
    def _():
        run([k0_ref, k1_ref, k2_ref], [v0_ref, v1_ref, v2_ref])


def _nat(q, k, vt, bias, layer, need_ctx):
    bsz, t, n = q.shape
    off = 0 if need_ctx else 1
    nblk = t // TM - 1
    gw = NAT_GROUP * LANES
    ngroups = n // gw

    def kblock(qi, i):
        j = qi + off - 1
        return 1 + jnp.clip(j - 1, 0, nblk - NAT_KBLOCKS) + i

    def bias_class(qi):
        j = qi + off - 1
        return jnp.where(j <= 0, 0, jnp.where(j == nblk - 1, 2, 1))

    kspec = lambda i: pl.BlockSpec((1, TM, gw), lambda qi, hg, b: (b, kblock(qi, i), hg))
    vspec = lambda i: pl.BlockSpec((1, gw, TM), lambda qi, hg, b: (b, hg, kblock(qi, i)))
    return pl.pallas_call(
        functools.partial(_nat_kernel, off=off),
        grid=(t // TM - off, ngroups, bsz),
        in_specs=[pl.BlockSpec((1, TM, gw), lambda qi, hg, b: (b, qi + off, hg)),
                  kspec(0), kspec(1), kspec(2),
                  pl.BlockSpec((1, TM, gw), lambda qi, hg, b: (b, 0, hg)),
                  vspec(0), vspec(1), vspec(2),
                  pl.BlockSpec((1, gw, TM), lambda qi, hg, b: (b, hg, 0)),
                  pl.BlockSpec((2 * NAT_GROUP, 1, NAT_KBLOCKS * TM, TM),
                               lambda qi, hg, b: (layer * ngroups + hg, bias_class(qi), 0, 0))],
        out_specs=pl.BlockSpec((1, TM, gw), lambda qi, hg, b: (b, qi + off, hg)),
        out_shape=jax.ShapeDtypeStruct((bsz, t, n), BF16),
        scratch_shapes=[pltpu.VMEM((2, (NAT_KBLOCKS + 1) * TM, TM), F32)],
        compiler_params=_params(("parallel", "parallel", "parallel")), name="nat",
    )(q, k, k, k, k, vt, vt, vt, vt, bias)


def _nat_bias_tables(rpb, rows):
    nblk = rows // NAT_QROWS
    kr_win = min(WIN_R, rows)
    col = np.arange(GRID_W)
    c_start = np.clip(col - WIN_C // 2, 0, GRID_W - WIN_C)
    cmask = (col[:, None] >= c_start[None, :]) & (col[:, None] < c_start[None, :] + WIN_C)
    dc = np.clip(col[:, None] - col[None, :] + WIN_C - 1, 0, 2 * WIN_C - 2)
    drs, rmasks = [], []
    for jblk in (0, 1, nblk - 1):
        base = int(np.clip(jblk - 1, 0, nblk - NAT_KBLOCKS))
        r = NAT_QROWS * jblk + np.arange(NAT_QROWS)
        r_start = np.clip(r - kr_win // 2, 0, rows - kr_win)
        kr = NAT_QROWS * base + np.arange(NAT_KBLOCKS * NAT_QROWS)
        rmasks.append((kr[:, None] >= r_start[None, :]) & (kr[:, None] < r_start[None, :] + kr_win))
        drs.append(np.clip(kr[:, None] - r[None, :] + WIN_R - 1, 0, 2 * WIN_R - 2))
    drx = np.where(np.stack(rmasks), np.stack(drs), 2 * WIN_R - 1)
    dcx = np.where(cmask, dc, 2 * WIN_C - 1)
    nh = rpb.shape[0]
    ext = jnp.full((nh, 2 * WIN_R, 2 * WIN_C), NEG_INF, F32)
    ext = ext.at[:, :2 * WIN_R - 1, :2 * WIN_C - 1].set(rpb.astype(F32) * LOG2E)
    pick = np.tile(dcx[:, None, :], (1, NAT_QROWS, 1)).reshape(-1)
    onehot = jnp.asarray(np.arange(2 * WIN_C)[:, None] == pick[None, :], F32)
    a = jnp.einsum("hdj,jn->hdn", ext, onehot, precision=lax.Precision.HIGHEST)
    a = a.reshape(nh, 2 * WIN_R, GRID_W, NAT_QROWS * GRID_W)
    nk = NAT_KBLOCKS * NAT_QROWS * GRID_W
    return pl.pallas_call(
        functools.partial(_bias_kernel, drx=drx),
        grid=(nh,),
        in_specs=[pl.BlockSpec((1,) + a.shape[1:], lambda h: (h, 0, 0, 0))],
        out_specs=pl.BlockSpec((1, 3, nk, TM), lambda h: (h, 0, 0, 0)),
        out_shape=jax.ShapeDtypeStruct((nh, 3, nk, TM), F32),
        compiler_params=_params(("parallel",)), name="nat_bias",
    )(a)


def _bias_kernel(a_ref, o_ref, *, drx):
    lane_blk = lax.broadcasted_iota(jnp.int32, (GRID_W, TM), 1) // GRID_W
    n_cls, n_kri, n_ri = drx.shape
    for cls in range(n_cls):
        for kri in range(n_kri):
            strip = a_ref[0, int(drx[cls, kri, n_ri - 1])]
            for ri in range(n_ri - 1):
                strip = jnp.where(lane_blk == ri, a_ref[0, int(drx[cls, kri, ri])], strip)
            o_ref[0, cls, kri * GRID_W:(kri + 1) * GRID_W, :] = strip


def _gqa_kernel(q_ref, k_ref, vt_ref, o_ref, s_ref, acc_ref, *, off, n_lat):
    is_ctx = pl.program_id(1) + off == 0
    group = GQA_HEADS // GQA_KV_HEADS

    def run(n_blocks):
        qs = {}

        def scores(h, start, size, slot):
            if h not in qs:
                heads = range(h * group, (h + 1) * group)
                qs[h] = jnp.concatenate([q_ref[0, :, g * GQA_HD:(g + 1) * GQA_HD] for g in heads], axis=0)
            s_ref[h, slot, :size, :] = _dot_nt(k_ref[0, start:start + size, h * GQA_HD:(h + 1) * GQA_HD], qs[h])

        def start_head(h):
            scores(h, 0, TM, 0)
            if n_blocks:
                scores(h, TM, GQA_KB, 1)

        start_head(0)
        for h in range(GQA_KV_HEADS):
            rows = slice(h * GQA_HD, (h + 1) * GQA_HD)
            if n_blocks == 0 and h + 1 < GQA_KV_HEADS:
                start_head(h + 1)
            st = s_ref[h, 0, :TM, :]
            m = jnp.max(st, axis=0, keepdims=True)
            p = jnp.exp2(st - m)
            l = jnp.sum(p, axis=0, keepdims=True)
            acc_ref[h] = _dot(vt_ref[0, rows, :TM], p.astype(BF16))
            for j in range(n_blocks):
                start = TM + j * GQA_KB
                if j + 1 < n_blocks:
                    scores(h, start + GQA_KB, GQA_KB, j % 2)
                elif h + 1 < GQA_KV_HEADS:
                    start_head(h + 1)
                st = s_ref[h, (j + 1) % 2]
                m_new = jnp.maximum(m, jnp.max(st, axis=0, keepdims=True))
                alpha = jnp.exp2(m - m_new)
                p = jnp.exp2(st - m_new)
                l = alpha * l + jnp.sum(p, axis=0, keepdims=True)
                acc_ref[h] = alpha * acc_ref[h] + _dot(vt_ref[0, rows, start:start + GQA_KB], p.astype(BF16))
                m = m_new
            ot = acc_ref[h] * (1.0 / l)
            for g in range(group):
                cols = slice((h * group + g) * GQA_HD, (h * group + g + 1) * GQA_HD)
                o_ref[0, :, cols] = ot[:, g * TM:(g + 1) * TM].T.astype(BF16)

    @pl.when(is_ctx)
    def _():
        run(0)

    @pl.when(jnp.logical_not(is_ctx))
    def _():
        run(n_lat)


def _gqa(q, k, vt, need_ctx):
    bsz, t, nq = q.shape
    nkv = k.shape[2]
    off = 0 if need_ctx else 1
    group = GQA_HEADS // GQA_KV_HEADS
    assert (t - TM) % GQA_KB == 0
    return pl.pallas_call(
        functools.partial(_gqa_kernel, off=off, n_lat=(t - TM) // GQA_KB),
        grid=(bsz, t // TM - off),
        in_specs=[pl.BlockSpec((1, TM, nq), lambda b, qi: (b, qi + off, 0)),
                  pl.BlockSpec((1, t, nkv), lambda b, qi: (b, 0, 0)),
                  pl.BlockSpec((1, nkv, t), lambda b, qi: (b, 0, 0))],
        out_specs=pl.BlockSpec((1, TM, nq), lambda b, qi: (b, qi + off, 0)),
        out_shape=jax.ShapeDtypeStruct((bsz, t, nq), BF16),
        scratch_shapes=[pltpu.VMEM((GQA_KV_HEADS, 2, GQA_KB, group * TM), F32),
                        pltpu.VMEM((GQA_KV_HEADS, GQA_HD, group * TM), F32)],
        compiler_params=_params(("parallel", "arbitrary")), name="gqa",
    )(q, k, vt)


def _merge_kernel(x_ref, gt_ref, y0_ref, y1_ref, y2_ref, sg_ref, wb_ref, wo_ref, o_ref):
    z = None
    for i, y_ref in enumerate((y0_ref, y1_ref, y2_ref)):
        zi = sg_ref[0, :, i * D_MODEL:(i + 1) * D_MODEL].astype(F32) * _dot(y_ref[0], wb_ref[i])
        z = zi if z is None else z + zi
    o_ref[0] = x_ref[0] + gt_ref[0] * _dot(z.astype(BF16), wo_ref[...])


def _merge(xall, mod, ys, sg, wb, wo, off):
    bsz, t, _ = xall.shape
    ctx_row = mod.shape[0] - 1
    return pl.pallas_call(
        _merge_kernel,
        grid=(bsz, t // TM - off),
        in_specs=[_row_spec(off, D_MODEL), _mod_spec(5, off, ctx_row),
                  _row_spec(off, D_MODEL), _row_spec(off, D_MODEL), _row_spec(off, D_MODEL),
                  _row_spec(off, 3 * D_MODEL), _whole(wb, 2), _whole(wo, 2)],
        out_specs=_row_spec(off, D_MODEL),
        out_shape=jax.ShapeDtypeStruct((bsz, t, D_MODEL), F32),
        compiler_params=_params(("parallel", "parallel")), name="merge",
    )(xall, mod, ys[0], ys[1], ys[2], sg, wb, wo)


def _block_diag_ones(head_dim):
    i = np.arange(MXU)
    return jnp.asarray((i[:, None] // head_dim) == (i[None, :] // head_dim), dtype=BF16)


def _chunk_tri(upper):
    i = np.arange(TM)
    same = (i[:, None] // GLA_CHUNK) == (i[None, :] // GLA_CHUNK)
    tri = (i[None, :] >= i[:, None]) if upper else (i[:, None] >= i[None, :])
    return jnp.asarray(same & tri, dtype=BF16)


def _rope_tables(ctx_len, seq):
    quarter = GQA_HD // 4
    freqs = ROPE_BASE ** (-np.arange(quarter, dtype=np.float64) / quarter)
    tok = np.arange(seq)
    ang_r = (tok // GRID_W)[:, None] * freqs
    ang_c = (tok % GRID_W)[:, None] * freqs
    ang = np.concatenate([ang_r, ang_r, ang_c, ang_c], axis=1)
    sign = np.tile(np.concatenate([-np.ones(quarter), np.ones(quarter)]), 2)
    cos = np.concatenate([np.ones((ctx_len, GQA_HD)), np.cos(ang)], axis=0)
    sin = np.concatenate([np.zeros((ctx_len, GQA_HD)), np.sin(ang) * sign], axis=0)
    return jnp.asarray(cos, F32), jnp.asarray(sin, F32)


def _layer_weights(l, ffn_w_in, ffn_w_out, w_in, gla_fg_w2, gla_fg_b, gla_norm_g, nat_q_norm, nat_k_norm,
                   gqa_q_norm, gqa_k_norm, w_branch, w_out):
    w = w_in[l]
    nqk = GLA_HEADS * GLA_DK
    nv = GLA_HEADS * GLA_DV
    o_fg = 2 * nqk + 2 * nv
    o_nat = o_fg + 2 * GLA_LR
    n = NAT_HEADS * NAT_HD
    o_gqa = o_nat + 3 * n
    nq = GQA_HEADS * GQA_HD
    nkv = GQA_KV_HEADS * GQA_HD
    o_gate = o_gqa + nq + 2 * nkv
    pad = jnp.zeros((D_MODEL, LANES - 2 * GLA_LR), w.dtype)
    w2 = gla_fg_w2[l]
    zeros_lr = jnp.zeros((GLA_LR, nqk), w2.dtype)
    zeros_rest = jnp.zeros((LANES - 2 * GLA_LR, nqk), w2.dtype)
    return {
        "ffn": [(ffn_w_in[l, i, :, :D_FF].astype(BF16), ffn_w_in[l, i, :, D_FF:].astype(BF16),
                 ffn_w_out[l, i].astype(BF16)) for i in range(2)],
        "w_gla": w[:, :o_fg].astype(BF16),
        "w_fg": jnp.concatenate([w[:, o_fg:o_nat], pad], axis=1).astype(BF16),
        "w_nat_qk": w[:, o_nat:o_nat + 2 * n].astype(BF16),
        "w_nat_vt": w[:, o_nat + 2 * n:o_gqa].T.astype(BF16),
        "w_gqa_qk": w[:, o_gqa:o_gqa + nq + nkv].astype(BF16),
        "w_gqa_vt": w[:, o_gqa + nq + nkv:o_gate].T.astype(BF16),
        "w_gates": w[:, o_gate:].astype(BF16),
        "w2": jnp.concatenate([jnp.concatenate([w2[0], zeros_lr, zeros_rest], axis=0),
                               jnp.concatenate([zeros_lr, w2[1], zeros_rest], axis=0)], axis=1).astype(BF16),
        "b2": gla_fg_b[l].reshape(1, 2 * nqk),
        "tril": _chunk_tri(False),
        "triu": _chunk_tri(True),
        "gla_gain": gla_norm_g[l].reshape(1, GLA_DV),
        "nat_gq": jnp.tile(nat_q_norm[l], NAT_HEADS).reshape(1, n),
        "nat_gk": jnp.tile(nat_k_norm[l], NAT_HEADS).reshape(1, n),
        "gqa_gq": jnp.tile(gqa_q_norm[l], GQA_HEADS).reshape(1, nq),
        "gqa_gk": jnp.tile(gqa_k_norm[l], GQA_KV_HEADS).reshape(1, nkv),
        "ones64": _block_diag_ones(NAT_HD),
        "ones128": _block_diag_ones(GQA_HD),
        "w_branch": w_branch[l].astype(BF16),
        "w_out": w_out[l].astype(BF16),
    }


def kernel(x, c, ctx, c_ctx, w_mod, b_mod, norm_g, ffn_w_in, ffn_w_out, w_in, gla_fg_w2, gla_fg_b, gla_norm_g,
           nat_q_norm, nat_k_norm, nat_rpb, gqa_q_norm, gqa_k_norm, w_branch, w_out):
    bsz, seq, _ = x.shape
    ctx_len = ctx.shape[1]
    depth = w_mod.shape[0]
    assert seq % TM == 0 and ctx_len == TM and seq // TM >= NAT_KBLOCKS

    mod_rows = -(-(bsz + 1) // 8) * 8
    cvec = jnp.concatenate([c, c_ctx[None, :], jnp.zeros((mod_rows - bsz - 1, D_MODEL), c.dtype)], axis=0)
    mods = _mod_table(cvec, w_mod, b_mod)[:, :bsz + 1].reshape(depth, bsz + 1, 1, N_MOD * D_MODEL)
    rope_cos, rope_sin = _rope_tables(ctx_len, seq)
    nat_bias = _nat_bias_tables(nat_rpb.reshape((-1,) + nat_rpb.shape[2:]), seq // GRID_W)

    xall = x
    t = ctx_len + seq
    for l in range(depth):
        need_ctx = l < depth - 1
        off = 0 if need_ctx else 1
        lw = _layer_weights(l, ffn_w_in, ffn_w_out, w_in, gla_fg_w2, gla_fg_b, gla_norm_g, nat_q_norm,
                            nat_k_norm, gqa_q_norm, gqa_k_norm, w_branch, w_out)
        mod = mods[l]
        g = [norm_g[l, i].reshape(1, D_MODEL) for i in range(3)]
        xall = _ffn(xall, mod, g[0], *lw["ffn"][0], sub=0, off=0, out_off=0, out_tokens=t,
                    ctx=ctx if l == 0 else None)
        (gq, gk, gv, gog, ggf, ggb), (nq_, nk_, nvt), (aq, ak, avt, sg) = _projections(
            xall, mod, g[1], lw, rope_cos, rope_sin)
        y_gla = _gla(gq, gk, gv, gog, ggf, ggb, lw, ctx_len)
        y_nat = _nat(nq_, nk_, nvt, nat_bias, l, need_ctx)
        y_gqa = _gqa(aq, ak, avt, need_ctx)
        xall = _merge(xall, mod, (y_gla, y_nat, y_gqa), sg, lw["w_branch"], lw["w_out"], off)
        if need_ctx:
            xall = _ffn(xall, mod, g[2], *lw["ffn"][1], sub=2, off=0, out_off=0, out_tokens=t)
        else:
            xall = _ffn(xall, mod, g[2], *lw["ffn"][1], sub=2, off=1, out_off=0, out_tokens=seq)
    return xall
```

```python
import functools

import numpy as np
import jax
import jax.numpy as jnp
from jax import lax
from jax.experimental import pallas as pl
from jax.experimental.pallas import tpu as pltpu

F32 = jnp.float32
BF16 = jnp.bfloat16

D_MODEL = 1024
GRID_W = 64
N_MOD = 9
D_FF = 2816
RMS_EPS = 1e-6
NEG_INF = -1e30

GLA_HEADS = 4
GLA_DK = 128
GLA_DV = 256
GLA_LR = 16
GLA_TAU = 16.0
GLA_CHUNK = 128

NAT_HEADS = 16
NAT_HD = 64
WIN_R = 8
WIN_C = 16
NAT_QROWS = 4
NAT_KBLOCKS = 3

GQA_HEADS = 8
GQA_KV_HEADS = 2
GQA_HD = 128
ROPE_BASE = 10000.0
GQA_KB = 512
LOG2E = 1.4426950408889634

TM = 256
LANES = 128
SUBLANES = 8
MXU = 256
VMEM_LIMIT = 56 * 1024 * 1024


def _dot(a, b):
    return jnp.dot(a, b, preferred_element_type=F32)


def _dot_nt(a, b):
    return lax.dot_general(a, b, (((1,), (1,)), ((), ())), preferred_element_type=F32)


def _dot_tn(a, b):
    return lax.dot_general(a, b, (((0,), (0,)), ((), ())), preferred_element_type=F32)


def _sigmoid(x):
    return 1.0 / (1.0 + jnp.exp(-x))


def _log_sigmoid(x):
    return -(jnp.maximum(-x, 0.0) + jnp.log(1.0 + jnp.exp(-jnp.abs(x))))


def _norm_mod(x, g, shift, scale):
    y = x * lax.rsqrt(jnp.mean(x * x, axis=-1, keepdims=True) + RMS_EPS) * g
    return y * (1.0 + scale) + shift


def _head_norm(t, gain, ones_bd, head_dim):
    sq = (t * t).astype(BF16)
    parts = [_dot(sq[:, i:i + MXU], ones_bd) for i in range(0, t.shape[1], MXU)]
    ss = parts[0] if len(parts) == 1 else jnp.concatenate(parts, axis=1)
    return t * lax.rsqrt(ss * (1.0 / head_dim) + RMS_EPS) * gain


def _params(semantics):
    return pltpu.CompilerParams(dimension_semantics=semantics, vmem_limit_bytes=VMEM_LIMIT)


def _whole(arr):
    zeros = (0,) * arr.ndim
    return pl.BlockSpec(arr.shape, lambda *_: zeros, pipeline_mode=pl.Buffered(1))


def _row_spec(off, width):
    return pl.BlockSpec((1, TM, width), lambda b, j: (b, j + off, 0))


def _mod_spec(col, off, ctx_row):
    return pl.BlockSpec((1, 1, D_MODEL), lambda b, j: (jnp.where(j + off == 0, ctx_row, b), 0, col))


def _mod_kernel(c_ref, w_ref, b_ref, o_ref):
    c = c_ref[...]
    s = (c * _sigmoid(c)).astype(BF16)
    o_ref[0] = _dot(s, w_ref[0].astype(BF16)) + b_ref[0]


def _mod_table(cvec, w_mod, b_mod):
    depth = w_mod.shape[0]
    rows = cvec.shape[0]
    width = 3 * D_MODEL
    return pl.pallas_call(
        _mod_kernel,
        grid=(depth, N_MOD * D_MODEL // width),
        in_specs=[pl.BlockSpec((rows, D_MODEL), lambda l, n: (0, 0)),
                  pl.BlockSpec((1, D_MODEL, width), lambda l, n: (l, 0, n)),
                  pl.BlockSpec((1, 1, width), lambda l, n: (l, 0, n))],
        out_specs=pl.BlockSpec((1, rows, width), lambda l, n: (l, 0, n)),
        out_shape=jax.ShapeDtypeStruct((depth, rows, N_MOD * D_MODEL), F32),
        compiler_params=_params(("arbitrary", "arbitrary")),
        name="mod_table",
    )(cvec, w_mod, b_mod.reshape(depth, 1, N_MOD * D_MODEL))


def _next_tile(b, j, nb, nj):
    wrap = j + 1 == nj
    last = jnp.logical_and(wrap, b + 1 == nb)
    return jnp.where(jnp.logical_and(wrap, jnp.logical_not(last)), b + 1, b), \
        jnp.where(wrap, jnp.where(last, j, 0), j + 1)


def _ffn_kernel(x_ref, ctx_ref, xn_ref, ctxn_ref, sh_ref, sc_ref, gt_ref, shn_ref, scn_ref, g_ref,
                wa_ref, wb_ref, wd_ref, o_ref, h_ref, *, split_input):
    b, j = pl.program_id(0), pl.program_id(1)

    def tile(x_r, ctx_r, jj):
        return jnp.where(jj == 0, ctx_r[0], x_r[0]) if split_input else x_r[0]

    def hidden(xt, shift, scale):
        u = _norm_mod(xt, g_ref[...], shift, scale).astype(BF16)
        a = _dot(u, wa_ref[...])
        return (a * _sigmoid(a) * _dot(u, wb_ref[...])).astype(BF16)

    @pl.when(jnp.logical_and(b == 0, j == 0))
    def _():
        h_ref[...] = hidden(tile(x_ref, ctx_ref, j), sh_ref[0], sc_ref[0])

    o_ref[0] = tile(x_ref, ctx_ref, j) + (0.5 * gt_ref[0]) * _dot(h_ref[...], wd_ref[...])
    _, jn = _next_tile(b, j, pl.num_programs(0), pl.num_programs(1))
    h_ref[...] = hidden(tile(xn_ref, ctxn_ref, jn), shn_ref[0], scn_ref[0])


def _ffn(xall, mod, g, wa, wb, wd, sub, tile_off, ctx=None):
    bsz, t, _ = xall.shape
    split = ctx is not None
    if split:
        t += ctx.shape[1]
    nj = t // TM
    ctx_row = mod.shape[0] - 1
    cur = lambda b, j: (b, j)
    nxt = lambda b, j: _next_tile(b, j, bsz, nj)

    def specs(at):
        if split:
            x_spec = pl.BlockSpec((1, TM, D_MODEL), lambda b, j: (at(b, j)[0], jnp.maximum(at(b, j)[1] - 1, 0), 0))
            ctx_spec = pl.BlockSpec((1, TM, D_MODEL), lambda b, j: (at(b, j)[0], 0, 0))
        else:
            x_spec = pl.BlockSpec((1, TM, D_MODEL), lambda b, j: (at(b, j)[0], at(b, j)[1], 0))
            ctx_spec = pl.BlockSpec((1, SUBLANES, D_MODEL), lambda b, j: (at(b, j)[0], 0, 0))
        return [x_spec, ctx_spec]

    def mod_spec(col, at):
        return pl.BlockSpec((1, 1, D_MODEL),
                            lambda b, j: (jnp.where(at(b, j)[1] + tile_off == 0, ctx_row, at(b, j)[0]), 0, col))

    if not split:
        ctx = xall
    return pl.pallas_call(
        functools.partial(_ffn_kernel, split_input=split),
        grid=(bsz, nj),
        in_specs=specs(cur) + specs(nxt) +
                 [mod_spec(3 * sub, cur), mod_spec(3 * sub + 1, cur), mod_spec(3 * sub + 2, cur),
                  mod_spec(3 * sub, nxt), mod_spec(3 * sub + 1, nxt),
                  _whole(g), _whole(wa), _whole(wb), _whole(wd)],
        out_specs=_row_spec(0, D_MODEL),
        out_shape=jax.ShapeDtypeStruct((bsz, t, D_MODEL), F32),
        scratch_shapes=[pltpu.VMEM((TM, wd.shape[0]), BF16)],
        compiler_params=_params(("arbitrary", "arbitrary")),
        name="ffn",
    )(xall, ctx, xall, ctx, mod, mod, mod, mod, mod, g, wa, wb, wd)


def _chunk_cumsum(tri, la):
    hi = la.astype(BF16)
    lo = (la - hi.astype(F32)).astype(BF16)
    return _dot(tri, hi) + _dot(tri, lo)


def _proj_gla_part(u, w_ref, wfg_ref, w2_ref, b2_ref, tril_ref, triu_ref,
                   q_ref, k_ref, v_ref, og_ref, gf_ref, gb_ref):
    nqk = GLA_HEADS * GLA_DK
    nv = GLA_HEADS * GLA_DV
    fg = _dot(u, wfg_ref[...]).astype(BF16)
    q_ref[0] = (_dot(u, w_ref[:, :nqk]) * GLA_DK ** -0.5).astype(BF16)
    z = _dot(fg, w2_ref[...]) + b2_ref[...]
    k_ref[0] = _dot(u, w_ref[:, nqk:2 * nqk]).astype(BF16)
    la = _log_sigmoid(z) * (1.0 / GLA_TAU)
    v_ref[0] = _dot(u, w_ref[:, 2 * nqk:2 * nqk + nv]).astype(BF16)
    gf_ref[0] = _chunk_cumsum(tril_ref[...], la[:, :nqk])
    gb_ref[0] = _chunk_cumsum(triu_ref[...], la[:, nqk:])
    og = _dot(u, w_ref[:, 2 * nqk + nv:])
    og_ref[0] = (og * _sigmoid(og)).astype(BF16)


def _proj_nat_part(u, wqk_ref, wvt_ref, gq_ref, gk_ref, ones_ref, q_ref, k_ref, vt_ref):
    p = _dot(u, wqk_ref[...])
    n = NAT_HEADS * NAT_HD
    ones_bd = ones_ref[...]
    q_ref[0] = (_head_norm(p[:, :n], gq_ref[...], ones_bd, NAT_HD) * (NAT_HD ** -0.5 * LOG2E)).astype(BF16)
    k_ref[0] = _head_norm(p[:, n:], gk_ref[...], ones_bd, NAT_HD).astype(BF16)
    vt_ref[0] = _dot_nt(wvt_ref[...], u).astype(BF16)


def _rope(t, cos, sin, lane):
    partner = jnp.where(lane % (GQA_HD // 2) < GQA_HD // 4,
                        pltpu.roll(t, GQA_HD - GQA_HD // 4, axis=1), pltpu.roll(t, GQA_HD // 4, axis=1))
    return t * cos + partner * sin


def _proj_gqa_part(u, wqk_ref, wvt_ref, wg_ref, gq_ref, gk_ref, ones_ref, cos_ref, sin_ref,
                   q_ref, k_ref, vt_ref, sg_ref):
    p = _dot(u, wqk_ref[...])
    nq = GQA_HEADS * GQA_HD
    ones_bd = ones_ref[...]
    qn = _head_norm(p[:, :nq], gq_ref[...], ones_bd, GQA_HD)
    kn = _head_norm(p[:, nq:], gk_ref[...], ones_bd, GQA_HD)
    cos = cos_ref[...]
    sin = sin_ref[...]
    lane = lax.broadcasted_iota(jnp.int32, (TM, GQA_HD), 1)
    for h in range(GQA_HEADS):
        sl = slice(h * GQA_HD, (h + 1) * GQA_HD)
        q_ref[0, :, sl] = (_rope(qn[:, sl], cos, sin, lane) * (GQA_HD ** -0.5 * LOG2E)).astype(BF16)
    for h in range(GQA_KV_HEADS):
        sl = slice(h * GQA_HD, (h + 1) * GQA_HD)
        k_ref[0, :, sl] = _rope(kn[:, sl], cos, sin, lane).astype(BF16)
    vt_ref[0] = _dot_nt(wvt_ref[...], u).astype(BF16)
    sg_ref[0] = _sigmoid(_dot(u, wg_ref[...])).astype(BF16)


N_PROJ_GLA_IN, N_PROJ_NAT_IN, N_PROJ_GQA_IN = 6, 5, 8
N_PROJ_GLA_OUT, N_PROJ_NAT_OUT, N_PROJ_GQA_OUT = 6, 3, 4


def _proj_kernel(*refs):
    it = iter(refs)
    take = lambda count: [next(it) for _ in range(count)]
    x_ref, sh_ref, sc_ref, g_ref = take(4)
    gla_in, nat_in, gqa_in = take(N_PROJ_GLA_IN), take(N_PROJ_NAT_IN), take(N_PROJ_GQA_IN)
    gla_out, nat_out, gqa_out = take(N_PROJ_GLA_OUT), take(N_PROJ_NAT_OUT), take(N_PROJ_GQA_OUT)
    u_ref, = take(1)

    @pl.when(pl.program_id(0) == 0)
    def _():
        u_ref[...] = jnp.zeros_like(u_ref)

    u_prev = u_ref[...]
    _proj_nat_part(u_prev, *nat_in, *nat_out)
    u = _norm_mod(x_ref[0], g_ref[...], sh_ref[0], sc_ref[0]).astype(BF16)
    _proj_gqa_part(u_prev, *gqa_in, *gqa_out)
    _proj_gla_part(u, *gla_in, *gla_out)
    u_ref[...] = u


def _projections(xall, mod, g, lw, rope_cos, rope_sin):
    bsz, t, _ = xall.shape
    ctx_row = mod.shape[0] - 1
    nt = t // TM
    n_tiles = bsz * nt
    lead = lambda s: jnp.minimum(s, n_tiles - 1)
    lag = lambda s: jnp.maximum(s - 1, 0)

    def rows(at, width):
        return pl.BlockSpec((1, TM, width), lambda s: (at(s) // nt, at(s) % nt, 0))

    def cols(at, height):
        return pl.BlockSpec((1, height, TM), lambda s: (at(s) // nt, 0, at(s) % nt))

    def mod_spec(col):
        return pl.BlockSpec((1, 1, D_MODEL),
                            lambda s: (jnp.where(lead(s) % nt == 0, ctx_row, lead(s) // nt), 0, col))

    sds = jax.ShapeDtypeStruct
    nqk = GLA_HEADS * GLA_DK
    nv = GLA_HEADS * GLA_DV
    n = NAT_HEADS * NAT_HD
    nq = GQA_HEADS * GQA_HD
    nkv = GQA_KV_HEADS * GQA_HD
    tab = pl.BlockSpec((TM, GQA_HD), lambda s: (lag(s) % nt, 0))
    gla_in = [lw["w_gla"], lw["w_fg"], lw["w2"], lw["b2"], lw["tril"], lw["triu"]]
    nat_in = [lw["w_nat_qk"], lw["w_nat_vt"], lw["nat_gq"], lw["nat_gk"], lw["ones64"]]
    gqa_in = [lw["w_gqa_qk"], lw["w_gqa_vt"], lw["w_gates"], lw["gqa_gq"], lw["gqa_gk"], lw["ones128"]]
    assert (len(gla_in), len(nat_in), len(gqa_in) + 2) == (N_PROJ_GLA_IN, N_PROJ_NAT_IN, N_PROJ_GQA_IN)
    outs = pl.pallas_call(
        _proj_kernel, grid=(n_tiles + 1,),
        in_specs=[rows(lead, D_MODEL), mod_spec(3), mod_spec(4), _whole(g)]
                 + [_whole(a) for a in gla_in + nat_in + gqa_in] + [tab, tab],
        out_specs=[rows(lead, nqk), rows(lead, nqk), rows(lead, nv), rows(lead, nv), rows(lead, nqk), rows(lead, nqk),
                   rows(lag, n), rows(lag, n), cols(lag, n),
                   rows(lag, nq), rows(lag, nkv), cols(lag, nkv), rows(lag, 3 * D_MODEL)],
        out_shape=[sds((bsz, t, nqk), BF16), sds((bsz, t, nqk), BF16), sds((bsz, t, nv), BF16),
                   sds((bsz, t, nv), BF16), sds((bsz, t, nqk), F32), sds((bsz, t, nqk), F32),
                   sds((bsz, t, n), BF16), sds((bsz, t, n), BF16), sds((bsz, n, t), BF16),
                   sds((bsz, t, nq), BF16), sds((bsz, t, nkv), BF16), sds((bsz, nkv, t), BF16),
                   sds((bsz, t, 3 * D_MODEL), BF16)],
        scratch_shapes=[pltpu.VMEM((TM, D_MODEL), BF16)],
        compiler_params=_params(("arbitrary",)), name="proj",
    )(xall, mod, mod, g, *gla_in, *nat_in, *gqa_in, rope_cos, rope_sin)
    return outs[:6], outs[6:9], outs[9:]


GLA_UNROLL = 8


def _gla_chain(n, forward, q_ref, k_ref, v_ref, g_ref, o_ref, state, tri, emit):
    c = GLA_CHUNK
    rows = pl.ds(pl.multiple_of(jnp.int32(n) * c, c), c)
    g = g_ref[0, rows, :]
    if forward:
        g_mid = g[c // 2 - 1:c // 2, :]
        g_edge = g[c - 1:c, :]
    else:
        g_mid = g[c // 2:c // 2 + 1, :]
        g_edge = g[0:1, :]
    q = q_ref[0, rows, :].astype(F32)
    k = k_ref[0, rows, :].astype(F32)
    v = v_ref[0, rows, :]
    q_in = (q * jnp.exp(g - g_mid)).astype(BF16)
    k_in = (k * jnp.exp(g_mid - g)).astype(BF16)
    q_x = (q * jnp.exp(g)).astype(BF16)
    k_end = (k * jnp.exp(g_edge - g)).astype(BF16)
    decay = jnp.exp(g_edge)
    yield
    att = _dot_nt(q_in, k_in)
    s_in = state["s"]
    state["s"] = s_in * decay + _dot_tn(v, k_end)
    yield
    att = jnp.where(tri, att, 0.0).astype(BF16)
    o = _dot(att, v) + _dot_nt(q_x, s_in.astype(BF16))
    yield
    if emit is None:
        o_ref[rows, :] = o
    else:
        emit(rows, o)


def _gla_kernel(q_ref, k_ref, v_ref, og_ref, gf_ref, gb_ref, gain_ref, y_ref,
                of_ref, ob_ref, sf_ref, sb_ref, *, n_chunks, n_ctx_chunks):
    c = GLA_CHUNK
    ri = lax.broadcasted_iota(jnp.int32, (c, c), 0)
    ci = lax.broadcasted_iota(jnp.int32, (c, c), 1)
    lower = ri >= ci
    upper = ci >= ri
    sf_ref[...] = jnp.zeros_like(sf_ref)
    sb_ref[...] = jnp.zeros_like(sb_ref)

    def finish(rows, o):
        y = o * lax.rsqrt(jnp.mean(o * o, axis=-1, keepdims=True) + RMS_EPS) * gain_ref[...]
        y_ref[0, rows, :] = (y * og_ref[0, rows, :].astype(F32)).astype(BF16)

    def scan(it, first, unroll, completing):
        fwd = {"s": sf_ref[...]}
        bwd = {"s": sb_ref[...]}
        emit_f = (lambda rows, o: finish(rows, o + ob_ref[rows, :])) if completing else None
        emit_b = (lambda rows, o: finish(rows, o + of_ref[rows, :])) if completing else None
        chains = []
        for u in range(unroll):
            s = first + it * unroll + u
            nb = jnp.where(s < n_ctx_chunks, n_ctx_chunks - 1 - s, n_chunks - 1 - (s - n_ctx_chunks))
            chains.append(_gla_chain(s, True, q_ref, k_ref, v_ref, gf_ref, of_ref, fwd, lower, emit_f))
            chains.append(_gla_chain(nb, False, q_ref, k_ref, v_ref, gb_ref, ob_ref, bwd, upper, emit_b))
        while chains:
            alive = []
            for ch in chains:
                try:
                    next(ch)
                    alive.append(ch)
                except StopIteration:
                    pass
            chains = alive
        sf_ref[...] = fwd["s"]
        sb_ref[...] = bwd["s"]

    half = (n_chunks - n_ctx_chunks) // 2
    scan(0, 0, n_ctx_chunks, False)
    for n in range(n_ctx_chunks):
        rows = pl.ds(n * c, c)
        finish(rows, of_ref[rows, :] + ob_ref[rows, :])
    lax.fori_loop(0, half // GLA_UNROLL,
                  lambda it, carry: scan(it, n_ctx_chunks, GLA_UNROLL, False) or carry, 0)
    lax.fori_loop(0, half // GLA_UNROLL,
                  lambda it, carry: scan(it, n_ctx_chunks + half, GLA_UNROLL, True) or carry, 0)


def _gla(q, k, v, og, gf, gb, lw, ctx_len):
    bsz, t, _ = q.shape
    assert (t - ctx_len) % (2 * GLA_UNROLL * GLA_CHUNK) == 0 and ctx_len % GLA_CHUNK == 0
    seq_spec = lambda w: pl.BlockSpec((1, t, w), lambda b, h: (b, 0, h))
    kern = functools.partial(_gla_kernel, n_chunks=t // GLA_CHUNK, n_ctx_chunks=ctx_len // GLA_CHUNK)
    return pl.pallas_call(
        kern, grid=(bsz, GLA_HEADS),
        in_specs=[seq_spec(GLA_DK), seq_spec(GLA_DK), seq_spec(GLA_DV), seq_spec(GLA_DV),
                  seq_spec(GLA_DK), seq_spec(GLA_DK), pl.BlockSpec((1, GLA_DV), lambda b, h: (0, 0))],
        out_specs=seq_spec(GLA_DV),
        out_shape=jax.ShapeDtypeStruct((bsz, t, GLA_HEADS * GLA_DV), BF16),
        scratch_shapes=[pltpu.VMEM((t, GLA_DV), F32), pltpu.VMEM((t, GLA_DV), F32),
                        pltpu.VMEM((GLA_DV, GLA_DK), F32), pltpu.VMEM((GLA_DV, GLA_DK), F32)],
        compiler_params=_params(("parallel", "parallel")), name="gla",
    )(q, k, v, og, gf, gb, lw["gla_gain"])


NAT_GROUP = 8


def _nat_kernel(q_ref, k0_ref, k1_ref, k2_ref, kc_ref, v0_ref, v1_ref, v2_ref, vc_ref, bias_ref, o_ref, s_ref,
                *, off):
    is_ctx = pl.program_id(0) + off == 0
    lane = lax.broadcasted_iota(jnp.int32, (TM, LANES), 1)
    row = lax.broadcasted_iota(jnp.int32, (LANES, TM), 0)
    n_heads = 2 * NAT_GROUP
    n_lat = NAT_KBLOCKS * TM

    def run(lat_keys, lat_vals):
        kcats, vlats = {}, {}

        def scores(i):
            hp, e = divmod(i, 2)
            sl = slice(hp * LANES, (hp + 1) * LANES)
            if hp not in kcats:
                refs = lat_keys + [kc_ref]
                half = (len(refs) + 1) // 2
                kcats[hp] = [jnp.concatenate([r[0, :, sl] for r in part], axis=0)
                             for part in (refs[:half], refs[half:]) if part]
            q = q_ref[0, :, sl]
            qm = jnp.where((lane >= NAT_HD) if e else (lane < NAT_HD), q, jnp.zeros_like(q))
            at = 0
            for kpart in kcats[hp]:
                s_ref[i % 2, at:at + kpart.shape[0], :] = _dot_nt(kpart, qm)
                at += kpart.shape[0]

        scores(0)
        outs = []
        for i in range(n_heads):
            if i + 1 < n_heads:
                scores(i + 1)
            hp, e = divmod(i, 2)
            sl = slice(hp * LANES, (hp + 1) * LANES)
            n_l = n_lat if lat_keys else 0
            s_ctx = s_ref[i % 2, n_l:n_l + TM, :]
            m = jnp.max(s_ctx, axis=0, keepdims=True)
            if lat_keys:
                s_lat = s_ref[i % 2, :n_lat, :] + bias_ref[i, 0]
                m = jnp.maximum(m, jnp.max(s_lat, axis=0, keepdims=True))
            def own_rows(v):
                vrow = lax.broadcasted_iota(jnp.int32, v.shape, 0)
                return jnp.where((vrow >= NAT_HD) if e else (vrow < NAT_HD), v, jnp.ones_like(v))

            p_ctx = jnp.exp2(s_ctx - m)
            o = _dot(own_rows(vc_ref[0, sl, :]), p_ctx.astype(BF16))
            if lat_keys:
                if hp not in vlats:
                    vlats[hp] = jnp.concatenate([r[0, sl, :] for r in lat_vals], axis=1)
                p_lat = jnp.exp2(s_lat - m)
                o = o + _dot(own_rows(vlats[hp]), p_lat.astype(BF16))
            l = o[0:1, :] if e else o[LANES - 1:LANES, :]
            outs.append(o * (1.0 / l))
            if e == 1:
                ot = jnp.where(row < NAT_HD, outs[-2], outs[-1])
                o_ref[0, :, sl] = ot.T.astype(BF16)

    @pl.when(is_ctx)
    def _():
        run([], [])

    @pl.when(jnp.logical_not(is_ctx))
    def _():
        run([k0_ref, k1_ref, k2_ref], [v0_ref, v1_ref, v2_ref])


def _nat(q, k, vt, bias, layer, need_ctx):
    bsz, t, n = q.shape
    off = 0 if need_ctx else 1
    nblk = t // TM - 1
    gw = NAT_GROUP * LANES
    ngroups = n // gw

    def kblock(qi, i):
        j = qi + off - 1
        return 1 + jnp.clip(j - 1, 0, nblk - NAT_KBLOCKS) + i

    def bias_class(qi):
        j = qi + off - 1
        return jnp.where(j <= 0, 0, jnp.where(j == nblk - 1, 2, 1))

    kspec = lambda i: pl.BlockSpec((1, TM, gw), lambda qi, hg, b: (b, kblock(qi, i), hg))
    vspec = lambda i: pl.BlockSpec((1, gw, TM), lambda qi, hg, b: (b, hg, kblock(qi, i)))
    return pl.pallas_call(
        functools.partial(_nat_kernel, off=off),
        grid=(t // TM - off, ngroups, bsz),
        in_specs=[pl.BlockSpec((1, TM, gw), lambda qi, hg, b: (b, qi + off, hg)),
                  kspec(0), kspec(1), kspec(2),
                  pl.BlockSpec((1, TM, gw), lambda qi, hg, b: (b, 0, hg)),
                  vspec(0), vspec(1), vspec(2),
                  pl.BlockSpec((1, gw, TM), lambda qi, hg, b: (b, hg, 0)),
                  pl.BlockSpec((2 * NAT_GROUP, 1, NAT_KBLOCKS * TM, TM),
                               lambda qi, hg, b: (layer * ngroups + hg, bias_class(qi), 0, 0))],
        out_specs=pl.BlockSpec((1, TM, gw), lambda qi, hg, b: (b, qi, hg)),
        out_shape=jax.ShapeDtypeStruct((bsz, t - off * TM, n), BF16),
        scratch_shapes=[pltpu.VMEM((2, (NAT_KBLOCKS + 1) * TM, TM), F32)],
        compiler_params=_params(("parallel", "parallel", "parallel")), name="nat",
    )(q, k, k, k, k, vt, vt, vt, vt, bias)


def _nat_bias_tables(rpb, rows):
    nblk = rows // NAT_QROWS
    kr_win = min(WIN_R, rows)
    col = np.arange(GRID_W)
    c_start = np.clip(col - WIN_C // 2, 0, GRID_W - WIN_C)
    cmask = (col[:, None] >= c_start[None, :]) & (col[:, None] < c_start[None, :] + WIN_C)
    dc = np.clip(col[:, None] - col[None, :] + WIN_C - 1, 0, 2 * WIN_C - 2)
    drs, rmasks = [], []
    for jblk in (0, 1, nblk - 1):
        base = int(np.clip(jblk - 1, 0, nblk - NAT_KBLOCKS))
        r = NAT_QROWS * jblk + np.arange(NAT_QROWS)
        r_start = np.clip(r - kr_win // 2, 0, rows - kr_win)
        kr = NAT_QROWS * base + np.arange(NAT_KBLOCKS * NAT_QROWS)
        rmasks.append((kr[:, None] >= r_start[None, :]) & (kr[:, None] < r_start[None, :] + kr_win))
        drs.append(np.clip(kr[:, None] - r[None, :] + WIN_R - 1, 0, 2 * WIN_R - 2))
    drx = np.where(np.stack(rmasks), np.stack(drs), 2 * WIN_R - 1)
    dcx = np.where(cmask, dc, 2 * WIN_C - 1)
    nh = rpb.shape[0]
    ext = jnp.full((nh, 2 * WIN_R, 2 * WIN_C), NEG_INF, F32)
    ext = ext.at[:, :2 * WIN_R - 1, :2 * WIN_C - 1].set(rpb.astype(F32) * LOG2E)
    pick = np.tile(dcx[:, None, :], (1, NAT_QROWS, 1)).reshape(-1)
    onehot = jnp.asarray(np.arange(2 * WIN_C)[:, None] == pick[None, :], F32)
    a = jnp.einsum("hdj,jn->hdn", ext, onehot, precision=lax.Precision.HIGHEST)
    a = a.reshape(nh, 2 * WIN_R, GRID_W, NAT_QROWS * GRID_W)
    nk = NAT_KBLOCKS * NAT_QROWS * GRID_W
    return pl.pallas_call(
        functools.partial(_bias_kernel, drx=drx),
        grid=(nh,),
        in_specs=[pl.BlockSpec((1,) + a.shape[1:], lambda h: (h, 0, 0, 0))],
        out_specs=pl.BlockSpec((1, 3, nk, TM), lambda h: (h, 0, 0, 0)),
        out_shape=jax.ShapeDtypeStruct((nh, 3, nk, TM), F32),
        compiler_params=_params(("parallel",)), name="nat_bias",
    )(a)


def _bias_kernel(a_ref, o_ref, *, drx):
    lane_blk = lax.broadcasted_iota(jnp.int32, (GRID_W, TM), 1) // GRID_W
    n_cls, n_kri, n_ri = drx.shape
    for cls in range(n_cls):
        for kri in range(n_kri):
            strip = a_ref[0, int(drx[cls, kri, n_ri - 1])]
            for ri in range(n_ri - 1):
                strip = jnp.where(lane_blk == ri, a_ref[0, int(drx[cls, kri, ri])], strip)
            o_ref[0, cls, kri * GRID_W:(kri + 1) * GRID_W, :] = strip


def _gqa_kernel(q_ref, k_ref, vt_ref, o_ref, s_ref, acc_ref, *, off, n_lat):
    is_ctx = pl.program_id(1) + off == 0
    group = GQA_HEADS // GQA_KV_HEADS

    def run(n_blocks):
        qs = {}

        def scores(h, start, size, slot):
            if h not in qs:
                heads = range(h * group, (h + 1) * group)
                qs[h] = jnp.concatenate([q_ref[0, :, g * GQA_HD:(g + 1) * GQA_HD] for g in heads], axis=0)
            s_ref[h, slot, :size, :] = _dot_nt(k_ref[0, start:start + size, h * GQA_HD:(h + 1) * GQA_HD], qs[h])

        def start_head(h):
            scores(h, 0, TM, 0)
            if n_blocks:
                scores(h, TM, GQA_KB, 1)

        start_head(0)
        for h in range(GQA_KV_HEADS):
            rows = slice(h * GQA_HD, (h + 1) * GQA_HD)
            if n_blocks == 0 and h + 1 < GQA_KV_HEADS:
                start_head(h + 1)
            st = s_ref[h, 0, :TM, :]
            m = jnp.max(st, axis=0, keepdims=True)
            p = jnp.exp2(st - m)
            l = jnp.sum(p, axis=0, keepdims=True)
            acc_ref[h] = _dot(vt_ref[0, rows, :TM], p.astype(BF16))
            for j in range(n_blocks):
                start = TM + j * GQA_KB
                if j + 1 < n_blocks:
                    scores(h, start + GQA_KB, GQA_KB, j % 2)
                elif h + 1 < GQA_KV_HEADS:
                    start_head(h + 1)
                st = s_ref[h, (j + 1) % 2]
                m_new = jnp.maximum(m, jnp.max(st, axis=0, keepdims=True))
                alpha = jnp.exp2(m - m_new)
                p = jnp.exp2(st - m_new)
                l = alpha * l + jnp.sum(p, axis=0, keepdims=True)
                acc_ref[h] = alpha * acc_ref[h] + _dot(vt_ref[0, rows, start:start + GQA_KB], p.astype(BF16))
                m = m_new
            ot = acc_ref[h] * (1.0 / l)
            for g in range(group):
                cols = slice((h * group + g) * GQA_HD, (h * group + g + 1) * GQA_HD)
                o_ref[0, :, cols] = ot[:, g * TM:(g + 1) * TM].T.astype(BF16)

    @pl.when(is_ctx)
    def _():
        run(0)

    @pl.when(jnp.logical_not(is_ctx))
    def _():
        run(n_lat)


def _gqa(q, k, vt, need_ctx):
    bsz, t, nq = q.shape
    nkv = k.shape[2]
    off = 0 if need_ctx else 1
    group = GQA_HEADS // GQA_KV_HEADS
    assert (t - TM) % GQA_KB == 0
    return pl.pallas_call(
        functools.partial(_gqa_kernel, off=off, n_lat=(t - TM) // GQA_KB),
        grid=(bsz, t // TM - off),
        in_specs=[pl.BlockSpec((1, TM, nq), lambda b, qi: (b, qi + off, 0)),
                  pl.BlockSpec((1, t, nkv), lambda b, qi: (b, 0, 0)),
                  pl.BlockSpec((1, nkv, t), lambda b, qi: (b, 0, 0))],
        out_specs=pl.BlockSpec((1, TM, nq), lambda b, qi: (b, qi, 0)),
        out_shape=jax.ShapeDtypeStruct((bsz, t - off * TM, nq), BF16),
        scratch_shapes=[pltpu.VMEM((GQA_KV_HEADS, 2, GQA_KB, group * TM), F32),
                        pltpu.VMEM((GQA_KV_HEADS, GQA_HD, group * TM), F32)],
        compiler_params=_params(("parallel", "arbitrary")), name="gqa",
    )(q, k, vt)


def _merge_kernel(x_ref, gt_ref, y0_ref, y1_ref, y2_ref, sg_ref, wb_ref, wo_ref, o_ref):
    z = None
    for i, y_ref in enumerate((y0_ref, y1_ref, y2_ref)):
        zi = sg_ref[0, :, i * D_MODEL:(i + 1) * D_MODEL].astype(F32) * _dot(y_ref[0], wb_ref[i])
        z = zi if z is None else z + zi
    o_ref[0] = x_ref[0] + gt_ref[0] * _dot(z.astype(BF16), wo_ref[...])


def _merge(xall, mod, ys, sg, wb, wo, off):
    bsz, t, _ = xall.shape
    ctx_row = mod.shape[0] - 1
    return pl.pallas_call(
        _merge_kernel,
        grid=(bsz, t // TM - off),
        in_specs=[_row_spec(off, D_MODEL), _mod_spec(5, off, ctx_row),
                  _row_spec(off, D_MODEL), _row_spec(0, D_MODEL), _row_spec(0, D_MODEL),
                  _row_spec(off, 3 * D_MODEL), _whole(wb), _whole(wo)],
        out_specs=_row_spec(0, D_MODEL),
        out_shape=jax.ShapeDtypeStruct((bsz, t - off * TM, D_MODEL), F32),
        compiler_params=_params(("parallel", "parallel")), name="merge",
    )(xall, mod, ys[0], ys[1], ys[2], sg, wb, wo)


def _block_diag_ones(head_dim):
    i = np.arange(MXU)
    return jnp.asarray((i[:, None] // head_dim) == (i[None, :] // head_dim), dtype=BF16)


def _chunk_tri(upper):
    i = np.arange(TM)
    same = (i[:, None] // GLA_CHUNK) == (i[None, :] // GLA_CHUNK)
    tri = (i[None, :] >= i[:, None]) if upper else (i[:, None] >= i[None, :])
    return jnp.asarray(same & tri, dtype=BF16)


def _rope_tables(ctx_len, seq):
    quarter = GQA_HD // 4
    freqs = ROPE_BASE ** (-np.arange(quarter, dtype=np.float64) / quarter)
    tok = np.arange(seq)
    ang_r = (tok // GRID_W)[:, None] * freqs
    ang_c = (tok % GRID_W)[:, None] * freqs
    ang = np.concatenate([ang_r, ang_r, ang_c, ang_c], axis=1)
    sign = np.tile(np.concatenate([-np.ones(quarter), np.ones(quarter)]), 2)
    cos = np.concatenate([np.ones((ctx_len, GQA_HD)), np.cos(ang)], axis=0)
    sin = np.concatenate([np.zeros((ctx_len, GQA_HD)), np.sin(ang) * sign], axis=0)
    return jnp.asarray(cos, F32), jnp.asarray(sin, F32)


def _layer_weights(l, ffn_w_in, ffn_w_out, w_in, gla_fg_w2, gla_fg_b, gla_norm_g, nat_q_norm, nat_k_norm,
                   gqa_q_norm, gqa_k_norm, w_branch, w_out):
    w = w_in[l]
    nqk = GLA_HEADS * GLA_DK
    nv = GLA_HEADS * GLA_DV
    o_fg = 2 * nqk + 2 * nv
    o_nat = o_fg + 2 * GLA_LR
    n = NAT_HEADS * NAT_HD
    o_gqa = o_nat + 3 * n
    nq = GQA_HEADS * GQA_HD
    nkv = GQA_KV_HEADS * GQA_HD
    o_gate = o_gqa + nq + 2 * nkv
    pad = jnp.zeros((D_MODEL, LANES - 2 * GLA_LR), w.dtype)
    w2 = gla_fg_w2[l]
    zeros_lr = jnp.zeros((GLA_LR, nqk), w2.dtype)
    zeros_rest = jnp.zeros((LANES - 2 * GLA_LR, nqk), w2.dtype)
    return {
        "ffn": [(ffn_w_in[l, i, :, :D_FF].astype(BF16), ffn_w_in[l, i, :, D_FF:].astype(BF16),
                 ffn_w_out[l, i].astype(BF16)) for i in range(2)],
        "w_gla": w[:, :o_fg].astype(BF16),
        "w_fg": jnp.concatenate([w[:, o_fg:o_nat], pad], axis=1).astype(BF16),
        "w_nat_qk": w[:, o_nat:o_nat + 2 * n].astype(BF16),
        "w_nat_vt": w[:, o_nat + 2 * n:o_gqa].T.astype(BF16),
        "w_gqa_qk": w[:, o_gqa:o_gqa + nq + nkv].astype(BF16),
        "w_gqa_vt": w[:, o_gqa + nq + nkv:o_gate].T.astype(BF16),
        "w_gates": w[:, o_gate:].astype(BF16),
        "w2": jnp.concatenate([jnp.concatenate([w2[0], zeros_lr, zeros_rest], axis=0),
                               jnp.concatenate([zeros_lr, w2[1], zeros_rest], axis=0)], axis=1).astype(BF16),
        "b2": gla_fg_b[l].reshape(1, 2 * nqk),
        "tril": _chunk_tri(False),
        "triu": _chunk_tri(True),
        "gla_gain": gla_norm_g[l].reshape(1, GLA_DV),
        "nat_gq": jnp.tile(nat_q_norm[l], NAT_HEADS).reshape(1, n),
        "nat_gk": jnp.tile(nat_k_norm[l], NAT_HEADS).reshape(1, n),
        "gqa_gq": jnp.tile(gqa_q_norm[l], GQA_HEADS).reshape(1, nq),
        "gqa_gk": jnp.tile(gqa_k_norm[l], GQA_KV_HEADS).reshape(1, nkv),
        "ones64": _block_diag_ones(NAT_HD),
        "ones128": _block_diag_ones(GQA_HD),
        "w_branch": w_branch[l].astype(BF16),
        "w_out": w_out[l].astype(BF16),
    }


def kernel(x, c, ctx, c_ctx, w_mod, b_mod, norm_g, ffn_w_in, ffn_w_out, w_in, gla_fg_w2, gla_fg_b, gla_norm_g,
           nat_q_norm, nat_k_norm, nat_rpb, gqa_q_norm, gqa_k_norm, w_branch, w_out):
    bsz, seq, _ = x.shape
    ctx_len = ctx.shape[1]
    depth = w_mod.shape[0]
    assert seq % TM == 0 and ctx_len == TM and seq // TM >= NAT_KBLOCKS

    mod_rows = -(-(bsz + 1) // SUBLANES) * SUBLANES
    cvec = jnp.concatenate([c, c_ctx[None, :], jnp.zeros((mod_rows - bsz - 1, D_MODEL), c.dtype)], axis=0)
    mods = _mod_table(cvec, w_mod, b_mod)[:, :bsz + 1].reshape(depth, bsz + 1, 1, N_MOD * D_MODEL)
    rope_cos, rope_sin = _rope_tables(ctx_len, seq)
    nat_bias = _nat_bias_tables(nat_rpb.reshape((-1,) + nat_rpb.shape[2:]), seq // GRID_W)

    xall = x
    for l in range(depth):
        need_ctx = l < depth - 1
        off = 0 if need_ctx else 1
        lw = _layer_weights(l, ffn_w_in, ffn_w_out, w_in, gla_fg_w2, gla_fg_b, gla_norm_g, nat_q_norm,
                            nat_k_norm, gqa_q_norm, gqa_k_norm, w_branch, w_out)
        mod = mods[l]
        g = [norm_g[l, i].reshape(1, D_MODEL) for i in range(3)]
        xall = _ffn(xall, mod, g[0], *lw["ffn"][0], sub=0, tile_off=0, ctx=ctx if l == 0 else None)
        (gq, gk, gv, gog, ggf, ggb), (nq_, nk_, nvt), (aq, ak, avt, sg) = _projections(
            xall, mod, g[1], lw, rope_cos, rope_sin)
        y_gla = _gla(gq, gk, gv, gog, ggf, ggb, lw, ctx_len)
        y_nat = _nat(nq_, nk_, nvt, nat_bias, l, need_ctx)
        y_gqa = _gqa(aq, ak, avt, need_ctx)
        xall = _merge(xall, mod, (y_gla, y_nat, y_gqa), sg, lw["w_branch"], lw["w_out"], off)
        xall = _ffn(xall, mod, g[2], *lw["ffn"][1], sub=2, tile_off=off)
    return xall
```

```python
import functools

import numpy as np
import jax
import jax.numpy as jnp
from jax import lax
from jax.experimental import pallas as pl
from jax.experimental.pallas import tpu as pltpu

F32 = jnp.float32
BF16 = jnp.bfloat16

D_MODEL = 1024
GRID_W = 64
N_MOD = 9
D_FF = 2816
RMS_EPS = 1e-6
NEG_INF = -1e30

GLA_HEADS = 4
GLA_DK = 128
GLA_DV = 256
GLA_LR = 16
GLA_TAU = 16.0
GLA_CHUNK = 128

NAT_HEADS = 16
NAT_HD = 64
WIN_R = 8
WIN_C = 16
NAT_QROWS = 4
NAT_KBLOCKS = 3

GQA_HEADS = 8
GQA_KV_HEADS = 2
GQA_HD = 128
ROPE_BASE = 10000.0
GQA_KB = 512
LOG2E = 1.4426950408889634

TM = 256
LANES = 128
SUBLANES = 8
MXU = 256
VMEM_LIMIT = 56 * 1024 * 1024


def _dot(a, b):
    return jnp.dot(a, b, preferred_element_type=F32)


def _dot_nt(a, b):
    return lax.dot_general(a, b, (((1,), (1,)), ((), ())), preferred_element_type=F32)


def _dot_tn(a, b):
    return lax.dot_general(a, b, (((0,), (0,)), ((), ())), preferred_element_type=F32)


def _sigmoid(x):
    return 1.0 / (1.0 + jnp.exp(-x))


def _log_sigmoid(x):
    return -(jnp.maximum(-x, 0.0) + jnp.log(1.0 + jnp.exp(-jnp.abs(x))))


def _norm_mod(x, g, shift, scale):
    y = x * lax.rsqrt(jnp.mean(x * x, axis=-1, keepdims=True) + RMS_EPS) * g
    return y * (1.0 + scale) + shift


def _head_norm(t, gain, ones_bd, head_dim):
    sq = (t * t).astype(BF16)
    parts = [_dot(sq[:, i:i + MXU], ones_bd) for i in range(0, t.shape[1], MXU)]
    ss = parts[0] if len(parts) == 1 else jnp.concatenate(parts, axis=1)
    return t * lax.rsqrt(ss * (1.0 / head_dim) + RMS_EPS) * gain


def _params(semantics):
    return pltpu.CompilerParams(dimension_semantics=semantics, vmem_limit_bytes=VMEM_LIMIT)


def _whole(arr):
    zeros = (0,) * arr.ndim
    return pl.BlockSpec(arr.shape, lambda *_: zeros, pipeline_mode=pl.Buffered(1))


def _row_spec(off, width):
    return pl.BlockSpec((1, TM, width), lambda b, j: (b, j + off, 0))


def _mod_spec(col, off, ctx_row):
    return pl.BlockSpec((1, 1, D_MODEL), lambda b, j: (jnp.where(j + off == 0, ctx_row, b), 0, col))


def _mod_kernel(c_ref, w_ref, b_ref, o_ref):
    c = c_ref[...]
    s = (c * _sigmoid(c)).astype(BF16)
    o_ref[0] = _dot(s, w_ref[0].astype(BF16)) + b_ref[0]


def _mod_table(cvec, w_mod, b_mod):
    depth = w_mod.shape[0]
    rows = cvec.shape[0]
    width = 3 * D_MODEL
    return pl.pallas_call(
        _mod_kernel,
        grid=(depth, N_MOD * D_MODEL // width),
        in_specs=[pl.BlockSpec((rows, D_MODEL), lambda l, n: (0, 0)),
                  pl.BlockSpec((1, D_MODEL, width), lambda l, n: (l, 0, n)),
                  pl.BlockSpec((1, 1, width), lambda l, n: (l, 0, n))],
        out_specs=pl.BlockSpec((1, rows, width), lambda l, n: (l, 0, n)),
        out_shape=jax.ShapeDtypeStruct((depth, rows, N_MOD * D_MODEL), F32),
        compiler_params=_params(("arbitrary", "arbitrary")),
        name="mod_table",
    )(cvec, w_mod, b_mod.reshape(depth, 1, N_MOD * D_MODEL))


def _next_tile(b, j, nb, nj):
    wrap = j + 1 == nj
    last = jnp.logical_and(wrap, b + 1 == nb)
    return jnp.where(jnp.logical_and(wrap, jnp.logical_not(last)), b + 1, b), \
        jnp.where(wrap, jnp.where(last, j, 0), j + 1)


def _ffn_kernel(x_ref, ctx_ref, xn_ref, ctxn_ref, sh_ref, sc_ref, gt_ref, shn_ref, scn_ref, g_ref,
                wa_ref, wb_ref, wd_ref, o_ref, h_ref, *, split_input):
    b, j = pl.program_id(0), pl.program_id(1)

    def tile(x_r, ctx_r, jj):
        return jnp.where(jj == 0, ctx_r[0], x_r[0]) if split_input else x_r[0]

    def hidden(xt, shift, scale):
        u = _norm_mod(xt, g_ref[...], shift, scale).astype(BF16)
        a = _dot(u, wa_ref[...])
        return (a * _sigmoid(a) * _dot(u, wb_ref[...])).astype(BF16)

    @pl.when(jnp.logical_and(b == 0, j == 0))
    def _():
        h_ref[...] = hidden(tile(x_ref, ctx_ref, j), sh_ref[0], sc_ref[0])

    o_ref[0] = tile(x_ref, ctx_ref, j) + (0.5 * gt_ref[0]) * _dot(h_ref[...], wd_ref[...])
    _, jn = _next_tile(b, j, pl.num_programs(0), pl.num_programs(1))
    h_ref[...] = hidden(tile(xn_ref, ctxn_ref, jn), shn_ref[0], scn_ref[0])


def _ffn(xall, mod, g, wa, wb, wd, sub, tile_off, ctx=None):
    bsz, t, _ = xall.shape
    split = ctx is not None
    if split:
        t += ctx.shape[1]
    nj = t // TM
    ctx_row = mod.shape[0] - 1
    cur = lambda b, j: (b, j)
    nxt = lambda b, j: _next_tile(b, j, bsz, nj)

    def specs(at):
        if split:
            x_spec = pl.BlockSpec((1, TM, D_MODEL), lambda b, j: (at(b, j)[0], jnp.maximum(at(b, j)[1] - 1, 0), 0))
            ctx_spec = pl.BlockSpec((1, TM, D_MODEL), lambda b, j: (at(b, j)[0], 0, 0))
        else:
            x_spec = pl.BlockSpec((1, TM, D_MODEL), lambda b, j: (at(b, j)[0], at(b, j)[1], 0))
            ctx_spec = pl.BlockSpec((1, SUBLANES, D_MODEL), lambda b, j: (at(b, j)[0], 0, 0))
        return [x_spec, ctx_spec]

    def mod_spec(col, at):
        return pl.BlockSpec((1, 1, D_MODEL),
                            lambda b, j: (jnp.where(at(b, j)[1] + tile_off == 0, ctx_row, at(b, j)[0]), 0, col))

    if not split:
        ctx = xall
    return pl.pallas_call(
        functools.partial(_ffn_kernel, split_input=split),
        grid=(bsz, nj),
        in_specs=specs(cur) + specs(nxt) +
                 [mod_spec(3 * sub, cur), mod_spec(3 * sub + 1, cur), mod_spec(3 * sub + 2, cur),
                  mod_spec(3 * sub, nxt), mod_spec(3 * sub + 1, nxt),
                  _whole(g), _whole(wa), _whole(wb), _whole(wd)],
        out_specs=_row_spec(0, D_MODEL),
        out_shape=jax.ShapeDtypeStruct((bsz, t, D_MODEL), F32),
        scratch_shapes=[pltpu.VMEM((TM, wd.shape[0]), BF16)],
        compiler_params=_params(("arbitrary", "arbitrary")),
        name="ffn",
    )(xall, ctx, xall, ctx, mod, mod, mod, mod, mod, g, wa, wb, wd)


def _chunk_cumsum(tri, la):
    hi = la.astype(BF16)
    lo = (la - hi.astype(F32)).astype(BF16)
    return _dot(tri, hi) + _dot(tri, lo)


def _proj_gla_part(u, w_ref, wfg_ref, w2_ref, b2_ref, tril_ref, triu_ref,
                   q_ref, k_ref, v_ref, og_ref, gf_ref, gb_ref):
    nqk = GLA_HEADS * GLA_DK
    nv = GLA_HEADS * GLA_DV
    fg = _dot(u, wfg_ref[...]).astype(BF16)
    q_ref[0] = (_dot(u, w_ref[:, :nqk]) * GLA_DK ** -0.5).astype(BF16)
    z = _dot(fg, w2_ref[...]) + b2_ref[...]
    k_ref[0] = _dot(u, w_ref[:, nqk:2 * nqk]).astype(BF16)
    la = _log_sigmoid(z) * (1.0 / GLA_TAU)
    v_ref[0] = _dot(u, w_ref[:, 2 * nqk:2 * nqk + nv]).astype(BF16)
    gf_ref[0] = _chunk_cumsum(tril_ref[...], la[:, :nqk])
    gb_ref[0] = _chunk_cumsum(triu_ref[...], la[:, nqk:])
    og = _dot(u, w_ref[:, 2 * nqk + nv:])
    og_ref[0] = (og * _sigmoid(og)).astype(BF16)


def _proj_nat_part(u, wqk_ref, wvt_ref, gq_ref, gk_ref, ones_ref, q_ref, k_ref, vt_ref):
    p = _dot(u, wqk_ref[...])
    n = NAT_HEADS * NAT_HD
    ones_bd = ones_ref[...]
    q_ref[0] = (_head_norm(p[:, :n], gq_ref[...], ones_bd, NAT_HD) * (NAT_HD ** -0.5 * LOG2E)).astype(BF16)
    k_ref[0] = _head_norm(p[:, n:], gk_ref[...], ones_bd, NAT_HD).astype(BF16)
    vt_ref[0] = _dot_nt(wvt_ref[...], u).astype(BF16)


def _rope(t, cos, sin, lane):
    partner = jnp.where(lane % (GQA_HD // 2) < GQA_HD // 4,
                        pltpu.roll(t, GQA_HD - GQA_HD // 4, axis=1), pltpu.roll(t, GQA_HD // 4, axis=1))
    return t * cos + partner * sin


def _proj_gqa_part(u, wqk_ref, wvt_ref, wg_ref, gq_ref, gk_ref, ones_ref, cos_ref, sin_ref,
                   q_ref, k_ref, vt_ref, sg_ref):
    p = _dot(u, wqk_ref[...])
    nq = GQA_HEADS * GQA_HD
    ones_bd = ones_ref[...]
    qn = _head_norm(p[:, :nq], gq_ref[...], ones_bd, GQA_HD)
    kn = _head_norm(p[:, nq:], gk_ref[...], ones_bd, GQA_HD)
    cos = cos_ref[...]
    sin = sin_ref[...]
    lane = lax.broadcasted_iota(jnp.int32, (TM, GQA_HD), 1)
    for h in range(GQA_HEADS):
        sl = slice(h * GQA_HD, (h + 1) * GQA_HD)
        q_ref[0, :, sl] = (_rope(qn[:, sl], cos, sin, lane) * (GQA_HD ** -0.5 * LOG2E)).astype(BF16)
    for h in range(GQA_KV_HEADS):
        sl = slice(h * GQA_HD, (h + 1) * GQA_HD)
        k_ref[0, :, sl] = _rope(kn[:, sl], cos, sin, lane).astype(BF16)
    vt_ref[0] = _dot_nt(wvt_ref[...], u).astype(BF16)
    sg_ref[0] = _sigmoid(_dot(u, wg_ref[...])).astype(BF16)


N_PROJ_GLA_IN, N_PROJ_NAT_IN, N_PROJ_GQA_IN = 6, 5, 8
N_PROJ_GLA_OUT, N_PROJ_NAT_OUT, N_PROJ_GQA_OUT = 6, 3, 4


def _proj_kernel(*refs):
    it = iter(refs)
    take = lambda count: [next(it) for _ in range(count)]
    x_ref, sh_ref, sc_ref, g_ref = take(4)
    gla_in, nat_in, gqa_in = take(N_PROJ_GLA_IN), take(N_PROJ_NAT_IN), take(N_PROJ_GQA_IN)
    gla_out, nat_out, gqa_out = take(N_PROJ_GLA_OUT), take(N_PROJ_NAT_OUT), take(N_PROJ_GQA_OUT)
    u_ref, = take(1)

    @pl.when(pl.program_id(0) == 0)
    def _():
        u_ref[...] = jnp.zeros_like(u_ref)

    u_prev = u_ref[...]
    _proj_nat_part(u_prev, *nat_in, *nat_out)
    u = _norm_mod(x_ref[0], g_ref[...], sh_ref[0], sc_ref[0]).astype(BF16)
    _proj_gqa_part(u_prev, *gqa_in, *gqa_out)
    _proj_gla_part(u, *gla_in, *gla_out)
    u_ref[...] = u


def _projections(xall, mod, g, lw, rope_cos, rope_sin):
    bsz, t, _ = xall.shape
    ctx_row = mod.shape[0] - 1
    nt = t // TM
    n_tiles = bsz * nt
    lead = lambda s: jnp.minimum(s, n_tiles - 1)
    lag = lambda s: jnp.maximum(s - 1, 0)

    def rows(at, width):
        return pl.BlockSpec((1, TM, width), lambda s: (at(s) // nt, at(s) % nt, 0))

    def cols(at, height):
        return pl.BlockSpec((1, height, TM), lambda s: (at(s) // nt, 0, at(s) % nt))

    def mod_spec(col):
        return pl.BlockSpec((1, 1, D_MODEL),
                            lambda s: (jnp.where(lead(s) % nt == 0, ctx_row, lead(s) // nt), 0, col))

    sds = jax.ShapeDtypeStruct
    nqk = GLA_HEADS * GLA_DK
    nv = GLA_HEADS * GLA_DV
    n = NAT_HEADS * NAT_HD
    nq = GQA_HEADS * GQA_HD
    nkv = GQA_KV_HEADS * GQA_HD
    tab = pl.BlockSpec((TM, GQA_HD), lambda s: (lag(s) % nt, 0))
    gla_in = [lw["w_gla"], lw["w_fg"], lw["w2"], lw["b2"], lw["tril"], lw["triu"]]
    nat_in = [lw["w_nat_qk"], lw["w_nat_vt"], lw["nat_gq"], lw["nat_gk"], lw["ones64"]]
    gqa_in = [lw["w_gqa_qk"], lw["w_gqa_vt"], lw["w_gates"], lw["gqa_gq"], lw["gqa_gk"], lw["ones128"]]
    assert (len(gla_in), len(nat_in), len(gqa_in) + 2) == (N_PROJ_GLA_IN, N_PROJ_NAT_IN, N_PROJ_GQA_IN)
    outs = pl.pallas_call(
        _proj_kernel, grid=(n_tiles + 1,),
        in_specs=[rows(lead, D_MODEL), mod_spec(3), mod_spec(4), _whole(g)]
                 + [_whole(a) for a in gla_in + nat_in + gqa_in] + [tab, tab],
        out_specs=[rows(lead, nqk), rows(lead, nqk), rows(lead, nv), rows(lead, nv), rows(lead, nqk), rows(lead, nqk),
                   rows(lag, n), rows(lag, n), cols(lag, n),
                   rows(lag, nq), rows(lag, nkv), cols(lag, nkv), rows(lag, 3 * D_MODEL)],
        out_shape=[sds((bsz, t, nqk), BF16), sds((bsz, t, nqk), BF16), sds((bsz, t, nv), BF16),
                   sds((bsz, t, nv), BF16), sds((bsz, t, nqk), F32), sds((bsz, t, nqk), F32),
                   sds((bsz, t, n), BF16), sds((bsz, t, n), BF16), sds((bsz, n, t), BF16),
                   sds((bsz, t, nq), BF16), sds((bsz, t, nkv), BF16), sds((bsz, nkv, t), BF16),
                   sds((bsz, t, 3 * D_MODEL), BF16)],
        scratch_shapes=[pltpu.VMEM((TM, D_MODEL), BF16)],
        compiler_params=_params(("arbitrary",)), name="proj",
    )(xall, mod, mod, g, *gla_in, *nat_in, *gqa_in, rope_cos, rope_sin)
    return outs[:6], outs[6:9], outs[9:]


GLA_UNROLL = 16


def _gla_chain(n, forward, q_ref, k_ref, v_ref, g_ref, o_ref, state, tri, emit):
    c = GLA_CHUNK
    rows = pl.ds(pl.multiple_of(jnp.int32(n) * c, c), c)
    g = g_ref[0, rows, :]
    if forward:
        g_mid = g[c // 2 - 1:c // 2, :]
        g_edge = g[c - 1:c, :]
    else:
        g_mid = g[c // 2:c // 2 + 1, :]
        g_edge = g[0:1, :]
    q = q_ref[0, rows, :].astype(F32)
    k = k_ref[0, rows, :].astype(F32)
    v = v_ref[0, rows, :]
    q_in = (q * jnp.exp(g - g_mid)).astype(BF16)
    k_in = (k * jnp.exp(g_mid - g)).astype(BF16)
    q_x = (q * jnp.exp(g)).astype(BF16)
    k_end = (k * jnp.exp(g_edge - g)).astype(BF16)
    decay = jnp.exp(g_edge)
    yield
    att = _dot_nt(q_in, k_in)
    s_in = state["s"]
    state["s"] = s_in * decay + _dot_tn(v, k_end)
    yield
    att = jnp.where(tri, att, 0.0).astype(BF16)
    o = _dot(att, v) + _dot_nt(q_x, s_in.astype(BF16))
    yield
    if emit is None:
        o_ref[rows, :] = o
    else:
        emit(rows, o)


def _gla_kernel(q_ref, k_ref, v_ref, og_ref, gf_ref, gb_ref, gain_ref, y_ref,
                of_ref, ob_ref, sf_ref, sb_ref, *, n_chunks, n_ctx_chunks):
    c = GLA_CHUNK
    ri = lax.broadcasted_iota(jnp.int32, (c, c), 0)
    ci = lax.broadcasted_iota(jnp.int32, (c, c), 1)
    lower = ri >= ci
    upper = ci >= ri
    sf_ref[...] = jnp.zeros_like(sf_ref)
    sb_ref[...] = jnp.zeros_like(sb_ref)

    def finish(rows, o):
        y = o * lax.rsqrt(jnp.mean(o * o, axis=-1, keepdims=True) + RMS_EPS) * gain_ref[...]
        y_ref[0, rows, :] = (y * og_ref[0, rows, :].astype(F32)).astype(BF16)

    def scan(it, first, unroll, completing):
        fwd = {"s": sf_ref[...]}
        bwd = {"s": sb_ref[...]}
        emit_f = (lambda rows, o: finish(rows, o + ob_ref[rows, :])) if completing else None
        emit_b = (lambda rows, o: finish(rows, o + of_ref[rows, :])) if completing else None
        chains = []
        for u in range(unroll):
            s = first + it * unroll + u
            nb = jnp.where(s < n_ctx_chunks, n_ctx_chunks - 1 - s, n_chunks - 1 - (s - n_ctx_chunks))
            chains.append(_gla_chain(s, True, q_ref, k_ref, v_ref, gf_ref, of_ref, fwd, lower, emit_f))
            chains.append(_gla_chain(nb, False, q_ref, k_ref, v_ref, gb_ref, ob_ref, bwd, upper, emit_b))
        while chains:
            alive = []
            for ch in chains:
                try:
                    next(ch)
                    alive.append(ch)
                except StopIteration:
                    pass
            chains = alive
        sf_ref[...] = fwd["s"]
        sb_ref[...] = bwd["s"]

    half = (n_chunks - n_ctx_chunks) // 2
    scan(0, 0, n_ctx_chunks, False)
    for n in range(n_ctx_chunks):
        rows = pl.ds(n * c, c)
        finish(rows, of_ref[rows, :] + ob_ref[rows, :])
    lax.fori_loop(0, half // GLA_UNROLL,
                  lambda it, carry: scan(it, n_ctx_chunks, GLA_UNROLL, False) or carry, 0)
    lax.fori_loop(0, half // GLA_UNROLL,
                  lambda it, carry: scan(it, n_ctx_chunks + half, GLA_UNROLL, True) or carry, 0)


def _gla(q, k, v, og, gf, gb, lw, ctx_len):
    bsz, t, _ = q.shape
    assert (t - ctx_len) % (2 * GLA_UNROLL * GLA_CHUNK) == 0 and ctx_len % GLA_CHUNK == 0
    seq_spec = lambda w: pl.BlockSpec((1, t, w), lambda b, h: (b, 0, h))
    kern = functools.partial(_gla_kernel, n_chunks=t // GLA_CHUNK, n_ctx_chunks=ctx_len // GLA_CHUNK)
    return pl.pallas_call(
        kern, grid=(bsz, GLA_HEADS),
        in_specs=[seq_spec(GLA_DK), seq_spec(GLA_DK), seq_spec(GLA_DV), seq_spec(GLA_DV),
                  seq_spec(GLA_DK), seq_spec(GLA_DK), pl.BlockSpec((1, GLA_DV), lambda b, h: (0, 0))],
        out_specs=seq_spec(GLA_DV),
        out_shape=jax.ShapeDtypeStruct((bsz, t, GLA_HEADS * GLA_DV), BF16),
        scratch_shapes=[pltpu.VMEM((t, GLA_DV), F32), pltpu.VMEM((t, GLA_DV), F32),
                        pltpu.VMEM((GLA_DV, GLA_DK), F32), pltpu.VMEM((GLA_DV, GLA_DK), F32)],
        compiler_params=_params(("parallel", "parallel")), name="gla",
    )(q, k, v, og, gf, gb, lw["gla_gain"])


NAT_GROUP = 8


def _nat_kernel(q_ref, k0_ref, k1_ref, k2_ref, kc_ref, v0_ref, v1_ref, v2_ref, vc_ref, bias_ref, o_ref, s_ref,
                *, off):
    is_ctx = pl.program_id(0) + off == 0
    lane = lax.broadcasted_iota(jnp.int32, (TM, LANES), 1)
    row = lax.broadcasted_iota(jnp.int32, (LANES, TM), 0)
    n_heads = 2 * NAT_GROUP
    n_lat = NAT_KBLOCKS * TM

    def run(lat_keys, lat_vals):
        kcats, vlats = {}, {}

        def scores(i):
            hp, e = divmod(i, 2)
            sl = slice(hp * LANES, (hp + 1) * LANES)
            if hp not in kcats:
                refs = lat_keys + [kc_ref]
                half = (len(refs) + 1) // 2
                kcats[hp] = [jnp.concatenate([r[0, :, sl] for r in part], axis=0)
                             for part in (refs[:half], refs[half:]) if part]
            q = q_ref[0, :, sl]
            qm = jnp.where((lane >= NAT_HD) if e else (lane < NAT_HD), q, jnp.zeros_like(q))
            at = 0
            for kpart in kcats[hp]:
                s_ref[i % 2, at:at + kpart.shape[0], :] = _dot_nt(kpart, qm)
                at += kpart.shape[0]

        scores(0)
        outs = []
        for i in range(n_heads):
            if i + 1 < n_heads:
                scores(i + 1)
            hp, e = divmod(i, 2)
            sl = slice(hp * LANES, (hp + 1) * LANES)
            n_l = n_lat if lat_keys else 0
            s_ctx = s_ref[i % 2, n_l:n_l + TM, :]
            m = jnp.max(s_ctx, axis=0, keepdims=True)
            if lat_keys:
                s_lat = s_ref[i % 2, :n_lat, :] + bias_ref[i, 0]
                m = jnp.maximum(m, jnp.max(s_lat, axis=0, keepdims=True))
            def own_rows(v):
                vrow = lax.broadcasted_iota(jnp.int32, v.shape, 0)
                return jnp.where((vrow >= NAT_HD) if e else (vrow < NAT_HD), v, jnp.ones_like(v))

            p_ctx = jnp.exp2(s_ctx - m)
            o = _dot(own_rows(vc_ref[0, sl, :]), p_ctx.astype(BF16))
            if lat_keys:
                if hp not in vlats:
                    vlats[hp] = jnp.concatenate([r[0, sl, :] for r in lat_vals], axis=1)
                p_lat = jnp.exp2(s_lat - m)
                o = o + _dot(own_rows(vlats[hp]), p_lat.astype(BF16))
            l = o[0:1, :] if e else o[LANES - 1:LANES, :]
            outs.append(o * (1.0 / l))
            if e == 1:
                ot = jnp.where(row < NAT_HD, outs[-2], outs[-1])
                o_ref[0, :, sl] = ot.T.astype(BF16)

    @pl.when(is_ctx)
    def _():
        run([], [])

    @pl.when(jnp.logical_not(is_ctx))
    def _():
        run([k0_ref, k1_ref, k2_ref], [v0_ref, v1_ref, v2_ref])


def _nat(q, k, vt, bias, layer, need_ctx):
    bsz, t, n = q.shape
    off = 0 if need_ctx else 1
    nblk = t // TM - 1
    gw = NAT_GROUP * LANES
    ngroups = n // gw

    def kblock(qi, i):
        j = qi + off - 1
        return 1 + jnp.clip(j - 1, 0, nblk - NAT_KBLOCKS) + i

    def bias_class(qi):
        j = qi + off - 1
        return jnp.where(j <= 0, 0, jnp.where(j == nblk - 1, 2, 1))

    kspec = lambda i: pl.BlockSpec((1, TM, gw), lambda qi, hg, b: (b, kblock(qi, i), hg))
    vspec = lambda i: pl.BlockSpec((1, gw, TM), lambda qi, hg, b: (b, hg, kblock(qi, i)))
    return pl.pallas_call(
        functools.partial(_nat_kernel, off=off),
        grid=(t // TM - off, ngroups, bsz),
        in_specs=[pl.BlockSpec((1, TM, gw), lambda qi, hg, b: (b, qi + off, hg)),
                  kspec(0), kspec(1), kspec(2),
                  pl.BlockSpec((1, TM, gw), lambda qi, hg, b: (b, 0, hg)),
                  vspec(0), vspec(1), vspec(2),
                  pl.BlockSpec((1, gw, TM), lambda qi, hg, b: (b, hg, 0)),
                  pl.BlockSpec((2 * NAT_GROUP, 1, NAT_KBLOCKS * TM, TM),
                               lambda qi, hg, b: (layer * ngroups + hg, bias_class(qi), 0, 0))],
        out_specs=pl.BlockSpec((1, TM, gw), lambda qi, hg, b: (b, qi, hg)),
        out_shape=jax.ShapeDtypeStruct((bsz, t - off * TM, n), BF16),
        scratch_shapes=[pltpu.VMEM((2, (NAT_KBLOCKS + 1) * TM, TM), F32)],
        compiler_params=_params(("parallel", "parallel", "parallel")), name="nat",
    )(q, k, k, k, k, vt, vt, vt, vt, bias)


def _nat_bias_tables(rpb, rows):
    nblk = rows // NAT_QROWS
    kr_win = min(WIN_R, rows)
    col = np.arange(GRID_W)
    c_start = np.clip(col - WIN_C // 2, 0, GRID_W - WIN_C)
    cmask = (col[:, None] >= c_start[None, :]) & (col[:, None] < c_start[None, :] + WIN_C)
    dc = np.clip(col[:, None] - col[None, :] + WIN_C - 1, 0, 2 * WIN_C - 2)
    drs, rmasks = [], []
    for jblk in (0, 1, nblk - 1):
        base = int(np.clip(jblk - 1, 0, nblk - NAT_KBLOCKS))
        r = NAT_QROWS * jblk + np.arange(NAT_QROWS)
        r_start = np.clip(r - kr_win // 2, 0, rows - kr_win)
        kr = NAT_QROWS * base + np.arange(NAT_KBLOCKS * NAT_QROWS)
        rmasks.append((kr[:, None] >= r_start[None, :]) & (kr[:, None] < r_start[None, :] + kr_win))
        drs.append(np.clip(kr[:, None] - r[None, :] + WIN_R - 1, 0, 2 * WIN_R - 2))
    drx = np.where(np.stack(rmasks), np.stack(drs), 2 * WIN_R - 1)
    dcx = np.where(cmask, dc, 2 * WIN_C - 1)
    nh = rpb.shape[0]
    ext = jnp.full((nh, 2 * WIN_R, 2 * WIN_C), NEG_INF, F32)
    ext = ext.at[:, :2 * WIN_R - 1, :2 * WIN_C - 1].set(rpb.astype(F32) * LOG2E)
    pick = np.tile(dcx[:, None, :], (1, NAT_QROWS, 1)).reshape(-1)
    onehot = jnp.asarray(np.arange(2 * WIN_C)[:, None] == pick[None, :], F32)
    a = jnp.einsum("hdj,jn->hdn", ext, onehot, precision=lax.Precision.HIGHEST)
    a = a.reshape(nh, 2 * WIN_R, GRID_W, NAT_QROWS * GRID_W)
    nk = NAT_KBLOCKS * NAT_QROWS * GRID_W
    return pl.pallas_call(
        functools.partial(_bias_kernel, drx=drx),
        grid=(nh,),
        in_specs=[pl.BlockSpec((1,) + a.shape[1:], lambda h: (h, 0, 0, 0))],
        out_specs=pl.BlockSpec((1, 3, nk, TM), lambda h: (h, 0, 0, 0)),
        out_shape=jax.ShapeDtypeStruct((nh, 3, nk, TM), F32),
        compiler_params=_params(("parallel",)), name="nat_bias",
    )(a)


def _bias_kernel(a_ref, o_ref, *, drx):
    lane_blk = lax.broadcasted_iota(jnp.int32, (GRID_W, TM), 1) // GRID_W
    n_cls, n_kri, n_ri = drx.shape
    for cls in range(n_cls):
        for kri in range(n_kri):
            strip = a_ref[0, int(drx[cls, kri, n_ri - 1])]
            for ri in range(n_ri - 1):
                strip = jnp.where(lane_blk == ri, a_ref[0, int(drx[cls, kri, ri])], strip)
            o_ref[0, cls, kri * GRID_W:(kri + 1) * GRID_W, :] = strip


def _gqa_kernel(q_ref, k_ref, vt_ref, o_ref, s_ref, acc_ref, *, off, n_lat):
    is_ctx = pl.program_id(1) + off == 0
    group = GQA_HEADS // GQA_KV_HEADS

    def run(n_blocks):
        qs = {}

        def scores(h, start, size, slot):
            if h not in qs:
                heads = range(h * group, (h + 1) * group)
                qs[h] = jnp.concatenate([q_ref[0, :, g * GQA_HD:(g + 1) * GQA_HD] for g in heads], axis=0)
            s_ref[h, slot, :size, :] = _dot_nt(k_ref[0, start:start + size, h * GQA_HD:(h + 1) * GQA_HD], qs[h])

        def start_head(h):
            scores(h, 0, TM, 0)
            if n_blocks:
                scores(h, TM, GQA_KB, 1)

        start_head(0)
        for h in range(GQA_KV_HEADS):
            rows = slice(h * GQA_HD, (h + 1) * GQA_HD)
            if n_blocks == 0 and h + 1 < GQA_KV_HEADS:
                start_head(h + 1)
            st = s_ref[h, 0, :TM, :]
            m = jnp.max(st, axis=0, keepdims=True)
            p = jnp.exp2(st - m)
            l = jnp.sum(p, axis=0, keepdims=True)
            acc_ref[h] = _dot(vt_ref[0, rows, :TM], p.astype(BF16))
            for j in range(n_blocks):
                start = TM + j * GQA_KB
                if j + 1 < n_blocks:
                    scores(h, start + GQA_KB, GQA_KB, j % 2)
                elif h + 1 < GQA_KV_HEADS:
                    start_head(h + 1)
                st = s_ref[h, (j + 1) % 2]
                m_new = jnp.maximum(m, jnp.max(st, axis=0, keepdims=True))
                alpha = jnp.exp2(m - m_new)
                p = jnp.exp2(st - m_new)
                l = alpha * l + jnp.sum(p, axis=0, keepdims=True)
                acc_ref[h] = alpha * acc_ref[h] + _dot(vt_ref[0, rows, start:start + GQA_KB], p.astype(BF16))
                m = m_new
            ot = acc_ref[h] * (1.0 / l)
            for g in range(group):
                cols = slice((h * group + g) * GQA_HD, (h * group + g + 1) * GQA_HD)
                o_ref[0, :, cols] = ot[:, g * TM:(g + 1) * TM].T.astype(BF16)

    @pl.when(is_ctx)
    def _():
        run(0)

    @pl.when(jnp.logical_not(is_ctx))
    def _():
        run(n_lat)


def _gqa(q, k, vt, need_ctx):
    bsz, t, nq = q.shape
    nkv = k.shape[2]
    off = 0 if need_ctx else 1
    group = GQA_HEADS // GQA_KV_HEADS
    assert (t - TM) % GQA_KB == 0
    return pl.pallas_call(
        functools.partial(_gqa_kernel, off=off, n_lat=(t - TM) // GQA_KB),
        grid=(bsz, t // TM - off),
        in_specs=[pl.BlockSpec((1, TM, nq), lambda b, qi: (b, qi + off, 0)),
                  pl.BlockSpec((1, t, nkv), lambda b, qi: (b, 0, 0)),
                  pl.BlockSpec((1, nkv, t), lambda b, qi: (b, 0, 0))],
        out_specs=pl.BlockSpec((1, TM, nq), lambda b, qi: (b, qi, 0)),
        out_shape=jax.ShapeDtypeStruct((bsz, t - off * TM, nq), BF16),
        scratch_shapes=[pltpu.VMEM((GQA_KV_HEADS, 2, GQA_KB, group * TM), F32),
                        pltpu.VMEM((GQA_KV_HEADS, GQA_HD, group * TM), F32)],
        compiler_params=_params(("parallel", "arbitrary")), name="gqa",
    )(q, k, vt)


def _merge_kernel(x_ref, gt_ref, y0_ref, y1_ref, y2_ref, sg_ref, wb_ref, wo_ref, o_ref):
    z = None
    for i, y_ref in enumerate((y0_ref, y1_ref, y2_ref)):
        zi = sg_ref[0, :, i * D_MODEL:(i + 1) * D_MODEL].astype(F32) * _dot(y_ref[0], wb_ref[i])
        z = zi if z is None else z + zi
    o_ref[0] = x_ref[0] + gt_ref[0] * _dot(z.astype(BF16), wo_ref[...])


def _merge(xall, mod, ys, sg, wb, wo, off):
    bsz, t, _ = xall.shape
    ctx_row = mod.shape[0] - 1
    return pl.pallas_call(
        _merge_kernel,
        grid=(bsz, t // TM - off),
        in_specs=[_row_spec(off, D_MODEL), _mod_spec(5, off, ctx_row),
                  _row_spec(off, D_MODEL), _row_spec(0, D_MODEL), _row_spec(0, D_MODEL),
                  _row_spec(off, 3 * D_MODEL), _whole(wb), _whole(wo)],
        out_specs=_row_spec(0, D_MODEL),
        out_shape=jax.ShapeDtypeStruct((bsz, t - off * TM, D_MODEL), F32),
        compiler_params=_params(("parallel", "parallel")), name="merge",
    )(xall, mod, ys[0], ys[1], ys[2], sg, wb, wo)


def _block_diag_ones(head_dim):
    i = np.arange(MXU)
    return jnp.asarray((i[:, None] // head_dim) == (i[None, :] // head_dim), dtype=BF16)


def _chunk_tri(upper):
    i = np.arange(TM)
    same = (i[:, None] // GLA_CHUNK) == (i[None, :] // GLA_CHUNK)
    tri = (i[None, :] >= i[:, None]) if upper else (i[:, None] >= i[None, :])
    return jnp.asarray(same & tri, dtype=BF16)


def _rope_tables(ctx_len, seq):
    quarter = GQA_HD // 4
    freqs = ROPE_BASE ** (-np.arange(quarter, dtype=np.float64) / quarter)
    tok = np.arange(seq)
    ang_r = (tok // GRID_W)[:, None] * freqs
    ang_c = (tok % GRID_W)[:, None] * freqs
    ang = np.concatenate([ang_r, ang_r, ang_c, ang_c], axis=1)
    sign = np.tile(np.concatenate([-np.ones(quarter), np.ones(quarter)]), 2)
    cos = np.concatenate([np.ones((ctx_len, GQA_HD)), np.cos(ang)], axis=0)
    sin = np.concatenate([np.zeros((ctx_len, GQA_HD)), np.sin(ang) * sign], axis=0)
    return jnp.asarray(cos, F32), jnp.asarray(sin, F32)


def _layer_weights(l, ffn_w_in, ffn_w_out, w_in, gla_fg_w2, gla_fg_b, gla_norm_g, nat_q_norm, nat_k_norm,
                   gqa_q_norm, gqa_k_norm, w_branch, w_out):
    w = w_in[l]
    nqk = GLA_HEADS * GLA_DK
    nv = GLA_HEADS * GLA_DV
    o_fg = 2 * nqk + 2 * nv
    o_nat = o_fg + 2 * GLA_LR
    n = NAT_HEADS * NAT_HD
    o_gqa = o_nat + 3 * n
    nq = GQA_HEADS * GQA_HD
    nkv = GQA_KV_HEADS * GQA_HD
    o_gate = o_gqa + nq + 2 * nkv
    pad = jnp.zeros((D_MODEL, LANES - 2 * GLA_LR), w.dtype)
    w2 = gla_fg_w2[l]
    zeros_lr = jnp.zeros((GLA_LR, nqk), w2.dtype)
    zeros_rest = jnp.zeros((LANES - 2 * GLA_LR, nqk), w2.dtype)
    return {
        "ffn": [(ffn_w_in[l, i, :, :D_FF].astype(BF16), ffn_w_in[l, i, :, D_FF:].astype(BF16),
                 ffn_w_out[l, i].astype(BF16)) for i in range(2)],
        "w_gla": w[:, :o_fg].astype(BF16),
        "w_fg": jnp.concatenate([w[:, o_fg:o_nat], pad], axis=1).astype(BF16),
        "w_nat_qk": w[:, o_nat:o_nat + 2 * n].astype(BF16),
        "w_nat_vt": w[:, o_nat + 2 * n:o_gqa].T.astype(BF16),
        "w_gqa_qk": w[:, o_gqa:o_gqa + nq + nkv].astype(BF16),
        "w_gqa_vt": w[:, o_gqa + nq + nkv:o_gate].T.astype(BF16),
        "w_gates": w[:, o_gate:].astype(BF16),
        "w2": jnp.concatenate([jnp.concatenate([w2[0], zeros_lr, zeros_rest], axis=0),
                               jnp.concatenate([zeros_lr, w2[1], zeros_rest], axis=0)], axis=1).astype(BF16),
        "b2": gla_fg_b[l].reshape(1, 2 * nqk),
        "tril": _chunk_tri(False),
        "triu": _chunk_tri(True),
        "gla_gain": gla_norm_g[l].reshape(1, GLA_DV),
        "nat_gq": jnp.tile(nat_q_norm[l], NAT_HEADS).reshape(1, n),
        "nat_gk": jnp.tile(nat_k_norm[l], NAT_HEADS).reshape(1, n),
        "gqa_gq": jnp.tile(gqa_q_norm[l], GQA_HEADS).reshape(1, nq),
        "gqa_gk": jnp.tile(gqa_k_norm[l], GQA_KV_HEADS).reshape(1, nkv),
        "ones64": _block_diag_ones(NAT_HD),
        "ones128": _block_diag_ones(GQA_HD),
        "w_branch": w_branch[l].astype(BF16),
        "w_out": w_out[l].astype(BF16),
    }


def kernel(x, c, ctx, c_ctx, w_mod, b_mod, norm_g, ffn_w_in, ffn_w_out, w_in, gla_fg_w2, gla_fg_b, gla_norm_g,
           nat_q_norm, nat_k_norm, nat_rpb, gqa_q_norm, gqa_k_norm, w_branch, w_out):
    bsz, seq, _ = x.shape
    ctx_len = ctx.shape[1]
    depth = w_mod.shape[0]
    assert seq % TM == 0 and ctx_len == TM and seq // TM >= NAT_KBLOCKS

    mod_rows = -(-(bsz + 1) // SUBLANES) * SUBLANES
    cvec = jnp.concatenate([c, c_ctx[None, :], jnp.zeros((mod_rows - bsz - 1, D_MODEL), c.dtype)], axis=0)
    mods = _mod_table(cvec, w_mod, b_mod)[:, :bsz + 1].reshape(depth, bsz + 1, 1, N_MOD * D_MODEL)
    rope_cos, rope_sin = _rope_tables(ctx_len, seq)
    nat_bias = _nat_bias_tables(nat_rpb.reshape((-1,) + nat_rpb.shape[2:]), seq // GRID_W)

    xall = x
    for l in range(depth):
        need_ctx = l < depth - 1
        off = 0 if need_ctx else 1
        lw = _layer_weights(l, ffn_w_in, ffn_w_out, w_in, gla_fg_w2, gla_fg_b, gla_norm_g, nat_q_norm,
                            nat_k_norm, gqa_q_norm, gqa_k_norm, w_branch, w_out)
        mod = mods[l]
        g = [norm_g[l, i].reshape(1, D_MODEL) for i in range(3)]
        xall = _ffn(xall, mod, g[0], *lw["ffn"][0], sub=0, tile_off=0, ctx=ctx if l == 0 else None)
        (gq, gk, gv, gog, ggf, ggb), (nq_, nk_, nvt), (aq, ak, avt, sg) = _projections(
            xall, mod, g[1], lw, rope_cos, rope_sin)
        y_gla = _gla(gq, gk, gv, gog, ggf, ggb, lw, ctx_len)
        y_nat = _nat(nq_, nk_, nvt, nat_bias, l, need_ctx)
        y_gqa = _gqa(aq, ak, avt, need_ctx)
        xall = _merge(xall, mod, (y_gla, y_nat, y_gqa), sg, lw["w_branch"], lw["w_out"], off)
        xall = _ffn(xall, mod, g[2], *lw["ffn"][1], sub=2, tile_off=off)
    return xall
```

```python
import functools

import numpy as np
import jax
import jax.numpy as jnp
from jax import lax
from jax.experimental import pallas as pl
from jax.experimental.pallas import tpu as pltpu

F32 = jnp.float32
BF16 = jnp.bfloat16

D_MODEL = 1024
GRID_W = 64
N_MOD = 9
D_FF = 2816
RMS_EPS = 1e-6
NEG_INF = -1e30

GLA_HEADS = 4
GLA_DK = 128
GLA_DV = 256
GLA_LR = 16
GLA_TAU = 16.0
GLA_CHUNK = 128

NAT_HEADS = 16
NAT_HD = 64
WIN_R = 8
WIN_C = 16
NAT_QROWS = 4
NAT_KBLOCKS = 3

GQA_HEADS = 8
GQA_KV_HEADS = 2
GQA_HD = 128
ROPE_BASE = 10000.0
GQA_KB = 512
GQA_SAMPLES = 2
LOG2E = 1.4426950408889634

TM = 256
LANES = 128
SUBLANES = 8
MXU = 256
VMEM_LIMIT = 56 * 1024 * 1024


def _dot(a, b):
    return jnp.dot(a, b, preferred_element_type=F32)


def _dot_nt(a, b):
    return lax.dot_general(a, b, (((1,), (1,)), ((), ())), preferred_element_type=F32)


def _dot_tn(a, b):
    return lax.dot_general(a, b, (((0,), (0,)), ((), ())), preferred_element_type=F32)


def _sigmoid(x):
    return 1.0 / (1.0 + jnp.exp(-x))


def _log_sigmoid(x):
    return -(jnp.maximum(-x, 0.0) + jnp.log(1.0 + jnp.exp(-jnp.abs(x))))


def _norm_mod(x, g, shift, scale):
    y = x * lax.rsqrt(jnp.mean(x * x, axis=-1, keepdims=True) + RMS_EPS) * g
    return y * (1.0 + scale) + shift


def _head_norm(t, gain, ones_bd, head_dim):
    sq = (t * t).astype(BF16)
    parts = [_dot(sq[:, i:i + MXU], ones_bd) for i in range(0, t.shape[1], MXU)]
    ss = parts[0] if len(parts) == 1 else jnp.concatenate(parts, axis=1)
    return t * lax.rsqrt(ss * (1.0 / head_dim) + RMS_EPS) * gain


def _params(semantics):
    return pltpu.CompilerParams(dimension_semantics=semantics, vmem_limit_bytes=VMEM_LIMIT)


def _whole(arr):
    zeros = (0,) * arr.ndim
    return pl.BlockSpec(arr.shape, lambda *_: zeros, pipeline_mode=pl.Buffered(1))


def _row_spec(off, width):
    return pl.BlockSpec((1, TM, width), lambda b, j: (b, j + off, 0))


def _mod_spec(col, off, ctx_row):
    return pl.BlockSpec((1, 1, D_MODEL), lambda b, j: (jnp.where(j + off == 0, ctx_row, b), 0, col))


def _mod_kernel(c_ref, w_ref, b_ref, o_ref):
    c = c_ref[...]
    s = (c * _sigmoid(c)).astype(BF16)
    o_ref[0] = _dot(s, w_ref[0].astype(BF16)) + b_ref[0]


def _mod_table(cvec, w_mod, b_mod):
    depth = w_mod.shape[0]
    rows = cvec.shape[0]
    width = 3 * D_MODEL
    return pl.pallas_call(
        _mod_kernel,
        grid=(depth, N_MOD * D_MODEL // width),
        in_specs=[pl.BlockSpec((rows, D_MODEL), lambda l, n: (0, 0)),
                  pl.BlockSpec((1, D_MODEL, width), lambda l, n: (l, 0, n)),
                  pl.BlockSpec((1, 1, width), lambda l, n: (l, 0, n))],
        out_specs=pl.BlockSpec((1, rows, width), lambda l, n: (l, 0, n)),
        out_shape=jax.ShapeDtypeStruct((depth, rows, N_MOD * D_MODEL), F32),
        compiler_params=_params(("arbitrary", "arbitrary")),
        name="mod_table",
    )(cvec, w_mod, b_mod.reshape(depth, 1, N_MOD * D_MODEL))


def _next_tile(b, j, nb, nj):
    wrap = j + 1 == nj
    last = jnp.logical_and(wrap, b + 1 == nb)
    return jnp.where(jnp.logical_and(wrap, jnp.logical_not(last)), b + 1, b), \
        jnp.where(wrap, jnp.where(last, j, 0), j + 1)


def _ffn_kernel(x_ref, ctx_ref, xn_ref, ctxn_ref, sh_ref, sc_ref, gt_ref, shn_ref, scn_ref, g_ref,
                wa_ref, wb_ref, wd_ref, o_ref, h_ref, *, split_input):
    b, j = pl.program_id(0), pl.program_id(1)

    def tile(x_r, ctx_r, jj):
        return jnp.where(jj == 0, ctx_r[0], x_r[0]) if split_input else x_r[0]

    def hidden(xt, shift, scale):
        u = _norm_mod(xt, g_ref[...], shift, scale).astype(BF16)
        a = _dot(u, wa_ref[...])
        return (a * _sigmoid(a) * _dot(u, wb_ref[...])).astype(BF16)

    @pl.when(jnp.logical_and(b == 0, j == 0))
    def _():
        h_ref[...] = hidden(tile(x_ref, ctx_ref, j), sh_ref[0], sc_ref[0])

    o_ref[0] = tile(x_ref, ctx_ref, j) + (0.5 * gt_ref[0]) * _dot(h_ref[...], wd_ref[...])
    _, jn = _next_tile(b, j, pl.num_programs(0), pl.num_programs(1))
    h_ref[...] = hidden(tile(xn_ref, ctxn_ref, jn), shn_ref[0], scn_ref[0])


def _ffn(xall, mod, g, wa, wb, wd, sub, tile_off, ctx=None):
    bsz, t, _ = xall.shape
    split = ctx is not None
    if split:
        t += ctx.shape[1]
    nj = t // TM
    ctx_row = mod.shape[0] - 1
    cur = lambda b, j: (b, j)
    nxt = lambda b, j: _next_tile(b, j, bsz, nj)

    def specs(at):
        if split:
            x_spec = pl.BlockSpec((1, TM, D_MODEL), lambda b, j: (at(b, j)[0], jnp.maximum(at(b, j)[1] - 1, 0), 0))
            ctx_spec = pl.BlockSpec((1, TM, D_MODEL), lambda b, j: (at(b, j)[0], 0, 0))
        else:
            x_spec = pl.BlockSpec((1, TM, D_MODEL), lambda b, j: (at(b, j)[0], at(b, j)[1], 0))
            ctx_spec = pl.BlockSpec((1, SUBLANES, D_MODEL), lambda b, j: (at(b, j)[0], 0, 0))
        return [x_spec, ctx_spec]

    def mod_spec(col, at):
        return pl.BlockSpec((1, 1, D_MODEL),
                            lambda b, j: (jnp.where(at(b, j)[1] + tile_off == 0, ctx_row, at(b, j)[0]), 0, col))

    if not split:
        ctx = xall
    return pl.pallas_call(
        functools.partial(_ffn_kernel, split_input=split),
        grid=(bsz, nj),
        in_specs=specs(cur) + specs(nxt) +
                 [mod_spec(3 * sub, cur), mod_spec(3 * sub + 1, cur), mod_spec(3 * sub + 2, cur),
                  mod_spec(3 * sub, nxt), mod_spec(3 * sub + 1, nxt),
                  _whole(g), _whole(wa), _whole(wb), _whole(wd)],
        out_specs=_row_spec(0, D_MODEL),
        out_shape=jax.ShapeDtypeStruct((bsz, t, D_MODEL), F32),
        scratch_shapes=[pltpu.VMEM((TM, wd.shape[0]), BF16)],
        compiler_params=_params(("arbitrary", "arbitrary")),
        name="ffn",
    )(xall, ctx, xall, ctx, mod, mod, mod, mod, mod, g, wa, wb, wd)


def _chunk_cumsum(tri, la):
    hi = la.astype(BF16)
    lo = (la - hi.astype(F32)).astype(BF16)
    return _dot(tri, hi) + _dot(tri, lo)


def _proj_gla_part(u, w_ref, wfg_ref, w2_ref, b2_ref, tril_ref, triu_ref,
                   q_ref, k_ref, v_ref, og_ref, gf_ref, gb_ref):
    nqk = GLA_HEADS * GLA_DK
    nv = GLA_HEADS * GLA_DV
    fg = _dot(u, wfg_ref[...]).astype(BF16)
    q_ref[0] = (_dot(u, w_ref[:, :nqk]) * GLA_DK ** -0.5).astype(BF16)
    z = _dot(fg, w2_ref[...]) + b2_ref[...]
    k_ref[0] = _dot(u, w_ref[:, nqk:2 * nqk]).astype(BF16)
    la = _log_sigmoid(z) * (1.0 / GLA_TAU)
    v_ref[0] = _dot(u, w_ref[:, 2 * nqk:2 * nqk + nv]).astype(BF16)
    gf_ref[0] = _chunk_cumsum(tril_ref[...], la[:, :nqk])
    gb_ref[0] = _chunk_cumsum(triu_ref[...], la[:, nqk:])
    og = _dot(u, w_ref[:, 2 * nqk + nv:])
    og_ref[0] = (og * _sigmoid(og)).astype(BF16)


def _proj_nat_part(u, wqk_ref, wvt_ref, gq_ref, gk_ref, ones_ref, q_ref, k_ref, vt_ref):
    p = _dot(u, wqk_ref[...])
    n = NAT_HEADS * NAT_HD
    ones_bd = ones_ref[...]
    q_ref[0] = (_head_norm(p[:, :n], gq_ref[...], ones_bd, NAT_HD) * (NAT_HD ** -0.5 * LOG2E)).astype(BF16)
    k_ref[0] = _head_norm(p[:, n:], gk_ref[...], ones_bd, NAT_HD).astype(BF16)
    vt_ref[0] = _dot_nt(wvt_ref[...], u).astype(BF16)


def _rope(t, cos, sin, lane):
    partner = jnp.where(lane % (GQA_HD // 2) < GQA_HD // 4,
                        pltpu.roll(t, GQA_HD - GQA_HD // 4, axis=1), pltpu.roll(t, GQA_HD // 4, axis=1))
    return t * cos + partner * sin


def _proj_gqa_part(u, wqk_ref, wvt_ref, wg_ref, gq_ref, gk_ref, ones_ref, cos_ref, sin_ref,
                   q_ref, k_ref, vt_ref, sg_ref):
    p = _dot(u, wqk_ref[...])
    nq = GQA_HEADS * GQA_HD
    ones_bd = ones_ref[...]
    qn = _head_norm(p[:, :nq], gq_ref[...], ones_bd, GQA_HD)
    kn = _head_norm(p[:, nq:], gk_ref[...], ones_bd, GQA_HD)
    cos = cos_ref[...]
    sin = sin_ref[...]
    lane = lax.broadcasted_iota(jnp.int32, (TM, GQA_HD), 1)
    for h in range(GQA_HEADS):
        sl = slice(h * GQA_HD, (h + 1) * GQA_HD)
        q_ref[0, :, sl] = (_rope(qn[:, sl], cos, sin, lane) * (GQA_HD ** -0.5 * LOG2E)).astype(BF16)
    for h in range(GQA_KV_HEADS):
        sl = slice(h * GQA_HD, (h + 1) * GQA_HD)
        k_ref[0, :, sl] = _rope(kn[:, sl], cos, sin, lane).astype(BF16)
    vt_ref[0] = _dot_nt(wvt_ref[...], u).astype(BF16)
    sg_ref[0] = _sigmoid(_dot(u, wg_ref[...])).astype(BF16)


N_PROJ_GLA_IN, N_PROJ_NAT_IN, N_PROJ_GQA_IN = 6, 5, 8
N_PROJ_GLA_OUT, N_PROJ_NAT_OUT, N_PROJ_GQA_OUT = 6, 3, 4


def _proj_kernel(*refs):
    it = iter(refs)
    take = lambda count: [next(it) for _ in range(count)]
    x_ref, sh_ref, sc_ref, g_ref = take(4)
    gla_in, nat_in, gqa_in = take(N_PROJ_GLA_IN), take(N_PROJ_NAT_IN), take(N_PROJ_GQA_IN)
    gla_out, nat_out, gqa_out = take(N_PROJ_GLA_OUT), take(N_PROJ_NAT_OUT), take(N_PROJ_GQA_OUT)
    u_ref, = take(1)

    @pl.when(pl.program_id(0) == 0)
    def _():
        u_ref[...] = jnp.zeros_like(u_ref)

    u_prev = u_ref[...]
    _proj_nat_part(u_prev, *nat_in, *nat_out)
    u = _norm_mod(x_ref[0], g_ref[...], sh_ref[0], sc_ref[0]).astype(BF16)
    _proj_gqa_part(u_prev, *gqa_in, *gqa_out)
    _proj_gla_part(u, *gla_in, *gla_out)
    u_ref[...] = u


def _projections(xall, mod, g, lw, rope_cos, rope_sin):
    bsz, t, _ = xall.shape
    ctx_row = mod.shape[0] - 1
    nt = t // TM
    n_tiles = bsz * nt
    lead = lambda s: jnp.minimum(s, n_tiles - 1)
    lag = lambda s: jnp.maximum(s - 1, 0)

    def rows(at, width):
        return pl.BlockSpec((1, TM, width), lambda s: (at(s) // nt, at(s) % nt, 0))

    def cols(at, height):
        return pl.BlockSpec((1, height, TM), lambda s: (at(s) // nt, 0, at(s) % nt))

    def mod_spec(col):
        return pl.BlockSpec((1, 1, D_MODEL),
                            lambda s: (jnp.where(lead(s) % nt == 0, ctx_row, lead(s) // nt), 0, col))

    sds = jax.ShapeDtypeStruct
    nqk = GLA_HEADS * GLA_DK
    nv = GLA_HEADS * GLA_DV
    n = NAT_HEADS * NAT_HD
    nq = GQA_HEADS * GQA_HD
    nkv = GQA_KV_HEADS * GQA_HD
    tab = pl.BlockSpec((TM, GQA_HD), lambda s: (lag(s) % nt, 0))
    gla_in = [lw["w_gla"], lw["w_fg"], lw["w2"], lw["b2"], lw["tril"], lw["triu"]]
    nat_in = [lw["w_nat_qk"], lw["w_nat_vt"], lw["nat_gq"], lw["nat_gk"], lw["ones64"]]
    gqa_in = [lw["w_gqa_qk"], lw["w_gqa_vt"], lw["w_gates"], lw["gqa_gq"], lw["gqa_gk"], lw["ones128"]]
    assert (len(gla_in), len(nat_in), len(gqa_in) + 2) == (N_PROJ_GLA_IN, N_PROJ_NAT_IN, N_PROJ_GQA_IN)
    outs = pl.pallas_call(
        _proj_kernel, grid=(n_tiles + 1,),
        in_specs=[rows(lead, D_MODEL), mod_spec(3), mod_spec(4), _whole(g)]
                 + [_whole(a) for a in gla_in + nat_in + gqa_in] + [tab, tab],
        out_specs=[rows(lead, nqk), rows(lead, nqk), rows(lead, nv), rows(lead, nv), rows(lead, nqk), rows(lead, nqk),
                   rows(lag, n), rows(lag, n), cols(lag, n),
                   rows(lag, nq), rows(lag, nkv), cols(lag, nkv), rows(lag, 3 * D_MODEL)],
        out_shape=[sds((bsz, t, nqk), BF16), sds((bsz, t, nqk), BF16), sds((bsz, t, nv), BF16),
                   sds((bsz, t, nv), BF16), sds((bsz, t, nqk), F32), sds((bsz, t, nqk), F32),
                   sds((bsz, t, n), BF16), sds((bsz, t, n), BF16), sds((bsz, n, t), BF16),
                   sds((bsz, t, nq), BF16), sds((bsz, t, nkv), BF16), sds((bsz, nkv, t), BF16),
                   sds((bsz, t, 3 * D_MODEL), BF16)],
        scratch_shapes=[pltpu.VMEM((TM, D_MODEL), BF16)],
        compiler_params=_params(("arbitrary",)), name="proj",
    )(xall, mod, mod, g, *gla_in, *nat_in, *gqa_in, rope_cos, rope_sin)
    return outs[:6], outs[6:9], outs[9:]


GLA_UNROLL = 16


def _gla_chain(n, forward, q_ref, k_ref, v_ref, g_ref, o_ref, state, tri, emit):
    c = GLA_CHUNK
    rows = pl.ds(pl.multiple_of(jnp.int32(n) * c, c), c)
    g = g_ref[0, rows, :]
    if forward:
        g_mid = g[c // 2 - 1:c // 2, :]
        g_edge = g[c - 1:c, :]
    else:
        g_mid = g[c // 2:c // 2 + 1, :]
        g_edge = g[0:1, :]
    q = q_ref[0, rows, :].astype(F32)
    k = k_ref[0, rows, :].astype(F32)
    v = v_ref[0, rows, :]
    q_in = (q * jnp.exp(g - g_mid)).astype(BF16)
    k_in = (k * jnp.exp(g_mid - g)).astype(BF16)
    q_x = (q * jnp.exp(g)).astype(BF16)
    k_end = (k * jnp.exp(g_edge - g)).astype(BF16)
    decay = jnp.exp(g_edge)
    yield
    att = _dot_nt(q_in, k_in)
    s_in = state["s"]
    state["s"] = s_in * decay + _dot_tn(v, k_end)
    yield
    att = jnp.where(tri, att, 0.0).astype(BF16)
    o = _dot(att, v) + _dot_nt(q_x, s_in.astype(BF16))
    yield
    if emit is None:
        o_ref[rows, :] = o
    else:
        emit(rows, o)


def _gla_kernel(q_ref, k_ref, v_ref, og_ref, gf_ref, gb_ref, gain_ref, y_ref,
                of_ref, ob_ref, sf_ref, sb_ref, *, n_chunks, n_ctx_chunks):
    c = GLA_CHUNK
    ri = lax.broadcasted_iota(jnp.int32, (c, c), 0)
    ci = lax.broadcasted_iota(jnp.int32, (c, c), 1)
    lower = ri >= ci
    upper = ci >= ri
    sf_ref[...] = jnp.zeros_like(sf_ref)
    sb_ref[...] = jnp.zeros_like(sb_ref)

    def finish(rows, o):
        y = o * lax.rsqrt(jnp.mean(o * o, axis=-1, keepdims=True) + RMS_EPS) * gain_ref[...]
        y_ref[0, rows, :] = (y * og_ref[0, rows, :].astype(F32)).astype(BF16)

    def scan(it, first, unroll, completing):
        fwd = {"s": sf_ref[...]}
        bwd = {"s": sb_ref[...]}
        emit_f = (lambda rows, o: finish(rows, o + ob_ref[rows, :])) if completing else None
        emit_b = (lambda rows, o: finish(rows, o + of_ref[rows, :])) if completing else None
        chains = []
        for u in range(unroll):
            s = first + it * unroll + u
            nb = jnp.where(s < n_ctx_chunks, n_ctx_chunks - 1 - s, n_chunks - 1 - (s - n_ctx_chunks))
            chains.append(_gla_chain(s, True, q_ref, k_ref, v_ref, gf_ref, of_ref, fwd, lower, emit_f))
            chains.append(_gla_chain(nb, False, q_ref, k_ref, v_ref, gb_ref, ob_ref, bwd, upper, emit_b))
        while chains:
            alive = []
            for ch in chains:
                try:
                    next(ch)
                    alive.append(ch)
                except StopIteration:
                    pass
            chains = alive
        sf_ref[...] = fwd["s"]
        sb_ref[...] = bwd["s"]

    half = (n_chunks - n_ctx_chunks) // 2
    scan(0, 0, n_ctx_chunks, False)
    for n in range(n_ctx_chunks):
        rows = pl.ds(n * c, c)
        finish(rows, of_ref[rows, :] + ob_ref[rows, :])
    lax.fori_loop(0, half // GLA_UNROLL,
                  lambda it, carry: scan(it, n_ctx_chunks, GLA_UNROLL, False) or carry, 0)
    lax.fori_loop(0, half // GLA_UNROLL,
                  lambda it, carry: scan(it, n_ctx_chunks + half, GLA_UNROLL, True) or carry, 0)


def _gla(q, k, v, og, gf, gb, lw, ctx_len):
    bsz, t, _ = q.shape
    assert (t - ctx_len) % (2 * GLA_UNROLL * GLA_CHUNK) == 0 and ctx_len % GLA_CHUNK == 0
    seq_spec = lambda w: pl.BlockSpec((1, t, w), lambda b, h: (b, 0, h))
    kern = functools.partial(_gla_kernel, n_chunks=t // GLA_CHUNK, n_ctx_chunks=ctx_len // GLA_CHUNK)
    return pl.pallas_call(
        kern, grid=(bsz, GLA_HEADS),
        in_specs=[seq_spec(GLA_DK), seq_spec(GLA_DK), seq_spec(GLA_DV), seq_spec(GLA_DV),
                  seq_spec(GLA_DK), seq_spec(GLA_DK), pl.BlockSpec((1, GLA_DV), lambda b, h: (0, 0))],
        out_specs=seq_spec(GLA_DV),
        out_shape=jax.ShapeDtypeStruct((bsz, t, GLA_HEADS * GLA_DV), BF16),
        scratch_shapes=[pltpu.VMEM((t, GLA_DV), F32), pltpu.VMEM((t, GLA_DV), F32),
                        pltpu.VMEM((GLA_DV, GLA_DK), F32), pltpu.VMEM((GLA_DV, GLA_DK), F32)],
        compiler_params=_params(("parallel", "parallel")), name="gla",
    )(q, k, v, og, gf, gb, lw["gla_gain"])


NAT_GROUP = 8
NAT_SAMPLES = 2


def _nat_kernel(q_ref, k0_ref, k1_ref, k2_ref, kc_ref, v0_ref, v1_ref, v2_ref, vc_ref, bias_ref, o_ref, s_ref,
                *, off):
    is_ctx = pl.program_id(0) + off == 0
    lane = lax.broadcasted_iota(jnp.int32, (TM, LANES), 1)
    row = lax.broadcasted_iota(jnp.int32, (LANES, TM), 0)
    n_heads = 2 * NAT_GROUP
    n_lat = NAT_KBLOCKS * TM

    def run(lat_keys, lat_vals):
        kcats, vlats = {}, {}

        def scores(c):
            smp, i = divmod(c, n_heads)
            hp, e = divmod(i, 2)
            sl = slice(hp * LANES, (hp + 1) * LANES)
            if (smp, hp) not in kcats:
                refs = lat_keys + [kc_ref]
                half = (len(refs) + 1) // 2
                kcats[smp, hp] = [jnp.concatenate([r[smp, :, sl] for r in part], axis=0)
                                  for part in (refs[:half], refs[half:]) if part]
            q = q_ref[smp, :, sl]
            qm = jnp.where((lane >= NAT_HD) if e else (lane < NAT_HD), q, jnp.zeros_like(q))
            at = 0
            for kpart in kcats[smp, hp]:
                s_ref[c % 2, at:at + kpart.shape[0], :] = _dot_nt(kpart, qm)
                at += kpart.shape[0]

        scores(0)
        outs = []
        for c in range(NAT_SAMPLES * n_heads):
            if c + 1 < NAT_SAMPLES * n_heads:
                scores(c + 1)
            smp, i = divmod(c, n_heads)
            hp, e = divmod(i, 2)
            sl = slice(hp * LANES, (hp + 1) * LANES)
            n_l = n_lat if lat_keys else 0
            s_ctx = s_ref[c % 2, n_l:n_l + TM, :]
            m = jnp.max(s_ctx, axis=0, keepdims=True)
            if lat_keys:
                s_lat = s_ref[c % 2, :n_lat, :] + bias_ref[i, 0]
                m = jnp.maximum(m, jnp.max(s_lat, axis=0, keepdims=True))
            def own_rows(v):
                vrow = lax.broadcasted_iota(jnp.int32, v.shape, 0)
                return jnp.where((vrow >= NAT_HD) if e else (vrow < NAT_HD), v, jnp.ones_like(v))

            p_ctx = jnp.exp2(s_ctx - m)
            o = _dot(own_rows(vc_ref[smp, sl, :]), p_ctx.astype(BF16))
            if lat_keys:
                if (smp, hp) not in vlats:
                    vlats[smp, hp] = jnp.concatenate([r[smp, sl, :] for r in lat_vals], axis=1)
                p_lat = jnp.exp2(s_lat - m)
                o = o + _dot(own_rows(vlats[smp, hp]), p_lat.astype(BF16))
            l = o[0:1, :] if e else o[LANES - 1:LANES, :]
            outs.append(o * (1.0 / l))
            if e == 1:
                ot = jnp.where(row < NAT_HD, outs[-2], outs[-1])
                o_ref[smp, :, sl] = ot.T.astype(BF16)

    @pl.when(is_ctx)
    def _():
        run([], [])

    @pl.when(jnp.logical_not(is_ctx))
    def _():
        run([k0_ref, k1_ref, k2_ref], [v0_ref, v1_ref, v2_ref])


def _nat(q, k, vt, bias, layer, need_ctx):
    bsz, t, n = q.shape
    off = 0 if need_ctx else 1
    nblk = t // TM - 1
    gw = NAT_GROUP * LANES
    ngroups = n // gw

    def kblock(qi, i):
        j = qi + off - 1
        return 1 + jnp.clip(j - 1, 0, nblk - NAT_KBLOCKS) + i

    def bias_class(qi):
        j = qi + off - 1
        return jnp.where(j <= 0, 0, jnp.where(j == nblk - 1, 2, 1))

    ns = NAT_SAMPLES
    assert bsz % ns == 0
    kspec = lambda i: pl.BlockSpec((ns, TM, gw), lambda qi, hg, b: (b, kblock(qi, i), hg))
    vspec = lambda i: pl.BlockSpec((ns, gw, TM), lambda qi, hg, b: (b, hg, kblock(qi, i)))
    return pl.pallas_call(
        functools.partial(_nat_kernel, off=off),
        grid=(t // TM - off, ngroups, bsz // ns),
        in_specs=[pl.BlockSpec((ns, TM, gw), lambda qi, hg, b: (b, qi + off, hg)),
                  kspec(0), kspec(1), kspec(2),
                  pl.BlockSpec((ns, TM, gw), lambda qi, hg, b: (b, 0, hg)),
                  vspec(0), vspec(1), vspec(2),
                  pl.BlockSpec((ns, gw, TM), lambda qi, hg, b: (b, hg, 0)),
                  pl.BlockSpec((2 * NAT_GROUP, 1, NAT_KBLOCKS * TM, TM),
                               lambda qi, hg, b: (layer * ngroups + hg, bias_class(qi), 0, 0),
                               pipeline_mode=pl.Buffered(1))],
        out_specs=pl.BlockSpec((ns, TM, gw), lambda qi, hg, b: (b, qi, hg)),
        out_shape=jax.ShapeDtypeStruct((bsz, t - off * TM, n), BF16),
        scratch_shapes=[pltpu.VMEM((2, (NAT_KBLOCKS + 1) * TM, TM), F32)],
        compiler_params=_params(("parallel", "parallel", "parallel")), name="nat",
    )(q, k, k, k, k, vt, vt, vt, vt, bias)


def _nat_bias_tables(rpb, rows):
    nblk = rows // NAT_QROWS
    kr_win = min(WIN_R, rows)
    col = np.arange(GRID_W)
    c_start = np.clip(col - WIN_C // 2, 0, GRID_W - WIN_C)
    cmask = (col[:, None] >= c_start[None, :]) & (col[:, None] < c_start[None, :] + WIN_C)
    dc = np.clip(col[:, None] - col[None, :] + WIN_C - 1, 0, 2 * WIN_C - 2)
    drs, rmasks = [], []
    for jblk in (0, 1, nblk - 1):
        base = int(np.clip(jblk - 1, 0, nblk - NAT_KBLOCKS))
        r = NAT_QROWS * jblk + np.arange(NAT_QROWS)
        r_start = np.clip(r - kr_win // 2, 0, rows - kr_win)
        kr = NAT_QROWS * base + np.arange(NAT_KBLOCKS * NAT_QROWS)
        rmasks.append((kr[:, None] >= r_start[None, :]) & (kr[:, None] < r_start[None, :] + kr_win))
        drs.append(np.clip(kr[:, None] - r[None, :] + WIN_R - 1, 0, 2 * WIN_R - 2))
    drx = np.where(np.stack(rmasks), np.stack(drs), 2 * WIN_R - 1)
    dcx = np.where(cmask, dc, 2 * WIN_C - 1)
    nh = rpb.shape[0]
    ext = jnp.full((nh, 2 * WIN_R, 2 * WIN_C), NEG_INF, F32)
    ext = ext.at[:, :2 * WIN_R - 1, :2 * WIN_C - 1].set(rpb.astype(F32) * LOG2E)
    pick = np.tile(dcx[:, None, :], (1, NAT_QROWS, 1)).reshape(-1)
    onehot = jnp.asarray(np.arange(2 * WIN_C)[:, None] == pick[None, :], F32)
    a = jnp.einsum("hdj,jn->hdn", ext, onehot, precision=lax.Precision.HIGHEST)
    a = a.reshape(nh, 2 * WIN_R, GRID_W, NAT_QROWS * GRID_W)
    nk = NAT_KBLOCKS * NAT_QROWS * GRID_W
    return pl.pallas_call(
        functools.partial(_bias_kernel, drx=drx),
        grid=(nh,),
        in_specs=[pl.BlockSpec((1,) + a.shape[1:], lambda h: (h, 0, 0, 0))],
        out_specs=pl.BlockSpec((1, 3, nk, TM), lambda h: (h, 0, 0, 0)),
        out_shape=jax.ShapeDtypeStruct((nh, 3, nk, TM), F32),
        compiler_params=_params(("parallel",)), name="nat_bias",
    )(a)


def _bias_kernel(a_ref, o_ref, *, drx):
    lane_blk = lax.broadcasted_iota(jnp.int32, (GRID_W, TM), 1) // GRID_W
    n_cls, n_kri, n_ri = drx.shape
    for cls in range(n_cls):
        for kri in range(n_kri):
            strip = a_ref[0, int(drx[cls, kri, n_ri - 1])]
            for ri in range(n_ri - 1):
                strip = jnp.where(lane_blk == ri, a_ref[0, int(drx[cls, kri, ri])], strip)
            o_ref[0, cls, kri * GRID_W:(kri + 1) * GRID_W, :] = strip


def _gqa_kernel(q_ref, k_ref, vt_ref, o_ref, s_ref, acc_ref, *, off, n_lat):
    is_ctx = pl.program_id(1) + off == 0
    group = GQA_HEADS // GQA_KV_HEADS

    def run(n_blocks):
        qs = {}

        n_chains = GQA_SAMPLES * GQA_KV_HEADS

        def scores(c, start, size, slot):
            smp, h = divmod(c, GQA_KV_HEADS)
            if c not in qs:
                heads = range(h * group, (h + 1) * group)
                qs[c] = jnp.concatenate([q_ref[smp, :, g * GQA_HD:(g + 1) * GQA_HD] for g in heads], axis=0)
            s_ref[c % 2, slot, :size, :] = _dot_nt(k_ref[smp, start:start + size, h * GQA_HD:(h + 1) * GQA_HD], qs[c])

        def start_chain(c):
            scores(c, 0, TM, 0)
            if n_blocks:
                scores(c, TM, GQA_KB, 1)

        start_chain(0)
        for c in range(n_chains):
            smp, h = divmod(c, GQA_KV_HEADS)
            rows = slice(h * GQA_HD, (h + 1) * GQA_HD)
            if n_blocks == 0 and c + 1 < n_chains:
                start_chain(c + 1)
            st = s_ref[c % 2, 0, :TM, :]
            m = jnp.max(st, axis=0, keepdims=True)
            p = jnp.exp2(st - m)
            l = jnp.sum(p, axis=0, keepdims=True)
            acc_ref[c % 2] = _dot(vt_ref[smp, rows, :TM], p.astype(BF16))
            for j in range(n_blocks):
                start = TM + j * GQA_KB
                if j + 1 < n_blocks:
                    scores(c, start + GQA_KB, GQA_KB, j % 2)
                elif c + 1 < n_chains:
                    start_chain(c + 1)
                st = s_ref[c % 2, (j + 1) % 2]
                m_new = jnp.maximum(m, jnp.max(st, axis=0, keepdims=True))
                alpha = jnp.exp2(m - m_new)
                p = jnp.exp2(st - m_new)
                l = alpha * l + jnp.sum(p, axis=0, keepdims=True)
                acc_ref[c % 2] = alpha * acc_ref[c % 2] + _dot(vt_ref[smp, rows, start:start + GQA_KB],
                                                               p.astype(BF16))
                m = m_new
            ot = acc_ref[c % 2] * (1.0 / l)
            for g in range(group):
                cols = slice((h * group + g) * GQA_HD, (h * group + g + 1) * GQA_HD)
                o_ref[smp, :, cols] = ot[:, g * TM:(g + 1) * TM].T.astype(BF16)

    @pl.when(is_ctx)
    def _():
        run(0)

    @pl.when(jnp.logical_not(is_ctx))
    def _():
        run(n_lat)


def _gqa(q, k, vt, need_ctx):
    bsz, t, nq = q.shape
    nkv = k.shape[2]
    off = 0 if need_ctx else 1
    group = GQA_HEADS // GQA_KV_HEADS
    ns = GQA_SAMPLES
    assert (t - TM) % GQA_KB == 0 and bsz % ns == 0
    return pl.pallas_call(
        functools.partial(_gqa_kernel, off=off, n_lat=(t - TM) // GQA_KB),
        grid=(bsz // ns, t // TM - off),
        in_specs=[pl.BlockSpec((ns, TM, nq), lambda b, qi: (b, qi + off, 0)),
                  pl.BlockSpec((ns, t, nkv), lambda b, qi: (b, 0, 0)),
                  pl.BlockSpec((ns, nkv, t), lambda b, qi: (b, 0, 0))],
        out_specs=pl.BlockSpec((ns, TM, nq), lambda b, qi: (b, qi, 0)),
        out_shape=jax.ShapeDtypeStruct((bsz, t - off * TM, nq), BF16),
        scratch_shapes=[pltpu.VMEM((2, 2, GQA_KB, group * TM), F32),
                        pltpu.VMEM((2, GQA_HD, group * TM), F32)],
        compiler_params=_params(("parallel", "arbitrary")), name="gqa",
    )(q, k, vt)


def _merge_kernel(x_ref, gt_ref, y0_ref, y1_ref, y2_ref, sg_ref, wb_ref, wo_ref, o_ref):
    z = None
    for i, y_ref in enumerate((y0_ref, y1_ref, y2_ref)):
        zi = sg_ref[0, :, i * D_MODEL:(i + 1) * D_MODEL].astype(F32) * _dot(y_ref[0], wb_ref[i])
        z = zi if z is None else z + zi
    o_ref[0] = x_ref[0] + gt_ref[0] * _dot(z.astype(BF16), wo_ref[...])


def _merge(xall, mod, ys, sg, wb, wo, off):
    bsz, t, _ = xall.shape
    ctx_row = mod.shape[0] - 1
    return pl.pallas_call(
        _merge_kernel,
        grid=(bsz, t // TM - off),
        in_specs=[_row_spec(off, D_MODEL), _mod_spec(5, off, ctx_row),
                  _row_spec(off, D_MODEL), _row_spec(0, D_MODEL), _row_spec(0, D_MODEL),
                  _row_spec(off, 3 * D_MODEL), _whole(wb), _whole(wo)],
        out_specs=_row_spec(0, D_MODEL),
        out_shape=jax.ShapeDtypeStruct((bsz, t - off * TM, D_MODEL), F32),
        compiler_params=_params(("parallel", "parallel")), name="merge",
    )(xall, mod, ys[0], ys[1], ys[2], sg, wb, wo)


def _block_diag_ones(head_dim):
    i = np.arange(MXU)
    return jnp.asarray((i[:, None] // head_dim) == (i[None, :] // head_dim), dtype=BF16)


def _chunk_tri(upper):
    i = np.arange(TM)
    same = (i[:, None] // GLA_CHUNK) == (i[None, :] // GLA_CHUNK)
    tri = (i[None, :] >= i[:, None]) if upper else (i[:, None] >= i[None, :])
    return jnp.asarray(same & tri, dtype=BF16)


def _rope_tables(ctx_len, seq):
    quarter = GQA_HD // 4
    freqs = ROPE_BASE ** (-np.arange(quarter, dtype=np.float64) / quarter)
    tok = np.arange(seq)
    ang_r = (tok // GRID_W)[:, None] * freqs
    ang_c = (tok % GRID_W)[:, None] * freqs
    ang = np.concatenate([ang_r, ang_r, ang_c, ang_c], axis=1)
    sign = np.tile(np.concatenate([-np.ones(quarter), np.ones(quarter)]), 2)
    cos = np.concatenate([np.ones((ctx_len, GQA_HD)), np.cos(ang)], axis=0)
    sin = np.concatenate([np.zeros((ctx_len, GQA_HD)), np.sin(ang) * sign], axis=0)
    return jnp.asarray(cos, F32), jnp.asarray(sin, F32)


def _layer_weights(l, ffn_w_in, ffn_w_out, w_in, gla_fg_w2, gla_fg_b, gla_norm_g, nat_q_norm, nat_k_norm,
                   gqa_q_norm, gqa_k_norm, w_branch, w_out):
    w = w_in[l]
    nqk = GLA_HEADS * GLA_DK
    nv = GLA_HEADS * GLA_DV
    o_fg = 2 * nqk + 2 * nv
    o_nat = o_fg + 2 * GLA_LR
    n = NAT_HEADS * NAT_HD
    o_gqa = o_nat + 3 * n
    nq = GQA_HEADS * GQA_HD
    nkv = GQA_KV_HEADS * GQA_HD
    o_gate = o_gqa + nq + 2 * nkv
    pad = jnp.zeros((D_MODEL, LANES - 2 * GLA_LR), w.dtype)
    w2 = gla_fg_w2[l]
    zeros_lr = jnp.zeros((GLA_LR, nqk), w2.dtype)
    zeros_rest = jnp.zeros((LANES - 2 * GLA_LR, nqk), w2.dtype)
    return {
        "ffn": [(ffn_w_in[l, i, :, :D_FF].astype(BF16), ffn_w_in[l, i, :, D_FF:].astype(BF16),
                 ffn_w_out[l, i].astype(BF16)) for i in range(2)],
        "w_gla": w[:, :o_fg].astype(BF16),
        "w_fg": jnp.concatenate([w[:, o_fg:o_nat], pad], axis=1).astype(BF16),
        "w_nat_qk": w[:, o_nat:o_nat + 2 * n].astype(BF16),
        "w_nat_vt": w[:, o_nat + 2 * n:o_gqa].T.astype(BF16),
        "w_gqa_qk": w[:, o_gqa:o_gqa + nq + nkv].astype(BF16),
        "w_gqa_vt": w[:, o_gqa + nq + nkv:o_gate].T.astype(BF16),
        "w_gates": w[:, o_gate:].astype(BF16),
        "w2": jnp.concatenate([jnp.concatenate([w2[0], zeros_lr, zeros_rest], axis=0),
                               jnp.concatenate([zeros_lr, w2[1], zeros_rest], axis=0)], axis=1).astype(BF16),
        "b2": gla_fg_b[l].reshape(1, 2 * nqk),
        "tril": _chunk_tri(False),
        "triu": _chunk_tri(True),
        "gla_gain": gla_norm_g[l].reshape(1, GLA_DV),
        "nat_gq": jnp.tile(nat_q_norm[l], NAT_HEADS).reshape(1, n),
        "nat_gk": jnp.tile(nat_k_norm[l], NAT_HEADS).reshape(1, n),
        "gqa_gq": jnp.tile(gqa_q_norm[l], GQA_HEADS).reshape(1, nq),
        "gqa_gk": jnp.tile(gqa_k_norm[l], GQA_KV_HEADS).reshape(1, nkv),
        "ones64": _block_diag_ones(NAT_HD),
        "ones128": _block_diag_ones(GQA_HD),
        "w_branch": w_branch[l].astype(BF16),
        "w_out": w_out[l].astype(BF16),
    }


def kernel(x, c, ctx, c_ctx, w_mod, b_mod, norm_g, ffn_w_in, ffn_w_out, w_in, gla_fg_w2, gla_fg_b, gla_norm_g,
           nat_q_norm, nat_k_norm, nat_rpb, gqa_q_norm, gqa_k_norm, w_branch, w_out):
    bsz, seq, _ = x.shape
    ctx_len = ctx.shape[1]
    depth = w_mod.shape[0]
    assert seq % TM == 0 and ctx_len == TM and seq // TM >= NAT_KBLOCKS

    mod_rows = -(-(bsz + 1) // SUBLANES) * SUBLANES
    cvec = jnp.concatenate([c, c_ctx[None, :], jnp.zeros((mod_rows - bsz - 1, D_MODEL), c.dtype)], axis=0)
    mods = _mod_table(cvec, w_mod, b_mod)[:, :bsz + 1].reshape(depth, bsz + 1, 1, N_MOD * D_MODEL)
    rope_cos, rope_sin = _rope_tables(ctx_len, seq)
    nat_bias = _nat_bias_tables(nat_rpb.reshape((-1,) + nat_rpb.shape[2:]), seq // GRID_W)

    xall = x
    for l in range(depth):
        need_ctx = l < depth - 1
        off = 0 if need_ctx else 1
        lw = _layer_weights(l, ffn_w_in, ffn_w_out, w_in, gla_fg_w2, gla_fg_b, gla_norm_g, nat_q_norm,
                            nat_k_norm, gqa_q_norm, gqa_k_norm, w_branch, w_out)
        mod = mods[l]
        g = [norm_g[l, i].reshape(1, D_MODEL) for i in range(3)]
        xall = _ffn(xall, mod, g[0], *lw["ffn"][0], sub=0, tile_off=0, ctx=ctx if l == 0 else None)
        (gq, gk, gv, gog, ggf, ggb), (nq_, nk_, nvt), (aq, ak, avt, sg) = _projections(
            xall, mod, g[1], lw, rope_cos, rope_sin)
        y_gla = _gla(gq, gk, gv, gog, ggf, ggb, lw, ctx_len)
        y_nat = _nat(nq_, nk_, nvt, nat_bias, l, need_ctx)
        y_gqa = _gqa(aq, ak, avt, need_ctx)
        xall = _merge(xall, mod, (y_gla, y_nat, y_gqa), sg, lw["w_branch"], lw["w_out"], off)
        xall = _ffn(xall, mod, g[2], *lw["ffn"][1], sub=2, tile_off=off)
    return xall
```

```python
import functools

import numpy as np
import jax
import jax.numpy as jnp
from jax import lax
from jax.experimental import pallas as pl
from jax.experimental.pallas import tpu as pltpu

F32 = jnp.float32
BF16 = jnp.bfloat16

D_MODEL = 1024
GRID_W = 64
N_MOD = 9
D_FF = 2816
RMS_EPS = 1e-6
NEG_INF = -1e30

GLA_HEADS = 4
GLA_DK = 128
GLA_DV = 256
GLA_LR = 16
GLA_TAU = 16.0
GLA_CHUNK = 128

NAT_HEADS = 16
NAT_HD = 64
WIN_R = 8
WIN_C = 16
NAT_QROWS = 4
NAT_KBLOCKS = 3

GQA_HEADS = 8
GQA_KV_HEADS = 2
GQA_HD = 128
ROPE_BASE = 10000.0
GQA_KB = 512
GQA_SAMPLES = 2
LOG2E = 1.4426950408889634

TM = 256
LANES = 128
SUBLANES = 8
MXU = 256
VMEM_LIMIT = 56 * 1024 * 1024


def _dot(a, b):
    return jnp.dot(a, b, preferred_element_type=F32)


def _dot_nt(a, b):
    return lax.dot_general(a, b, (((1,), (1,)), ((), ())), preferred_element_type=F32)


def _dot_tn(a, b):
    return lax.dot_general(a, b, (((0,), (0,)), ((), ())), preferred_element_type=F32)


def _sigmoid(x):
    return 1.0 / (1.0 + jnp.exp(-x))


def _log_sigmoid(x):
    return -(jnp.maximum(-x, 0.0) + jnp.log(1.0 + jnp.exp(-jnp.abs(x))))


def _norm_mod(x, g, shift, scale):
    y = x * lax.rsqrt(jnp.mean(x * x, axis=-1, keepdims=True) + RMS_EPS) * g
    return y * (1.0 + scale) + shift


def _head_norm(t, gain, ones_bd, head_dim):
    sq = (t * t).astype(BF16)
    parts = [_dot(sq[:, i:i + MXU], ones_bd) for i in range(0, t.shape[1], MXU)]
    ss = parts[0] if len(parts) == 1 else jnp.concatenate(parts, axis=1)
    return t * lax.rsqrt(ss * (1.0 / head_dim) + RMS_EPS) * gain


def _params(semantics):
    return pltpu.CompilerParams(dimension_semantics=semantics, vmem_limit_bytes=VMEM_LIMIT)


def _whole(arr):
    zeros = (0,) * arr.ndim
    return pl.BlockSpec(arr.shape, lambda *_: zeros, pipeline_mode=pl.Buffered(1))


def _row_spec(off, width):
    return pl.BlockSpec((1, TM, width), lambda b, j: (b, j + off, 0))


def _mod_spec(col, off, ctx_row):
    return pl.BlockSpec((1, 1, D_MODEL), lambda b, j: (jnp.where(j + off == 0, ctx_row, b), 0, col))


def _mod_kernel(c_ref, w_ref, b_ref, o_ref):
    c = c_ref[...]
    s = (c * _sigmoid(c)).astype(BF16)
    o_ref[0] = _dot(s, w_ref[0].astype(BF16)) + b_ref[0]


def _mod_table(cvec, w_mod, b_mod):
    depth = w_mod.shape[0]
    rows = cvec.shape[0]
    width = 3 * D_MODEL
    return pl.pallas_call(
        _mod_kernel,
        grid=(depth, N_MOD * D_MODEL // width),
        in_specs=[pl.BlockSpec((rows, D_MODEL), lambda l, n: (0, 0)),
                  pl.BlockSpec((1, D_MODEL, width), lambda l, n: (l, 0, n)),
                  pl.BlockSpec((1, 1, width), lambda l, n: (l, 0, n))],
        out_specs=pl.BlockSpec((1, rows, width), lambda l, n: (l, 0, n)),
        out_shape=jax.ShapeDtypeStruct((depth, rows, N_MOD * D_MODEL), F32),
        compiler_params=_params(("arbitrary", "arbitrary")),
        name="mod_table",
    )(cvec, w_mod, b_mod.reshape(depth, 1, N_MOD * D_MODEL))


def _next_tile(b, j, nb, nj):
    wrap = j + 1 == nj
    last = jnp.logical_and(wrap, b + 1 == nb)
    return jnp.where(jnp.logical_and(wrap, jnp.logical_not(last)), b + 1, b), \
        jnp.where(wrap, jnp.where(last, j, 0), j + 1)


def _ffn_kernel(x_ref, ctx_ref, xn_ref, ctxn_ref, sh_ref, sc_ref, gt_ref, shn_ref, scn_ref, g_ref,
                wa_ref, wb_ref, wd_ref, o_ref, h_ref, *, split_input):
    b, j = pl.program_id(0), pl.program_id(1)

    def tile(x_r, ctx_r, jj):
        return jnp.where(jj == 0, ctx_r[0], x_r[0]) if split_input else x_r[0]

    def hidden(xt, shift, scale):
        u = _norm_mod(xt, g_ref[...], shift, scale).astype(BF16)
        a = _dot(u, wa_ref[...])
        return (a * _sigmoid(a) * _dot(u, wb_ref[...])).astype(BF16)

    @pl.when(jnp.logical_and(b == 0, j == 0))
    def _():
        h_ref[...] = hidden(tile(x_ref, ctx_ref, j), sh_ref[0], sc_ref[0])

    o_ref[0] = tile(x_ref, ctx_ref, j) + (0.5 * gt_ref[0]) * _dot(h_ref[...], wd_ref[...])
    _, jn = _next_tile(b, j, pl.num_programs(0), pl.num_programs(1))
    h_ref[...] = hidden(tile(xn_ref, ctxn_ref, jn), shn_ref[0], scn_ref[0])


def _ffn(xall, mod, g, wa, wb, wd, sub, tile_off, ctx=None):
    bsz, t, _ = xall.shape
    split = ctx is not None
    if split:
        t += ctx.shape[1]
    nj = t // TM
    ctx_row = mod.shape[0] - 1
    cur = lambda b, j: (b, j)
    nxt = lambda b, j: _next_tile(b, j, bsz, nj)

    def specs(at):
        if split:
            x_spec = pl.BlockSpec((1, TM, D_MODEL), lambda b, j: (at(b, j)[0], jnp.maximum(at(b, j)[1] - 1, 0), 0))
            ctx_spec = pl.BlockSpec((1, TM, D_MODEL), lambda b, j: (at(b, j)[0], 0, 0))
        else:
            x_spec = pl.BlockSpec((1, TM, D_MODEL), lambda b, j: (at(b, j)[0], at(b, j)[1], 0))
            ctx_spec = pl.BlockSpec((1, SUBLANES, D_MODEL), lambda b, j: (at(b, j)[0], 0, 0))
        return [x_spec, ctx_spec]

    def mod_spec(col, at):
        return pl.BlockSpec((1, 1, D_MODEL),
                            lambda b, j: (jnp.where(at(b, j)[1] + tile_off == 0, ctx_row, at(b, j)[0]), 0, col))

    if not split:
        ctx = xall
    return pl.pallas_call(
        functools.partial(_ffn_kernel, split_input=split),
        grid=(bsz, nj),
        in_specs=specs(cur) + specs(nxt) +
                 [mod_spec(3 * sub, cur), mod_spec(3 * sub + 1, cur), mod_spec(3 * sub + 2, cur),
                  mod_spec(3 * sub, nxt), mod_spec(3 * sub + 1, nxt),
                  _whole(g), _whole(wa), _whole(wb), _whole(wd)],
        out_specs=_row_spec(0, D_MODEL),
        out_shape=jax.ShapeDtypeStruct((bsz, t, D_MODEL), F32),
        scratch_shapes=[pltpu.VMEM((TM, wd.shape[0]), BF16)],
        compiler_params=_params(("arbitrary", "arbitrary")),
        name="ffn",
    )(xall, ctx, xall, ctx, mod, mod, mod, mod, mod, g, wa, wb, wd)


def _chunk_cumsum(tri, la):
    hi = la.astype(BF16)
    lo = (la - hi.astype(F32)).astype(BF16)
    return _dot(tri, hi) + _dot(tri, lo)


def _proj_gla_part(u, w_ref, wfg_ref, w2_ref, b2_ref, tril_ref, triu_ref,
                   q_ref, k_ref, v_ref, og_ref, gf_ref, gb_ref):
    nqk = GLA_HEADS * GLA_DK
    nv = GLA_HEADS * GLA_DV
    fg = _dot(u, wfg_ref[...]).astype(BF16)
    q_ref[0] = (_dot(u, w_ref[:, :nqk]) * GLA_DK ** -0.5).astype(BF16)
    z = _dot(fg, w2_ref[...]) + b2_ref[...]
    k_ref[0] = _dot(u, w_ref[:, nqk:2 * nqk]).astype(BF16)
    la = _log_sigmoid(z) * (1.0 / GLA_TAU)
    v_ref[0] = _dot(u, w_ref[:, 2 * nqk:2 * nqk + nv]).astype(BF16)
    gf_ref[0] = _chunk_cumsum(tril_ref[...], la[:, :nqk])
    gb_ref[0] = _chunk_cumsum(triu_ref[...], la[:, nqk:])
    og = _dot(u, w_ref[:, 2 * nqk + nv:])
    og_ref[0] = (og * _sigmoid(og)).astype(BF16)


def _proj_nat_part(u, wqk_ref, wvt_ref, gq_ref, gk_ref, ones_ref, q_ref, k_ref, vt_ref):
    p = _dot(u, wqk_ref[...])
    n = NAT_HEADS * NAT_HD
    ones_bd = ones_ref[...]
    q_ref[0] = (_head_norm(p[:, :n], gq_ref[...], ones_bd, NAT_HD) * (NAT_HD ** -0.5 * LOG2E)).astype(BF16)
    k_ref[0] = _head_norm(p[:, n:], gk_ref[...], ones_bd, NAT_HD).astype(BF16)
    vt_ref[0] = _dot_nt(wvt_ref[...], u).astype(BF16)


def _rope(t, cos, sin, lane):
    partner = jnp.where(lane % (GQA_HD // 2) < GQA_HD // 4,
                        pltpu.roll(t, GQA_HD - GQA_HD // 4, axis=1), pltpu.roll(t, GQA_HD // 4, axis=1))
    return t * cos + partner * sin


def _proj_gqa_part(u, wqk_ref, wvt_ref, wg_ref, gq_ref, gk_ref, ones_ref, cos_ref, sin_ref,
                   q_ref, k_ref, vt_ref, sg_ref):
    p = _dot(u, wqk_ref[...])
    nq = GQA_HEADS * GQA_HD
    ones_bd = ones_ref[...]
    qn = _head_norm(p[:, :nq], gq_ref[...], ones_bd, GQA_HD)
    kn = _head_norm(p[:, nq:], gk_ref[...], ones_bd, GQA_HD)
    cos = cos_ref[...]
    sin = sin_ref[...]
    lane = lax.broadcasted_iota(jnp.int32, (TM, GQA_HD), 1)
    for h in range(GQA_HEADS):
        sl = slice(h * GQA_HD, (h + 1) * GQA_HD)
        q_ref[0, :, sl] = (_rope(qn[:, sl], cos, sin, lane) * (GQA_HD ** -0.5 * LOG2E)).astype(BF16)
    for h in range(GQA_KV_HEADS):
        sl = slice(h * GQA_HD, (h + 1) * GQA_HD)
        k_ref[0, :, sl] = _rope(kn[:, sl], cos, sin, lane).astype(BF16)
    vt_ref[0] = _dot_nt(wvt_ref[...], u).astype(BF16)
    sg_ref[0] = _sigmoid(_dot(u, wg_ref[...])).astype(BF16)


N_PROJ_GLA_IN, N_PROJ_NAT_IN, N_PROJ_GQA_IN = 6, 5, 8
N_PROJ_GLA_OUT, N_PROJ_NAT_OUT, N_PROJ_GQA_OUT = 6, 3, 4


def _proj_kernel(*refs):
    it = iter(refs)
    take = lambda count: [next(it) for _ in range(count)]
    x_ref, sh_ref, sc_ref, g_ref = take(4)
    gla_in, nat_in, gqa_in = take(N_PROJ_GLA_IN), take(N_PROJ_NAT_IN), take(N_PROJ_GQA_IN)
    gla_out, nat_out, gqa_out = take(N_PROJ_GLA_OUT), take(N_PROJ_NAT_OUT), take(N_PROJ_GQA_OUT)
    u_ref, = take(1)

    @pl.when(pl.program_id(0) == 0)
    def _():
        u_ref[...] = jnp.zeros_like(u_ref)

    u_prev = u_ref[...]
    _proj_nat_part(u_prev, *nat_in, *nat_out)
    u = _norm_mod(x_ref[0], g_ref[...], sh_ref[0], sc_ref[0]).astype(BF16)
    _proj_gqa_part(u_prev, *gqa_in, *gqa_out)
    _proj_gla_part(u, *gla_in, *gla_out)
    u_ref[...] = u


def _projections(xall, mod, g, lw, rope_cos, rope_sin):
    bsz, t, _ = xall.shape
    ctx_row = mod.shape[0] - 1
    nt = t // TM
    n_tiles = bsz * nt
    lead = lambda s: jnp.minimum(s, n_tiles - 1)
    lag = lambda s: jnp.maximum(s - 1, 0)

    def rows(at, width):
        return pl.BlockSpec((1, TM, width), lambda s: (at(s) // nt, at(s) % nt, 0))

    def cols(at, height):
        return pl.BlockSpec((1, height, TM), lambda s: (at(s) // nt, 0, at(s) % nt))

    def mod_spec(col):
        return pl.BlockSpec((1, 1, D_MODEL),
                            lambda s: (jnp.where(lead(s) % nt == 0, ctx_row, lead(s) // nt), 0, col))

    sds = jax.ShapeDtypeStruct
    nqk = GLA_HEADS * GLA_DK
    nv = GLA_HEADS * GLA_DV
    n = NAT_HEADS * NAT_HD
    nq = GQA_HEADS * GQA_HD
    nkv = GQA_KV_HEADS * GQA_HD
    tab = pl.BlockSpec((TM, GQA_HD), lambda s: (lag(s) % nt, 0))
    gla_in = [lw["w_gla"], lw["w_fg"], lw["w2"], lw["b2"], lw["tril"], lw["triu"]]
    nat_in = [lw["w_nat_qk"], lw["w_nat_vt"], lw["nat_gq"], lw["nat_gk"], lw["ones64"]]
    gqa_in = [lw["w_gqa_qk"], lw["w_gqa_vt"], lw["w_gates"], lw["gqa_gq"], lw["gqa_gk"], lw["ones128"]]
    assert (len(gla_in), len(nat_in), len(gqa_in) + 2) == (N_PROJ_GLA_IN, N_PROJ_NAT_IN, N_PROJ_GQA_IN)
    outs = pl.pallas_call(
        _proj_kernel, grid=(n_tiles + 1,),
        in_specs=[rows(lead, D_MODEL), mod_spec(3), mod_spec(4), _whole(g)]
                 + [_whole(a) for a in gla_in + nat_in + gqa_in] + [tab, tab],
        out_specs=[rows(lead, nqk), rows(lead, nqk), rows(lead, nv), rows(lead, nv), rows(lead, nqk), rows(lead, nqk),
                   rows(lag, n), rows(lag, n), cols(lag, n),
                   rows(lag, nq), rows(lag, nkv), cols(lag, nkv), rows(lag, 3 * D_MODEL)],
        out_shape=[sds((bsz, t, nqk), BF16), sds((bsz, t, nqk), BF16), sds((bsz, t, nv), BF16),
                   sds((bsz, t, nv), BF16), sds((bsz, t, nqk), F32), sds((bsz, t, nqk), F32),
                   sds((bsz, t, n), BF16), sds((bsz, t, n), BF16), sds((bsz, n, t), BF16),
                   sds((bsz, t, nq), BF16), sds((bsz, t, nkv), BF16), sds((bsz, nkv, t), BF16),
                   sds((bsz, t, 3 * D_MODEL), BF16)],
        scratch_shapes=[pltpu.VMEM((TM, D_MODEL), BF16)],
        compiler_params=_params(("arbitrary",)), name="proj",
    )(xall, mod, mod, g, *gla_in, *nat_in, *gqa_in, rope_cos, rope_sin)
    return outs[:6], outs[6:9], outs[9:]


GLA_UNROLL = 16


def _gla_chain(n, forward, q_ref, k_ref, v_ref, g_ref, o_ref, state, tri, emit):
    c = GLA_CHUNK
    rows = pl.ds(pl.multiple_of(jnp.int32(n) * c, c), c)
    g = g_ref[0, rows, :]
    if forward:
        g_mid = g[c // 2 - 1:c // 2, :]
        g_edge = g[c - 1:c, :]
    else:
        g_mid = g[c // 2:c // 2 + 1, :]
        g_edge = g[0:1, :]
    q = q_ref[0, rows, :].astype(F32)
    k = k_ref[0, rows, :].astype(F32)
    v = v_ref[0, rows, :]
    q_mid = q * jnp.exp(g - g_mid)
    k_mid = k * jnp.exp(g_mid - g)
    q_in = q_mid.astype(BF16)
    k_in = k_mid.astype(BF16)
    q_x = (q_mid * jnp.exp(g_mid)).astype(BF16)
    k_end = (k_mid * jnp.exp(g_edge - g_mid)).astype(BF16)
    decay = jnp.exp(g_edge)
    yield
    att = _dot_nt(q_in, k_in)
    s_in = state["s"]
    state["s"] = s_in * decay + _dot_tn(v, k_end)
    yield
    att = jnp.where(tri, att, 0.0).astype(BF16)
    o = _dot(att, v) + _dot_nt(q_x, s_in.astype(BF16))
    yield
    if emit is None:
        o_ref[rows, :] = o
    else:
        emit(rows, o)


def _gla_kernel(q_ref, k_ref, v_ref, og_ref, gf_ref, gb_ref, gain_ref, y_ref,
                of_ref, ob_ref, sf_ref, sb_ref, *, n_chunks, n_ctx_chunks):
    c = GLA_CHUNK
    ri = lax.broadcasted_iota(jnp.int32, (c, c), 0)
    ci = lax.broadcasted_iota(jnp.int32, (c, c), 1)
    lower = ri >= ci
    upper = ci >= ri
    sf_ref[...] = jnp.zeros_like(sf_ref)
    sb_ref[...] = jnp.zeros_like(sb_ref)

    def finish(rows, o):
        y = o * lax.rsqrt(jnp.mean(o * o, axis=-1, keepdims=True) + RMS_EPS) * gain_ref[...]
        y_ref[0, rows, :] = (y * og_ref[0, rows, :].astype(F32)).astype(BF16)

    def scan(it, first, unroll, completing):
        fwd = {"s": sf_ref[...]}
        bwd = {"s": sb_ref[...]}
        emit_f = (lambda rows, o: finish(rows, o + ob_ref[rows, :])) if completing else None
        emit_b = (lambda rows, o: finish(rows, o + of_ref[rows, :])) if completing else None
        chains = []
        for u in range(unroll):
            s = first + it * unroll + u
            nb = jnp.where(s < n_ctx_chunks, n_ctx_chunks - 1 - s, n_chunks - 1 - (s - n_ctx_chunks))
            chains.append(_gla_chain(s, True, q_ref, k_ref, v_ref, gf_ref, of_ref, fwd, lower, emit_f))
            chains.append(_gla_chain(nb, False, q_ref, k_ref, v_ref, gb_ref, ob_ref, bwd, upper, emit_b))
        while chains:
            alive = []
            for ch in chains:
                try:
                    next(ch)
                    alive.append(ch)
                except StopIteration:
                    pass
            chains = alive
        sf_ref[...] = fwd["s"]
        sb_ref[...] = bwd["s"]

    half = (n_chunks - n_ctx_chunks) // 2
    scan(0, 0, n_ctx_chunks, False)
    for n in range(n_ctx_chunks):
        rows = pl.ds(n * c, c)
        finish(rows, of_ref[rows, :] + ob_ref[rows, :])
    lax.fori_loop(0, half // GLA_UNROLL,
                  lambda it, carry: scan(it, n_ctx_chunks, GLA_UNROLL, False) or carry, 0)
    lax.fori_loop(0, half // GLA_UNROLL,
                  lambda it, carry: scan(it, n_ctx_chunks + half, GLA_UNROLL, True) or carry, 0)


def _gla(q, k, v, og, gf, gb, lw, ctx_len):
    bsz, t, _ = q.shape
    assert (t - ctx_len) % (2 * GLA_UNROLL * GLA_CHUNK) == 0 and ctx_len % GLA_CHUNK == 0
    seq_spec = lambda w: pl.BlockSpec((1, t, w), lambda b, h: (b, 0, h))
    kern = functools.partial(_gla_kernel, n_chunks=t // GLA_CHUNK, n_ctx_chunks=ctx_len // GLA_CHUNK)
    return pl.pallas_call(
        kern, grid=(bsz, GLA_HEADS),
        in_specs=[seq_spec(GLA_DK), seq_spec(GLA_DK), seq_spec(GLA_DV), seq_spec(GLA_DV),
                  seq_spec(GLA_DK), seq_spec(GLA_DK), pl.BlockSpec((1, GLA_DV), lambda b, h: (0, 0))],
        out_specs=seq_spec(GLA_DV),
        out_shape=jax.ShapeDtypeStruct((bsz, t, GLA_HEADS * GLA_DV), BF16),
        scratch_shapes=[pltpu.VMEM((t, GLA_DV), F32), pltpu.VMEM((t, GLA_DV), F32),
                        pltpu.VMEM((GLA_DV, GLA_DK), F32), pltpu.VMEM((GLA_DV, GLA_DK), F32)],
        compiler_params=_params(("parallel", "parallel")), name="gla",
    )(q, k, v, og, gf, gb, lw["gla_gain"])


NAT_GROUP = 8
NAT_SAMPLES = 2


def _nat_kernel(q_ref, k0_ref, k1_ref, k2_ref, kc_ref, v0_ref, v1_ref, v2_ref, vc_ref, bias_ref, o_ref, s_ref,
                *, off):
    is_ctx = pl.program_id(0) + off == 0
    lane = lax.broadcasted_iota(jnp.int32, (TM, LANES), 1)
    row = lax.broadcasted_iota(jnp.int32, (LANES, TM), 0)
    n_heads = 2 * NAT_GROUP
    n_lat = NAT_KBLOCKS * TM

    def run(lat_keys, lat_vals):
        kcats, vlats = {}, {}

        def scores(c):
            smp, i = divmod(c, n_heads)
            hp, e = divmod(i, 2)
            sl = slice(hp * LANES, (hp + 1) * LANES)
            if (smp, hp) not in kcats:
                refs = lat_keys + [kc_ref]
                half = (len(refs) + 1) // 2
                kcats[smp, hp] = [jnp.concatenate([r[smp, :, sl] for r in part], axis=0)
                                  for part in (refs[:half], refs[half:]) if part]
            q = q_ref[smp, :, sl]
            qm = jnp.where((lane >= NAT_HD) if e else (lane < NAT_HD), q, jnp.zeros_like(q))
            at = 0
            for kpart in kcats[smp, hp]:
                s_ref[c % 2, at:at + kpart.shape[0], :] = _dot_nt(kpart, qm)
                at += kpart.shape[0]

        scores(0)
        outs = []
        for c in range(NAT_SAMPLES * n_heads):
            if c + 1 < NAT_SAMPLES * n_heads:
                scores(c + 1)
            smp, i = divmod(c, n_heads)
            hp, e = divmod(i, 2)
            sl = slice(hp * LANES, (hp + 1) * LANES)
            n_l = n_lat if lat_keys else 0
            s_ctx = s_ref[c % 2, n_l:n_l + TM, :]
            m = jnp.max(s_ctx, axis=0, keepdims=True)
            if lat_keys:
                s_lat = s_ref[c % 2, :n_lat, :] + bias_ref[i, 0]
                m = jnp.maximum(m, jnp.max(s_lat, axis=0, keepdims=True))
            def own_rows(v):
                vrow = lax.broadcasted_iota(jnp.int32, v.shape, 0)
                return jnp.where((vrow >= NAT_HD) if e else (vrow < NAT_HD), v, jnp.ones_like(v))

            p_ctx = jnp.exp2(s_ctx - m)
            o = _dot(own_rows(vc_ref[smp, sl, :]), p_ctx.astype(BF16))
            if lat_keys:
                if (smp, hp) not in vlats:
                    vlats[smp, hp] = jnp.concatenate([r[smp, sl, :] for r in lat_vals], axis=1)
                p_lat = jnp.exp2(s_lat - m)
                o = o + _dot(own_rows(vlats[smp, hp]), p_lat.astype(BF16))
            l = o[0:1, :] if e else o[LANES - 1:LANES, :]
            outs.append(o * (1.0 / l))
            if e == 1:
                ot = jnp.where(row < NAT_HD, outs[-2], outs[-1])
                o_ref[smp, :, sl] = ot.T.astype(BF16)

    @pl.when(is_ctx)
    def _():
        run([], [])

    @pl.when(jnp.logical_not(is_ctx))
    def _():
        run([k0_ref, k1_ref, k2_ref], [v0_ref, v1_ref, v2_ref])


def _nat(q, k, vt, bias, layer, need_ctx):
    bsz, t, n = q.shape
    off = 0 if need_ctx else 1
    nblk = t // TM - 1
    gw = NAT_GROUP * LANES
    ngroups = n // gw

    def kblock(qi, i):
        j = qi + off - 1
        return 1 + jnp.clip(j - 1, 0, nblk - NAT_KBLOCKS) + i

    def bias_class(qi):
        j = qi + off - 1
        return jnp.where(j <= 0, 0, jnp.where(j == nblk - 1, 2, 1))

    ns = NAT_SAMPLES
    assert bsz % ns == 0
    kspec = lambda i: pl.BlockSpec((ns, TM, gw), lambda qi, hg, b: (b, kblock(qi, i), hg))
    vspec = lambda i: pl.BlockSpec((ns, gw, TM), lambda qi, hg, b: (b, hg, kblock(qi, i)))
    return pl.pallas_call(
        functools.partial(_nat_kernel, off=off),
        grid=(t // TM - off, ngroups, bsz // ns),
        in_specs=[pl.BlockSpec((ns, TM, gw), lambda qi, hg, b: (b, qi + off, hg)),
                  kspec(0), kspec(1), kspec(2),
                  pl.BlockSpec((ns, TM, gw), lambda qi, hg, b: (b, 0, hg)),
                  vspec(0), vspec(1), vspec(2),
                  pl.BlockSpec((ns, gw, TM), lambda qi, hg, b: (b, hg, 0)),
                  pl.BlockSpec((2 * NAT_GROUP, 1, NAT_KBLOCKS * TM, TM),
                               lambda qi, hg, b: (layer * ngroups + hg, bias_class(qi), 0, 0),
                               pipeline_mode=pl.Buffered(1))],
        out_specs=pl.BlockSpec((ns, TM, gw), lambda qi, hg, b: (b, qi, hg)),
        out_shape=jax.ShapeDtypeStruct((bsz, t - off * TM, n), BF16),
        scratch_shapes=[pltpu.VMEM((2, (NAT_KBLOCKS + 1) * TM, TM), F32)],
        compiler_params=_params(("parallel", "parallel", "parallel")), name="nat",
    )(q, k, k, k, k, vt, vt, vt, vt, bias)


def _nat_bias_tables(rpb, rows):
    nblk = rows // NAT_QROWS
    kr_win = min(WIN_R, rows)
    col = np.arange(GRID_W)
    c_start = np.clip(col - WIN_C // 2, 0, GRID_W - WIN_C)
    cmask = (col[:, None] >= c_start[None, :]) & (col[:, None] < c_start[None, :] + WIN_C)
    dc = np.clip(col[:, None] - col[None, :] + WIN_C - 1, 0, 2 * WIN_C - 2)
    drs, rmasks = [], []
    for jblk in (0, 1, nblk - 1):
        base = int(np.clip(jblk - 1, 0, nblk - NAT_KBLOCKS))
        r = NAT_QROWS * jblk + np.arange(NAT_QROWS)
        r_start = np.clip(r - kr_win // 2, 0, rows - kr_win)
        kr = NAT_QROWS * base + np.arange(NAT_KBLOCKS * NAT_QROWS)
        rmasks.append((kr[:, None] >= r_start[None, :]) & (kr[:, None] < r_start[None, :] + kr_win))
        drs.append(np.clip(kr[:, None] - r[None, :] + WIN_R - 1, 0, 2 * WIN_R - 2))
    drx = np.where(np.stack(rmasks), np.stack(drs), 2 * WIN_R - 1)
    dcx = np.where(cmask, dc, 2 * WIN_C - 1)
    nh = rpb.shape[0]
    ext = jnp.full((nh, 2 * WIN_R, 2 * WIN_C), NEG_INF, F32)
    ext = ext.at[:, :2 * WIN_R - 1, :2 * WIN_C - 1].set(rpb.astype(F32) * LOG2E)
    pick = np.tile(dcx[:, None, :], (1, NAT_QROWS, 1)).reshape(-1)
    onehot = jnp.asarray(np.arange(2 * WIN_C)[:, None] == pick[None, :], F32)
    a = jnp.einsum("hdj,jn->hdn", ext, onehot, precision=lax.Precision.HIGHEST)
    a = a.reshape(nh, 2 * WIN_R, GRID_W, NAT_QROWS * GRID_W)
    nk = NAT_KBLOCKS * NAT_QROWS * GRID_W
    return pl.pallas_call(
        functools.partial(_bias_kernel, drx=drx),
        grid=(nh,),
        in_specs=[pl.BlockSpec((1,) + a.shape[1:], lambda h: (h, 0, 0, 0))],
        out_specs=pl.BlockSpec((1, 3, nk, TM), lambda h: (h, 0, 0, 0)),
        out_shape=jax.ShapeDtypeStruct((nh, 3, nk, TM), F32),
        compiler_params=_params(("parallel",)), name="nat_bias",
    )(a)


def _bias_kernel(a_ref, o_ref, *, drx):
    lane_blk = lax.broadcasted_iota(jnp.int32, (GRID_W, TM), 1) // GRID_W
    n_cls, n_kri, n_ri = drx.shape
    for cls in range(n_cls):
        for kri in range(n_kri):
            strip = a_ref[0, int(drx[cls, kri, n_ri - 1])]
            for ri in range(n_ri - 1):
                strip = jnp.where(lane_blk == ri, a_ref[0, int(drx[cls, kri, ri])], strip)
            o_ref[0, cls, kri * GRID_W:(kri + 1) * GRID_W, :] = strip


def _gqa_kernel(q_ref, k_ref, vt_ref, o_ref, s_ref, acc_ref, *, off, n_lat):
    is_ctx = pl.program_id(1) + off == 0
    group = GQA_HEADS // GQA_KV_HEADS

    def run(n_blocks):
        qs = {}

        n_chains = GQA_SAMPLES * GQA_KV_HEADS

        def scores(c, start, size, slot):
            smp, h = divmod(c, GQA_KV_HEADS)
            if c not in qs:
                heads = range(h * group, (h + 1) * group)
                qs[c] = jnp.concatenate([q_ref[smp, :, g * GQA_HD:(g + 1) * GQA_HD] for g in heads], axis=0)
            s_ref[c % 2, slot, :size, :] = _dot_nt(k_ref[smp, start:start + size, h * GQA_HD:(h + 1) * GQA_HD], qs[c])

        def start_chain(c):
            scores(c, 0, TM, 0)
            if n_blocks:
                scores(c, TM, GQA_KB, 1)

        start_chain(0)
        for c in range(n_chains):
            smp, h = divmod(c, GQA_KV_HEADS)
            rows = slice(h * GQA_HD, (h + 1) * GQA_HD)
            if n_blocks == 0 and c + 1 < n_chains:
                start_chain(c + 1)
            st = s_ref[c % 2, 0, :TM, :]
            m = jnp.max(st, axis=0, keepdims=True)
            p = jnp.exp2(st - m)
            l = jnp.sum(p, axis=0, keepdims=True)
            acc_ref[c % 2] = _dot(vt_ref[smp, rows, :TM], p.astype(BF16))
            for j in range(n_blocks):
                start = TM + j * GQA_KB
                if j + 1 < n_blocks:
                    scores(c, start + GQA_KB, GQA_KB, j % 2)
                elif c + 1 < n_chains:
                    start_chain(c + 1)
                st = s_ref[c % 2, (j + 1) % 2]
                m_new = jnp.maximum(m, jnp.max(st, axis=0, keepdims=True))
                alpha = jnp.exp2(m - m_new)
                p = jnp.exp2(st - m_new)
                l = alpha * l + jnp.sum(p, axis=0, keepdims=True)
                acc_ref[c % 2] = alpha * acc_ref[c % 2] + _dot(vt_ref[smp, rows, start:start + GQA_KB],
                                                               p.astype(BF16))
                m = m_new
            ot = acc_ref[c % 2] * (1.0 / l)
            for g in range(group):
                cols = slice((h * group + g) * GQA_HD, (h * group + g + 1) * GQA_HD)
                o_ref[smp, :, cols] = ot[:, g * TM:(g + 1) * TM].T.astype(BF16)

    @pl.when(is_ctx)
    def _():
        run(0)

    @pl.when(jnp.logical_not(is_ctx))
    def _():
        run(n_lat)


def _gqa(q, k, vt, need_ctx):
    bsz, t, nq = q.shape
    nkv = k.shape[2]
    off = 0 if need_ctx else 1
    group = GQA_HEADS // GQA_KV_HEADS
    ns = GQA_SAMPLES
    assert (t - TM) % GQA_KB == 0 and bsz % ns == 0
    return pl.pallas_call(
        functools.partial(_gqa_kernel, off=off, n_lat=(t - TM) // GQA_KB),
        grid=(bsz // ns, t // TM - off),
        in_specs=[pl.BlockSpec((ns, TM, nq), lambda b, qi: (b, qi + off, 0)),
                  pl.BlockSpec((ns, t, nkv), lambda b, qi: (b, 0, 0)),
                  pl.BlockSpec((ns, nkv, t), lambda b, qi: (b, 0, 0))],
        out_specs=pl.BlockSpec((ns, TM, nq), lambda b, qi: (b, qi, 0)),
        out_shape=jax.ShapeDtypeStruct((bsz, t - off * TM, nq), BF16),
        scratch_shapes=[pltpu.VMEM((2, 2, GQA_KB, group * TM), F32),
                        pltpu.VMEM((2, GQA_HD, group * TM), F32)],
        compiler_params=_params(("parallel", "arbitrary")), name="gqa",
    )(q, k, vt)


def _merge_kernel(x_ref, gt_ref, y0_ref, y1_ref, y2_ref, sg_ref, wb_ref, wo_ref, o_ref):
    z = None
    for i, y_ref in enumerate((y0_ref, y1_ref, y2_ref)):
        zi = sg_ref[0, :, i * D_MODEL:(i + 1) * D_MODEL].astype(F32) * _dot(y_ref[0], wb_ref[i])
        z = zi if z is None else z + zi
    o_ref[0] = x_ref[0] + gt_ref[0] * _dot(z.astype(BF16), wo_ref[...])


def _merge(xall, mod, ys, sg, wb, wo, off):
    bsz, t, _ = xall.shape
    ctx_row = mod.shape[0] - 1
    return pl.pallas_call(
        _merge_kernel,
        grid=(bsz, t // TM - off),
        in_specs=[_row_spec(off, D_MODEL), _mod_spec(5, off, ctx_row),
                  _row_spec(off, D_MODEL), _row_spec(0, D_MODEL), _row_spec(0, D_MODEL),
                  _row_spec(off, 3 * D_MODEL), _whole(wb), _whole(wo)],
        out_specs=_row_spec(0, D_MODEL),
        out_shape=jax.ShapeDtypeStruct((bsz, t - off * TM, D_MODEL), F32),
        compiler_params=_params(("parallel", "parallel")), name="merge",
    )(xall, mod, ys[0], ys[1], ys[2], sg, wb, wo)


def _block_diag_ones(head_dim):
    i = np.arange(MXU)
    return jnp.asarray((i[:, None] // head_dim) == (i[None, :] // head_dim), dtype=BF16)


def _chunk_tri(upper):
    i = np.arange(TM)
    same = (i[:, None] // GLA_CHUNK) == (i[None, :] // GLA_CHUNK)
    tri = (i[None, :] >= i[:, None]) if upper else (i[:, None] >= i[None, :])
    return jnp.asarray(same & tri, dtype=BF16)


def _rope_tables(ctx_len, seq):
    quarter = GQA_HD // 4
    freqs = ROPE_BASE ** (-np.arange(quarter, dtype=np.float64) / quarter)
    tok = np.arange(seq)
    ang_r = (tok // GRID_W)[:, None] * freqs
    ang_c = (tok % GRID_W)[:, None] * freqs
    ang = np.concatenate([ang_r, ang_r, ang_c, ang_c], axis=1)
    sign = np.tile(np.concatenate([-np.ones(quarter), np.ones(quarter)]), 2)
    cos = np.concatenate([np.ones((ctx_len, GQA_HD)), np.cos(ang)], axis=0)
    sin = np.concatenate([np.zeros((ctx_len, GQA_HD)), np.sin(ang) * sign], axis=0)
    return jnp.asarray(cos, F32), jnp.asarray(sin, F32)


def _layer_weights(l, ffn_w_in, ffn_w_out, w_in, gla_fg_w2, gla_fg_b, gla_norm_g, nat_q_norm, nat_k_norm,
                   gqa_q_norm, gqa_k_norm, w_branch, w_out):
    w = w_in[l]
    nqk = GLA_HEADS * GLA_DK
    nv = GLA_HEADS * GLA_DV
    o_fg = 2 * nqk + 2 * nv
    o_nat = o_fg + 2 * GLA_LR
    n = NAT_HEADS * NAT_HD
    o_gqa = o_nat + 3 * n
    nq = GQA_HEADS * GQA_HD
    nkv = GQA_KV_HEADS * GQA_HD
    o_gate = o_gqa + nq + 2 * nkv
    pad = jnp.zeros((D_MODEL, LANES - 2 * GLA_LR), w.dtype)
    w2 = gla_fg_w2[l]
    zeros_lr = jnp.zeros((GLA_LR, nqk), w2.dtype)
    zeros_rest = jnp.zeros((LANES - 2 * GLA_LR, nqk), w2.dtype)
    return {
        "ffn": [(ffn_w_in[l, i, :, :D_FF].astype(BF16), ffn_w_in[l, i, :, D_FF:].astype(BF16),
                 ffn_w_out[l, i].astype(BF16)) for i in range(2)],
        "w_gla": w[:, :o_fg].astype(BF16),
        "w_fg": jnp.concatenate([w[:, o_fg:o_nat], pad], axis=1).astype(BF16),
        "w_nat_qk": w[:, o_nat:o_nat + 2 * n].astype(BF16),
        "w_nat_vt": w[:, o_nat + 2 * n:o_gqa].T.astype(BF16),
        "w_gqa_qk": w[:, o_gqa:o_gqa + nq + nkv].astype(BF16),
        "w_gqa_vt": w[:, o_gqa + nq + nkv:o_gate].T.astype(BF16),
        "w_gates": w[:, o_gate:].astype(BF16),
        "w2": jnp.concatenate([jnp.concatenate([w2[0], zeros_lr, zeros_rest], axis=0),
                               jnp.concatenate([zeros_lr, w2[1], zeros_rest], axis=0)], axis=1).astype(BF16),
        "b2": gla_fg_b[l].reshape(1, 2 * nqk),
        "tril": _chunk_tri(False),
        "triu": _chunk_tri(True),
        "gla_gain": gla_norm_g[l].reshape(1, GLA_DV),
        "nat_gq": jnp.tile(nat_q_norm[l], NAT_HEADS).reshape(1, n),
        "nat_gk": jnp.tile(nat_k_norm[l], NAT_HEADS).reshape(1, n),
        "gqa_gq": jnp.tile(gqa_q_norm[l], GQA_HEADS).reshape(1, nq),
        "gqa_gk": jnp.tile(gqa_k_norm[l], GQA_KV_HEADS).reshape(1, nkv),
        "ones64": _block_diag_ones(NAT_HD),
        "ones128": _block_diag_ones(GQA_HD),
        "w_branch": w_branch[l].astype(BF16),
        "w_out": w_out[l].astype(BF16),
    }


def kernel(x, c, ctx, c_ctx, w_mod, b_mod, norm_g, ffn_w_in, ffn_w_out, w_in, gla_fg_w2, gla_fg_b, gla_norm_g,
           nat_q_norm, nat_k_norm, nat_rpb, gqa_q_norm, gqa_k_norm, w_branch, w_out):
    bsz, seq, _ = x.shape
    ctx_len = ctx.shape[1]
    depth = w_mod.shape[0]
    assert seq % TM == 0 and ctx_len == TM and seq // TM >= NAT_KBLOCKS

    mod_rows = -(-(bsz + 1) // SUBLANES) * SUBLANES
    cvec = jnp.concatenate([c, c_ctx[None, :], jnp.zeros((mod_rows - bsz - 1, D_MODEL), c.dtype)], axis=0)
    mods = _mod_table(cvec, w_mod, b_mod)[:, :bsz + 1].reshape(depth, bsz + 1, 1, N_MOD * D_MODEL)
    rope_cos, rope_sin = _rope_tables(ctx_len, seq)
    nat_bias = _nat_bias_tables(nat_rpb.reshape((-1,) + nat_rpb.shape[2:]), seq // GRID_W)

    xall = x
    for l in range(depth):
        need_ctx = l < depth - 1
        off = 0 if need_ctx else 1
        lw = _layer_weights(l, ffn_w_in, ffn_w_out, w_in, gla_fg_w2, gla_fg_b, gla_norm_g, nat_q_norm,
                            nat_k_norm, gqa_q_norm, gqa_k_norm, w_branch, w_out)
        mod = mods[l]
        g = [norm_g[l, i].reshape(1, D_MODEL) for i in range(3)]
        xall = _ffn(xall, mod, g[0], *lw["ffn"][0], sub=0, tile_off=0, ctx=ctx if l == 0 else None)
        (gq, gk, gv, gog, ggf, ggb), (nq_, nk_, nvt), (aq, ak, avt, sg) = _projections(
            xall, mod, g[1], lw, rope_cos, rope_sin)
        y_gla = _gla(gq, gk, gv, gog, ggf, ggb, lw, ctx_len)
        y_nat = _nat(nq_, nk_, nvt, nat_bias, l, need_ctx)
        y_gqa = _gqa(aq, ak, avt, need_ctx)
        xall = _merge(xall, mod, (y_gla, y_nat, y_gqa), sg, lw["w_branch"], lw["w_out"], off)
        xall = _ffn(xall, mod, g[2], *lw["ffn"][1], sub=2, tile_off=off)
    return xall
```

```python
import functools

import numpy as np
import jax
import jax.numpy as jnp
from jax import lax
from jax.experimental import pallas as pl
from jax.experimental.pallas import tpu as pltpu

F32 = jnp.float32
BF16 = jnp.bfloat16

D_MODEL = 1024
GRID_W = 64
N_MOD = 9
D_FF = 2816
RMS_EPS = 1e-6
NEG_INF = -1e30

GLA_HEADS = 4
GLA_DK = 128
GLA_DV = 256
GLA_LR = 16
GLA_TAU = 16.0
GLA_CHUNK = 128

NAT_HEADS = 16
NAT_HD = 64
WIN_R = 8
WIN_C = 16
NAT_QROWS = 4
NAT_KBLOCKS = 3

GQA_HEADS = 8
GQA_KV_HEADS = 2
GQA_HD = 128
ROPE_BASE = 10000.0
GQA_KB = 512
GQA_SAMPLES = 2
LOG2E = 1.4426950408889634

TM = 256
LANES = 128
SUBLANES = 8
MXU = 256
VMEM_LIMIT = 56 * 1024 * 1024


def _dot(a, b):
    return jnp.dot(a, b, preferred_element_type=F32)


def _dot_nt(a, b):
    return lax.dot_general(a, b, (((1,), (1,)), ((), ())), preferred_element_type=F32)


def _dot_tn(a, b):
    return lax.dot_general(a, b, (((0,), (0,)), ((), ())), preferred_element_type=F32)


def _sigmoid(x):
    return 1.0 / (1.0 + jnp.exp(-x))


def _log_sigmoid(x):
    return -(jnp.maximum(-x, 0.0) + jnp.log(1.0 + jnp.exp(-jnp.abs(x))))


def _norm_mod(x, g, shift, scale):
    y = x * lax.rsqrt(jnp.mean(x * x, axis=-1, keepdims=True) + RMS_EPS) * g
    return y * (1.0 + scale) + shift


def _head_norm(t, gain, ones_bd, head_dim):
    sq = (t * t).astype(BF16)
    parts = [_dot(sq[:, i:i + MXU], ones_bd) for i in range(0, t.shape[1], MXU)]
    ss = parts[0] if len(parts) == 1 else jnp.concatenate(parts, axis=1)
    return t * lax.rsqrt(ss * (1.0 / head_dim) + RMS_EPS) * gain


def _params(semantics):
    return pltpu.CompilerParams(dimension_semantics=semantics, vmem_limit_bytes=VMEM_LIMIT)


def _whole(arr):
    zeros = (0,) * arr.ndim
    return pl.BlockSpec(arr.shape, lambda *_: zeros, pipeline_mode=pl.Buffered(1))


def _row_spec(off, width):
    return pl.BlockSpec((1, TM, width), lambda b, j: (b, j + off, 0))


def _mod_spec(col, off, ctx_row):
    return pl.BlockSpec((1, 1, D_MODEL), lambda b, j: (jnp.where(j + off == 0, ctx_row, b), 0, col))


def _mod_kernel(c_ref, w_ref, b_ref, o_ref):
    c = c_ref[...]
    s = (c * _sigmoid(c)).astype(BF16)
    o_ref[0] = _dot(s, w_ref[0].astype(BF16)) + b_ref[0]


def _mod_table(cvec, w_mod, b_mod):
    depth = w_mod.shape[0]
    rows = cvec.shape[0]
    width = 3 * D_MODEL
    return pl.pallas_call(
        _mod_kernel,
        grid=(depth, N_MOD * D_MODEL // width),
        in_specs=[pl.BlockSpec((rows, D_MODEL), lambda l, n: (0, 0)),
                  pl.BlockSpec((1, D_MODEL, width), lambda l, n: (l, 0, n)),
                  pl.BlockSpec((1, 1, width), lambda l, n: (l, 0, n))],
        out_specs=pl.BlockSpec((1, rows, width), lambda l, n: (l, 0, n)),
        out_shape=jax.ShapeDtypeStruct((depth, rows, N_MOD * D_MODEL), F32),
        compiler_params=_params(("arbitrary", "arbitrary")),
        name="mod_table",
    )(cvec, w_mod, b_mod.reshape(depth, 1, N_MOD * D_MODEL))


def _next_tile(b, j, nb, nj):
    wrap = j + 1 == nj
    last = jnp.logical_and(wrap, b + 1 == nb)
    return jnp.where(jnp.logical_and(wrap, jnp.logical_not(last)), b + 1, b), \
        jnp.where(wrap, jnp.where(last, j, 0), j + 1)


def _ffn_kernel(x_ref, ctx_ref, xn_ref, ctxn_ref, sh_ref, sc_ref, gt_ref, shn_ref, scn_ref, g_ref,
                wa_ref, wb_ref, wd_ref, o_ref, h_ref, *, split_input):
    b, j = pl.program_id(0), pl.program_id(1)

    def tile(x_r, ctx_r, jj):
        return jnp.where(jj == 0, ctx_r[0], x_r[0]) if split_input else x_r[0]

    def hidden(xt, shift, scale):
        u = _norm_mod(xt, g_ref[...], shift, scale).astype(BF16)
        a = _dot(u, wa_ref[...])
        return (a * _sigmoid(a) * _dot(u, wb_ref[...])).astype(BF16)

    @pl.when(jnp.logical_and(b == 0, j == 0))
    def _():
        h_ref[...] = hidden(tile(x_ref, ctx_ref, j), sh_ref[0], sc_ref[0])

    o_ref[0] = tile(x_ref, ctx_ref, j) + (0.5 * gt_ref[0]) * _dot(h_ref[...], wd_ref[...])
    _, jn = _next_tile(b, j, pl.num_programs(0), pl.num_programs(1))
    h_ref[...] = hidden(tile(xn_ref, ctxn_ref, jn), shn_ref[0], scn_ref[0])


def _ffn(xall, mod, g, wa, wb, wd, sub, tile_off, ctx=None):
    bsz, t, _ = xall.shape
    split = ctx is not None
    if split:
        t += ctx.shape[1]
    nj = t // TM
    ctx_row = mod.shape[0] - 1
    cur = lambda b, j: (b, j)
    nxt = lambda b, j: _next_tile(b, j, bsz, nj)

    def specs(at):
        if split:
            x_spec = pl.BlockSpec((1, TM, D_MODEL), lambda b, j: (at(b, j)[0], jnp.maximum(at(b, j)[1] - 1, 0), 0))
            ctx_spec = pl.BlockSpec((1, TM, D_MODEL), lambda b, j: (at(b, j)[0], 0, 0))
        else:
            x_spec = pl.BlockSpec((1, TM, D_MODEL), lambda b, j: (at(b, j)[0], at(b, j)[1], 0))
            ctx_spec = pl.BlockSpec((1, SUBLANES, D_MODEL), lambda b, j: (at(b, j)[0], 0, 0))
        return [x_spec, ctx_spec]

    def mod_spec(col, at):
        return pl.BlockSpec((1, 1, D_MODEL),
                            lambda b, j: (jnp.where(at(b, j)[1] + tile_off == 0, ctx_row, at(b, j)[0]), 0, col))

    if not split:
        ctx = xall
    return pl.pallas_call(
        functools.partial(_ffn_kernel, split_input=split),
        grid=(bsz, nj),
        in_specs=specs(cur) + specs(nxt) +
                 [mod_spec(3 * sub, cur), mod_spec(3 * sub + 1, cur), mod_spec(3 * sub + 2, cur),
                  mod_spec(3 * sub, nxt), mod_spec(3 * sub + 1, nxt),
                  _whole(g), _whole(wa), _whole(wb), _whole(wd)],
        out_specs=_row_spec(0, D_MODEL),
        out_shape=jax.ShapeDtypeStruct((bsz, t, D_MODEL), F32),
        scratch_shapes=[pltpu.VMEM((TM, wd.shape[0]), BF16)],
        compiler_params=_params(("arbitrary", "arbitrary")),
        name="ffn",
    )(xall, ctx, xall, ctx, mod, mod, mod, mod, mod, g, wa, wb, wd)


def _chunk_cumsum(tri, la):
    hi = la.astype(BF16)
    lo = (la - hi.astype(F32)).astype(BF16)
    return _dot(tri, hi) + _dot(tri, lo)


def _proj_gla_part(u, w_ref, wfg_ref, w2_ref, b2_ref, tril_ref, triu_ref,
                   q_ref, k_ref, v_ref, og_ref, gf_ref, gb_ref):
    nqk = GLA_HEADS * GLA_DK
    nv = GLA_HEADS * GLA_DV
    fg = _dot(u, wfg_ref[...]).astype(BF16)
    q_ref[0] = (_dot(u, w_ref[:, :nqk]) * GLA_DK ** -0.5).astype(BF16)
    z = _dot(fg, w2_ref[...]) + b2_ref[...]
    k_ref[0] = _dot(u, w_ref[:, nqk:2 * nqk]).astype(BF16)
    la = _log_sigmoid(z) * (1.0 / GLA_TAU)
    v_ref[0] = _dot(u, w_ref[:, 2 * nqk:2 * nqk + nv]).astype(BF16)
    gf_ref[0] = _chunk_cumsum(tril_ref[...], la[:, :nqk])
    gb_ref[0] = _chunk_cumsum(triu_ref[...], la[:, nqk:])
    og = _dot(u, w_ref[:, 2 * nqk + nv:])
    og_ref[0] = (og * _sigmoid(og)).astype(BF16)


def _proj_nat_part(u, wqk_ref, wvt_ref, gq_ref, gk_ref, ones_ref, q_ref, k_ref, vt_ref):
    p = _dot(u, wqk_ref[...])
    n = NAT_HEADS * NAT_HD
    ones_bd = ones_ref[...]
    q_ref[0] = (_head_norm(p[:, :n], gq_ref[...], ones_bd, NAT_HD) * (NAT_HD ** -0.5 * LOG2E)).astype(BF16)
    k_ref[0] = _head_norm(p[:, n:], gk_ref[...], ones_bd, NAT_HD).astype(BF16)
    vt_ref[0] = _dot_nt(wvt_ref[...], u).astype(BF16)


def _rope(t, cos, sin, lane):
    partner = jnp.where(lane % (GQA_HD // 2) < GQA_HD // 4,
                        pltpu.roll(t, GQA_HD - GQA_HD // 4, axis=1), pltpu.roll(t, GQA_HD // 4, axis=1))
    return t * cos + partner * sin


def _proj_gqa_part(u, wqk_ref, wvt_ref, wg_ref, gq_ref, gk_ref, ones_ref, cos_ref, sin_ref,
                   q_ref, k_ref, vt_ref, sg_ref):
    p = _dot(u, wqk_ref[...])
    nq = GQA_HEADS * GQA_HD
    ones_bd = ones_ref[...]
    qn = _head_norm(p[:, :nq], gq_ref[...], ones_bd, GQA_HD)
    kn = _head_norm(p[:, nq:], gk_ref[...], ones_bd, GQA_HD)
    cos = cos_ref[...]
    sin = sin_ref[...]
    lane = lax.broadcasted_iota(jnp.int32, (TM, GQA_HD), 1)
    for h in range(GQA_HEADS):
        sl = slice(h * GQA_HD, (h + 1) * GQA_HD)
        q_ref[0, :, sl] = (_rope(qn[:, sl], cos, sin, lane) * (GQA_HD ** -0.5 * LOG2E)).astype(BF16)
    for h in range(GQA_KV_HEADS):
        sl = slice(h * GQA_HD, (h + 1) * GQA_HD)
        k_ref[0, :, sl] = _rope(kn[:, sl], cos, sin, lane).astype(BF16)
    vt_ref[0] = _dot_nt(wvt_ref[...], u).astype(BF16)
    sg_ref[0] = _sigmoid(_dot(u, wg_ref[...])).astype(BF16)


N_PROJ_GLA_IN, N_PROJ_NAT_IN, N_PROJ_GQA_IN = 6, 5, 8
N_PROJ_GLA_OUT, N_PROJ_NAT_OUT, N_PROJ_GQA_OUT = 6, 3, 4


def _proj_kernel(*refs):
    it = iter(refs)
    take = lambda count: [next(it) for _ in range(count)]
    x_ref, sh_ref, sc_ref, g_ref = take(4)
    gla_in, nat_in, gqa_in = take(N_PROJ_GLA_IN), take(N_PROJ_NAT_IN), take(N_PROJ_GQA_IN)
    gla_out, nat_out, gqa_out = take(N_PROJ_GLA_OUT), take(N_PROJ_NAT_OUT), take(N_PROJ_GQA_OUT)
    u_ref, = take(1)

    @pl.when(pl.program_id(0) == 0)
    def _():
        u_ref[...] = jnp.zeros_like(u_ref)

    u_prev = u_ref[...]
    _proj_gqa_part(u_prev, *gqa_in, *gqa_out)
    u = _norm_mod(x_ref[0], g_ref[...], sh_ref[0], sc_ref[0]).astype(BF16)
    _proj_gla_part(u, *gla_in, *gla_out)
    _proj_nat_part(u_prev, *nat_in, *nat_out)
    u_ref[...] = u


def _projections(xall, mod, g, lw, rope_cos, rope_sin):
    bsz, t, _ = xall.shape
    ctx_row = mod.shape[0] - 1
    nt = t // TM
    n_tiles = bsz * nt
    lead = lambda s: jnp.minimum(s, n_tiles - 1)
    lag = lambda s: jnp.maximum(s - 1, 0)

    def rows(at, width):
        return pl.BlockSpec((1, TM, width), lambda s: (at(s) // nt, at(s) % nt, 0))

    def cols(at, height):
        return pl.BlockSpec((1, height, TM), lambda s: (at(s) // nt, 0, at(s) % nt))

    def mod_spec(col):
        return pl.BlockSpec((1, 1, D_MODEL),
                            lambda s: (jnp.where(lead(s) % nt == 0, ctx_row, lead(s) // nt), 0, col))

    sds = jax.ShapeDtypeStruct
    nqk = GLA_HEADS * GLA_DK
    nv = GLA_HEADS * GLA_DV
    n = NAT_HEADS * NAT_HD
    nq = GQA_HEADS * GQA_HD
    nkv = GQA_KV_HEADS * GQA_HD
    tab = pl.BlockSpec((TM, GQA_HD), lambda s: (lag(s) % nt, 0))
    gla_in = [lw["w_gla"], lw["w_fg"], lw["w2"], lw["b2"], lw["tril"], lw["triu"]]
    nat_in = [lw["w_nat_qk"], lw["w_nat_vt"], lw["nat_gq"], lw["nat_gk"], lw["ones64"]]
    gqa_in = [lw["w_gqa_qk"], lw["w_gqa_vt"], lw["w_gates"], lw["gqa_gq"], lw["gqa_gk"], lw["ones128"]]
    assert (len(gla_in), len(nat_in), len(gqa_in) + 2) == (N_PROJ_GLA_IN, N_PROJ_NAT_IN, N_PROJ_GQA_IN)
    outs = pl.pallas_call(
        _proj_kernel, grid=(n_tiles + 1,),
        in_specs=[rows(lead, D_MODEL), mod_spec(3), mod_spec(4), _whole(g)]
                 + [_whole(a) for a in gla_in + nat_in + gqa_in] + [tab, tab],
        out_specs=[rows(lead, nqk), rows(lead, nqk), rows(lead, nv), rows(lead, nv), rows(lead, nqk), rows(lead, nqk),
                   rows(lag, n), rows(lag, n), cols(lag, n),
                   rows(lag, nq), rows(lag, nkv), cols(lag, nkv), rows(lag, 3 * D_MODEL)],
        out_shape=[sds((bsz, t, nqk), BF16), sds((bsz, t, nqk), BF16), sds((bsz, t, nv), BF16),
                   sds((bsz, t, nv), BF16), sds((bsz, t, nqk), F32), sds((bsz, t, nqk), F32),
                   sds((bsz, t, n), BF16), sds((bsz, t, n), BF16), sds((bsz, n, t), BF16),
                   sds((bsz, t, nq), BF16), sds((bsz, t, nkv), BF16), sds((bsz, nkv, t), BF16),
                   sds((bsz, t, 3 * D_MODEL), BF16)],
        scratch_shapes=[pltpu.VMEM((TM, D_MODEL), BF16)],
        compiler_params=_params(("arbitrary",)), name="proj",
    )(xall, mod, mod, g, *gla_in, *nat_in, *gqa_in, rope_cos, rope_sin)
    return outs[:6], outs[6:9], outs[9:]


GLA_UNROLL = 16


def _gla_chain(n, forward, q_ref, k_ref, v_ref, g_ref, o_ref, state, tri, emit):
    c = GLA_CHUNK
    rows = pl.ds(pl.multiple_of(jnp.int32(n) * c, c), c)
    g = g_ref[0, rows, :]
    if forward:
        g_mid = g[c // 2 - 1:c // 2, :]
        g_edge = g[c - 1:c, :]
    else:
        g_mid = g[c // 2:c // 2 + 1, :]
        g_edge = g[0:1, :]
    q = q_ref[0, rows, :].astype(F32)
    k = k_ref[0, rows, :].astype(F32)
    v = v_ref[0, rows, :]
    q_mid = q * jnp.exp(g - g_mid)
    k_mid = k * jnp.exp(g_mid - g)
    q_in = q_mid.astype(BF16)
    k_in = k_mid.astype(BF16)
    q_x = (q_mid * jnp.exp(g_mid)).astype(BF16)
    k_end = (k_mid * jnp.exp(g_edge - g_mid)).astype(BF16)
    decay = jnp.exp(g_edge)
    yield
    att = _dot_nt(q_in, k_in)
    s_in = state["s"]
    state["s"] = s_in * decay + _dot_tn(v, k_end)
    yield
    att = jnp.where(tri, att, 0.0).astype(BF16)
    o = _dot(att, v) + _dot_nt(q_x, s_in.astype(BF16))
    yield
    if emit is None:
        o_ref[rows, :] = o
    else:
        emit(rows, o)


def _gla_kernel(q_ref, k_ref, v_ref, og_ref, gf_ref, gb_ref, gain_ref, y_ref,
                of_ref, ob_ref, sf_ref, sb_ref, *, n_chunks, n_ctx_chunks):
    c = GLA_CHUNK
    ri = lax.broadcasted_iota(jnp.int32, (c, c), 0)
    ci = lax.broadcasted_iota(jnp.int32, (c, c), 1)
    lower = ri >= ci
    upper = ci >= ri
    sf_ref[...] = jnp.zeros_like(sf_ref)
    sb_ref[...] = jnp.zeros_like(sb_ref)

    def finish(rows, o):
        y = o * lax.rsqrt(jnp.mean(o * o, axis=-1, keepdims=True) + RMS_EPS) * gain_ref[...]
        y_ref[0, rows, :] = (y * og_ref[0, rows, :].astype(F32)).astype(BF16)

    def scan(it, first, unroll, completing):
        fwd = {"s": sf_ref[...]}
        bwd = {"s": sb_ref[...]}
        emit_f = (lambda rows, o: finish(rows, o + ob_ref[rows, :])) if completing else None
        emit_b = (lambda rows, o: finish(rows, o + of_ref[rows, :])) if completing else None
        chains = []
        for u in range(unroll):
            s = first + it * unroll + u
            nb = jnp.where(s < n_ctx_chunks, n_ctx_chunks - 1 - s, n_chunks - 1 - (s - n_ctx_chunks))
            chains.append(_gla_chain(s, True, q_ref, k_ref, v_ref, gf_ref, of_ref, fwd, lower, emit_f))
            chains.append(_gla_chain(nb, False, q_ref, k_ref, v_ref, gb_ref, ob_ref, bwd, upper, emit_b))
        while chains:
            alive = []
            for ch in chains:
                try:
                    next(ch)
                    alive.append(ch)
                except StopIteration:
                    pass
            chains = alive
        sf_ref[...] = fwd["s"]
        sb_ref[...] = bwd["s"]

    half = (n_chunks - n_ctx_chunks) // 2
    scan(0, 0, n_ctx_chunks, False)
    for n in range(n_ctx_chunks):
        rows = pl.ds(n * c, c)
        finish(rows, of_ref[rows, :] + ob_ref[rows, :])
    lax.fori_loop(0, half // GLA_UNROLL,
                  lambda it, carry: scan(it, n_ctx_chunks, GLA_UNROLL, False) or carry, 0)
    lax.fori_loop(0, half // GLA_UNROLL,
                  lambda it, carry: scan(it, n_ctx_chunks + half, GLA_UNROLL, True) or carry, 0)


def _gla(q, k, v, og, gf, gb, lw, ctx_len):
    bsz, t, _ = q.shape
    assert (t - ctx_len) % (2 * GLA_UNROLL * GLA_CHUNK) == 0 and ctx_len % GLA_CHUNK == 0
    seq_spec = lambda w: pl.BlockSpec((1, t, w), lambda b, h: (b, 0, h))
    kern = functools.partial(_gla_kernel, n_chunks=t // GLA_CHUNK, n_ctx_chunks=ctx_len // GLA_CHUNK)
    return pl.pallas_call(
        kern, grid=(bsz, GLA_HEADS),
        in_specs=[seq_spec(GLA_DK), seq_spec(GLA_DK), seq_spec(GLA_DV), seq_spec(GLA_DV),
                  seq_spec(GLA_DK), seq_spec(GLA_DK), pl.BlockSpec((1, GLA_DV), lambda b, h: (0, 0))],
        out_specs=seq_spec(GLA_DV),
        out_shape=jax.ShapeDtypeStruct((bsz, t, GLA_HEADS * GLA_DV), BF16),
        scratch_shapes=[pltpu.VMEM((t, GLA_DV), F32), pltpu.VMEM((t, GLA_DV), F32),
                        pltpu.VMEM((GLA_DV, GLA_DK), F32), pltpu.VMEM((GLA_DV, GLA_DK), F32)],
        compiler_params=_params(("parallel", "parallel")), name="gla",
    )(q, k, v, og, gf, gb, lw["gla_gain"])


NAT_GROUP = 8
NAT_SAMPLES = 2


def _nat_kernel(q_ref, k0_ref, k1_ref, k2_ref, kc_ref, v0_ref, v1_ref, v2_ref, vc_ref, bias_ref, o_ref, s_ref,
                *, off):
    is_ctx = pl.program_id(0) + off == 0
    lane = lax.broadcasted_iota(jnp.int32, (TM, LANES), 1)
    row = lax.broadcasted_iota(jnp.int32, (LANES, TM), 0)
    n_heads = 2 * NAT_GROUP
    n_lat = NAT_KBLOCKS * TM

    def run(lat_keys, lat_vals):
        kcats, vlats = {}, {}

        def scores(c):
            smp, i = divmod(c, n_heads)
            hp, e = divmod(i, 2)
            sl = slice(hp * LANES, (hp + 1) * LANES)
            if (smp, hp) not in kcats:
                refs = lat_keys + [kc_ref]
                half = (len(refs) + 1) // 2
                kcats[smp, hp] = [jnp.concatenate([r[smp, :, sl] for r in part], axis=0)
                                  for part in (refs[:half], refs[half:]) if part]
            q = q_ref[smp, :, sl]
            qm = jnp.where((lane >= NAT_HD) if e else (lane < NAT_HD), q, jnp.zeros_like(q))
            at = 0
            for kpart in kcats[smp, hp]:
                s_ref[c % 2, at:at + kpart.shape[0], :] = _dot_nt(kpart, qm)
                at += kpart.shape[0]

        scores(0)
        outs = []
        for c in range(NAT_SAMPLES * n_heads):
            if c + 1 < NAT_SAMPLES * n_heads:
                scores(c + 1)
            smp, i = divmod(c, n_heads)
            hp, e = divmod(i, 2)
            sl = slice(hp * LANES, (hp + 1) * LANES)
            n_l = n_lat if lat_keys else 0
            s_ctx = s_ref[c % 2, n_l:n_l + TM, :]
            m = jnp.max(s_ctx, axis=0, keepdims=True)
            if lat_keys:
                s_lat = s_ref[c % 2, :n_lat, :] + bias_ref[i, 0]
                m = jnp.maximum(m, jnp.max(s_lat, axis=0, keepdims=True))
            def own_rows(v):
                vrow = lax.broadcasted_iota(jnp.int32, v.shape, 0)
                return jnp.where((vrow >= NAT_HD) if e else (vrow < NAT_HD), v, jnp.ones_like(v))

            p_ctx = jnp.exp2(s_ctx - m)
            o = _dot(own_rows(vc_ref[smp, sl, :]), p_ctx.astype(BF16))
            if lat_keys:
                if (smp, hp) not in vlats:
                    vlats[smp, hp] = jnp.concatenate([r[smp, sl, :] for r in lat_vals], axis=1)
                p_lat = jnp.exp2(s_lat - m)
                o = o + _dot(own_rows(vlats[smp, hp]), p_lat.astype(BF16))
            l = o[0:1, :] if e else o[LANES - 1:LANES, :]
            outs.append(o * (1.0 / l))
            if e == 1:
                ot = jnp.where(row < NAT_HD, outs[-2], outs[-1])
                o_ref[smp, :, sl] = ot.T.astype(BF16)

    @pl.when(is_ctx)
    def _():
        run([], [])

    @pl.when(jnp.logical_not(is_ctx))
    def _():
        run([k0_ref, k1_ref, k2_ref], [v0_ref, v1_ref, v2_ref])


def _nat(q, k, vt, bias, layer, need_ctx):
    bsz, t, n = q.shape
    off = 0 if need_ctx else 1
    nblk = t // TM - 1
    gw = NAT_GROUP * LANES
    ngroups = n // gw

    def kblock(qi, i):
        j = qi + off - 1
        return 1 + jnp.clip(j - 1, 0, nblk - NAT_KBLOCKS) + i

    def bias_class(qi):
        j = qi + off - 1
        return jnp.where(j <= 0, 0, jnp.where(j == nblk - 1, 2, 1))

    ns = NAT_SAMPLES
    assert bsz % ns == 0
    kspec = lambda i: pl.BlockSpec((ns, TM, gw), lambda qi, hg, b: (b, kblock(qi, i), hg))
    vspec = lambda i: pl.BlockSpec((ns, gw, TM), lambda qi, hg, b: (b, hg, kblock(qi, i)))
    return pl.pallas_call(
        functools.partial(_nat_kernel, off=off),
        grid=(t // TM - off, ngroups, bsz // ns),
        in_specs=[pl.BlockSpec((ns, TM, gw), lambda qi, hg, b: (b, qi + off, hg)),
                  kspec(0), kspec(1), kspec(2),
                  pl.BlockSpec((ns, TM, gw), lambda qi, hg, b: (b, 0, hg)),
                  vspec(0), vspec(1), vspec(2),
                  pl.BlockSpec((ns, gw, TM), lambda qi, hg, b: (b, hg, 0)),
                  pl.BlockSpec((2 * NAT_GROUP, 1, NAT_KBLOCKS * TM, TM),
                               lambda qi, hg, b: (layer * ngroups + hg, bias_class(qi), 0, 0),
                               pipeline_mode=pl.Buffered(1))],
        out_specs=pl.BlockSpec((ns, TM, gw), lambda qi, hg, b: (b, qi, hg)),
        out_shape=jax.ShapeDtypeStruct((bsz, t - off * TM, n), BF16),
        scratch_shapes=[pltpu.VMEM((2, (NAT_KBLOCKS + 1) * TM, TM), F32)],
        compiler_params=_params(("parallel", "parallel", "parallel")), name="nat",
    )(q, k, k, k, k, vt, vt, vt, vt, bias)


def _nat_bias_tables(rpb, rows):
    nblk = rows // NAT_QROWS
    kr_win = min(WIN_R, rows)
    col = np.arange(GRID_W)
    c_start = np.clip(col - WIN_C // 2, 0, GRID_W - WIN_C)
    cmask = (col[:, None] >= c_start[None, :]) & (col[:, None] < c_start[None, :] + WIN_C)
    dc = np.clip(col[:, None] - col[None, :] + WIN_C - 1, 0, 2 * WIN_C - 2)
    drs, rmasks = [], []
    for jblk in (0, 1, nblk - 1):
        base = int(np.clip(jblk - 1, 0, nblk - NAT_KBLOCKS))
        r = NAT_QROWS * jblk + np.arange(NAT_QROWS)
        r_start = np.clip(r - kr_win // 2, 0, rows - kr_win)
        kr = NAT_QROWS * base + np.arange(NAT_KBLOCKS * NAT_QROWS)
        rmasks.append((kr[:, None] >= r_start[None, :]) & (kr[:, None] < r_start[None, :] + kr_win))
        drs.append(np.clip(kr[:, None] - r[None, :] + WIN_R - 1, 0, 2 * WIN_R - 2))
    drx = np.where(np.stack(rmasks), np.stack(drs), 2 * WIN_R - 1)
    dcx = np.where(cmask, dc, 2 * WIN_C - 1)
    nh = rpb.shape[0]
    ext = jnp.full((nh, 2 * WIN_R, 2 * WIN_C), NEG_INF, F32)
    ext = ext.at[:, :2 * WIN_R - 1, :2 * WIN_C - 1].set(rpb.astype(F32) * LOG2E)
    pick = np.tile(dcx[:, None, :], (1, NAT_QROWS, 1)).reshape(-1)
    onehot = jnp.asarray(np.arange(2 * WIN_C)[:, None] == pick[None, :], F32)
    a = jnp.einsum("hdj,jn->hdn", ext, onehot, precision=lax.Precision.HIGHEST)
    a = a.reshape(nh, 2 * WIN_R, GRID_W, NAT_QROWS * GRID_W)
    nk = NAT_KBLOCKS * NAT_QROWS * GRID_W
    return pl.pallas_call(
        functools.partial(_bias_kernel, drx=drx),
        grid=(nh,),
        in_specs=[pl.BlockSpec((1,) + a.shape[1:], lambda h: (h, 0, 0, 0))],
        out_specs=pl.BlockSpec((1, 3, nk, TM), lambda h: (h, 0, 0, 0)),
        out_shape=jax.ShapeDtypeStruct((nh, 3, nk, TM), F32),
        compiler_params=_params(("parallel",)), name="nat_bias",
    )(a)


def _bias_kernel(a_ref, o_ref, *, drx):
    lane_blk = lax.broadcasted_iota(jnp.int32, (GRID_W, TM), 1) // GRID_W
    n_cls, n_kri, n_ri = drx.shape
    for cls in range(n_cls):
        for kri in range(n_kri):
            strip = a_ref[0, int(drx[cls, kri, n_ri - 1])]
            for ri in range(n_ri - 1):
                strip = jnp.where(lane_blk == ri, a_ref[0, int(drx[cls, kri, ri])], strip)
            o_ref[0, cls, kri * GRID_W:(kri + 1) * GRID_W, :] = strip


def _gqa_kernel(q_ref, k_ref, vt_ref, o_ref, s_ref, acc_ref, *, off, n_lat):
    is_ctx = pl.program_id(1) + off == 0
    group = GQA_HEADS // GQA_KV_HEADS

    def run(n_blocks):
        qs = {}

        n_chains = GQA_SAMPLES * GQA_KV_HEADS

        def scores(c, start, size, slot):
            smp, h = divmod(c, GQA_KV_HEADS)
            if c not in qs:
                heads = range(h * group, (h + 1) * group)
                qs[c] = jnp.concatenate([q_ref[smp, :, g * GQA_HD:(g + 1) * GQA_HD] for g in heads], axis=0)
            s_ref[c % 2, slot, :size, :] = _dot_nt(k_ref[smp, start:start + size, h * GQA_HD:(h + 1) * GQA_HD], qs[c])

        def start_chain(c):
            scores(c, 0, TM, 0)
            if n_blocks:
                scores(c, TM, GQA_KB, 1)

        start_chain(0)
        for c in range(n_chains):
            smp, h = divmod(c, GQA_KV_HEADS)
            rows = slice(h * GQA_HD, (h + 1) * GQA_HD)
            if n_blocks == 0 and c + 1 < n_chains:
                start_chain(c + 1)
            st = s_ref[c % 2, 0, :TM, :]
            m = jnp.max(st, axis=0, keepdims=True)
            p = jnp.exp2(st - m)
            l = jnp.sum(p, axis=0, keepdims=True)
            acc_ref[c % 2] = _dot(vt_ref[smp, rows, :TM], p.astype(BF16))
            for j in range(n_blocks):
                start = TM + j * GQA_KB
                if j + 1 < n_blocks:
                    scores(c, start + GQA_KB, GQA_KB, j % 2)
                elif c + 1 < n_chains:
                    start_chain(c + 1)
                st = s_ref[c % 2, (j + 1) % 2]
                m_new = jnp.maximum(m, jnp.max(st, axis=0, keepdims=True))
                alpha = jnp.exp2(m - m_new)
                p = jnp.exp2(st - m_new)
                l = alpha * l + jnp.sum(p, axis=0, keepdims=True)
                acc_ref[c % 2] = alpha * acc_ref[c % 2] + _dot(vt_ref[smp, rows, start:start + GQA_KB],
                                                               p.astype(BF16))
                m = m_new
            ot = acc_ref[c % 2] * (1.0 / l)
            for g in range(group):
                cols = slice((h * group + g) * GQA_HD, (h * group + g + 1) * GQA_HD)
                o_ref[smp, :, cols] = ot[:, g * TM:(g + 1) * TM].T.astype(BF16)

    @pl.when(is_ctx)
    def _():
        run(0)

    @pl.when(jnp.logical_not(is_ctx))
    def _():
        run(n_lat)


def _gqa(q, k, vt, need_ctx):
    bsz, t, nq = q.shape
    nkv = k.shape[2]
    off = 0 if need_ctx else 1
    group = GQA_HEADS // GQA_KV_HEADS
    ns = GQA_SAMPLES
    assert (t - TM) % GQA_KB == 0 and bsz % ns == 0
    return pl.pallas_call(
        functools.partial(_gqa_kernel, off=off, n_lat=(t - TM) // GQA_KB),
        grid=(bsz // ns, t // TM - off),
        in_specs=[pl.BlockSpec((ns, TM, nq), lambda b, qi: (b, qi + off, 0)),
                  pl.BlockSpec((ns, t, nkv), lambda b, qi: (b, 0, 0)),
                  pl.BlockSpec((ns, nkv, t), lambda b, qi: (b, 0, 0))],
        out_specs=pl.BlockSpec((ns, TM, nq), lambda b, qi: (b, qi, 0)),
        out_shape=jax.ShapeDtypeStruct((bsz, t - off * TM, nq), BF16),
        scratch_shapes=[pltpu.VMEM((2, 2, GQA_KB, group * TM), F32),
                        pltpu.VMEM((2, GQA_HD, group * TM), F32)],
        compiler_params=_params(("parallel", "arbitrary")), name="gqa",
    )(q, k, vt)


def _merge_kernel(x_ref, gt_ref, y0_ref, y1_ref, y2_ref, sg_ref, wb_ref, wo_ref, o_ref):
    z = None
    for i, y_ref in enumerate((y0_ref, y1_ref, y2_ref)):
        zi = sg_ref[0, :, i * D_MODEL:(i + 1) * D_MODEL].astype(F32) * _dot(y_ref[0], wb_ref[i])
        z = zi if z is None else z + zi
    o_ref[0] = x_ref[0] + gt_ref[0] * _dot(z.astype(BF16), wo_ref[...])


def _merge(xall, mod, ys, sg, wb, wo, off):
    bsz, t, _ = xall.shape
    ctx_row = mod.shape[0] - 1
    return pl.pallas_call(
        _merge_kernel,
        grid=(bsz, t // TM - off),
        in_specs=[_row_spec(off, D_MODEL), _mod_spec(5, off, ctx_row),
                  _row_spec(off, D_MODEL), _row_spec(0, D_MODEL), _row_spec(0, D_MODEL),
                  _row_spec(off, 3 * D_MODEL), _whole(wb), _whole(wo)],
        out_specs=_row_spec(0, D_MODEL),
        out_shape=jax.ShapeDtypeStruct((bsz, t - off * TM, D_MODEL), F32),
        compiler_params=_params(("parallel", "parallel")), name="merge",
    )(xall, mod, ys[0], ys[1], ys[2], sg, wb, wo)


def _block_diag_ones(head_dim):
    i = np.arange(MXU)
    return jnp.asarray((i[:, None] // head_dim) == (i[None, :] // head_dim), dtype=BF16)


def _chunk_tri(upper):
    i = np.arange(TM)
    same = (i[:, None] // GLA_CHUNK) == (i[None, :] // GLA_CHUNK)
    tri = (i[None, :] >= i[:, None]) if upper else (i[:, None] >= i[None, :])
    return jnp.asarray(same & tri, dtype=BF16)


def _rope_tables(ctx_len, seq):
    quarter = GQA_HD // 4
    freqs = ROPE_BASE ** (-np.arange(quarter, dtype=np.float64) / quarter)
    tok = np.arange(seq)
    ang_r = (tok // GRID_W)[:, None] * freqs
    ang_c = (tok % GRID_W)[:, None] * freqs
    ang = np.concatenate([ang_r, ang_r, ang_c, ang_c], axis=1)
    sign = np.tile(np.concatenate([-np.ones(quarter), np.ones(quarter)]), 2)
    cos = np.concatenate([np.ones((ctx_len, GQA_HD)), np.cos(ang)], axis=0)
    sin = np.concatenate([np.zeros((ctx_len, GQA_HD)), np.sin(ang) * sign], axis=0)
    return jnp.asarray(cos, F32), jnp.asarray(sin, F32)


def _layer_weights(l, ffn_w_in, ffn_w_out, w_in, gla_fg_w2, gla_fg_b, gla_norm_g, nat_q_norm, nat_k_norm,
                   gqa_q_norm, gqa_k_norm, w_branch, w_out):
    w = w_in[l]
    nqk = GLA_HEADS * GLA_DK
    nv = GLA_HEADS * GLA_DV
    o_fg = 2 * nqk + 2 * nv
    o_nat = o_fg + 2 * GLA_LR
    n = NAT_HEADS * NAT_HD
    o_gqa = o_nat + 3 * n
    nq = GQA_HEADS * GQA_HD
    nkv = GQA_KV_HEADS * GQA_HD
    o_gate = o_gqa + nq + 2 * nkv
    pad = jnp.zeros((D_MODEL, LANES - 2 * GLA_LR), w.dtype)
    w2 = gla_fg_w2[l]
    zeros_lr = jnp.zeros((GLA_LR, nqk), w2.dtype)
    zeros_rest = jnp.zeros((LANES - 2 * GLA_LR, nqk), w2.dtype)
    return {
        "ffn": [(ffn_w_in[l, i, :, :D_FF].astype(BF16), ffn_w_in[l, i, :, D_FF:].astype(BF16),
                 ffn_w_out[l, i].astype(BF16)) for i in range(2)],
        "w_gla": w[:, :o_fg].astype(BF16),
        "w_fg": jnp.concatenate([w[:, o_fg:o_nat], pad], axis=1).astype(BF16),
        "w_nat_qk": w[:, o_nat:o_nat + 2 * n].astype(BF16),
        "w_nat_vt": w[:, o_nat + 2 * n:o_gqa].T.astype(BF16),
        "w_gqa_qk": w[:, o_gqa:o_gqa + nq + nkv].astype(BF16),
        "w_gqa_vt": w[:, o_gqa + nq + nkv:o_gate].T.astype(BF16),
        "w_gates": w[:, o_gate:].astype(BF16),
        "w2": jnp.concatenate([jnp.concatenate([w2[0], zeros_lr, zeros_rest], axis=0),
                               jnp.concatenate([zeros_lr, w2[1], zeros_rest], axis=0)], axis=1).astype(BF16),
        "b2": gla_fg_b[l].reshape(1, 2 * nqk),
        "tril": _chunk_tri(False),
        "triu": _chunk_tri(True),
        "gla_gain": gla_norm_g[l].reshape(1, GLA_DV),
        "nat_gq": jnp.tile(nat_q_norm[l], NAT_HEADS).reshape(1, n),
        "nat_gk": jnp.tile(nat_k_norm[l], NAT_HEADS).reshape(1, n),
        "gqa_gq": jnp.tile(gqa_q_norm[l], GQA_HEADS).reshape(1, nq),
        "gqa_gk": jnp.tile(gqa_k_norm[l], GQA_KV_HEADS).reshape(1, nkv),
        "ones64": _block_diag_ones(NAT_HD),
        "ones128": _block_diag_ones(GQA_HD),
        "w_branch": w_branch[l].astype(BF16),
        "w_out": w_out[l].astype(BF16),
    }


def kernel(x, c, ctx, c_ctx, w_mod, b_mod, norm_g, ffn_w_in, ffn_w_out, w_in, gla_fg_w2, gla_fg_b, gla_norm_g,
           nat_q_norm, nat_k_norm, nat_rpb, gqa_q_norm, gqa_k_norm, w_branch, w_out):
    bsz, seq, _ = x.shape
    ctx_len = ctx.shape[1]
    depth = w_mod.shape[0]
    assert seq % TM == 0 and ctx_len == TM and seq // TM >= NAT_KBLOCKS

    mod_rows = -(-(bsz + 1) // SUBLANES) * SUBLANES
    cvec = jnp.concatenate([c, c_ctx[None, :], jnp.zeros((mod_rows - bsz - 1, D_MODEL), c.dtype)], axis=0)
    mods = _mod_table(cvec, w_mod, b_mod)[:, :bsz + 1].reshape(depth, bsz + 1, 1, N_MOD * D_MODEL)
    rope_cos, rope_sin = _rope_tables(ctx_len, seq)
    nat_bias = _nat_bias_tables(nat_rpb.reshape((-1,) + nat_rpb.shape[2:]), seq // GRID_W)

    xall = x
    for l in range(depth):
        need_ctx = l < depth - 1
        off = 0 if need_ctx else 1
        lw = _layer_weights(l, ffn_w_in, ffn_w_out, w_in, gla_fg_w2, gla_fg_b, gla_norm_g, nat_q_norm,
                            nat_k_norm, gqa_q_norm, gqa_k_norm, w_branch, w_out)
        mod = mods[l]
        g = [norm_g[l, i].reshape(1, D_MODEL) for i in range(3)]
        xall = _ffn(xall, mod, g[0], *lw["ffn"][0], sub=0, tile_off=0, ctx=ctx if l == 0 else None)
        (gq, gk, gv, gog, ggf, ggb), (nq_, nk_, nvt), (aq, ak, avt, sg) = _projections(
            xall, mod, g[1], lw, rope_cos, rope_sin)
        y_gla = _gla(gq, gk, gv, gog, ggf, ggb, lw, ctx_len)
        y_nat = _nat(nq_, nk_, nvt, nat_bias, l, need_ctx)
        y_gqa = _gqa(aq, ak, avt, need_ctx)
        xall = _merge(xall, mod, (y_gla, y_nat, y_gqa), sg, lw["w_branch"], lw["w_out"], off)
        xall = _ffn(xall, mod, g[2], *lw["ffn"][1], sub=2, tile_off=off)
    return xall
```
